```python
import jax, jax.numpy as jnp
from jax import lax
import numpy as np

D_MODEL = 4096
BATCH = 1
SEQ = 16384
DEPTH = 1

RET_HEADS = 8
RET_DV = D_MODEL // (2 * RET_HEADS)
RET_DK = RET_DV // 2
RET_QK = RET_HEADS * RET_DK
RET_WIDTH = RET_HEADS * RET_DV
RET_CHUNK = 128
FOX_HEADS = 16
FOX_DH = D_MODEL // (2 * FOX_HEADS)
FOX_WIDTH = FOX_HEADS * FOX_DH
FOX_BLOCK = 128
MIX_WIDTH = RET_WIDTH + FOX_WIDTH
IN_WIDTH = 2 * RET_QK + 2 * RET_WIDTH + 3 * FOX_WIDTH + FOX_HEADS
ROPE_BASE = 10000.0
PEER_HEADS = 8
PEER_NKEYS = 128
PEER_NEXPERTS = PEER_NKEYS * PEER_NKEYS
PEER_DQ = 256
PEER_TOPK = 16
PEER_CHUNK = 64
EPS = 1e-6

kernel_name = 'hybrid_retention_fox_peer_adaln'


def _in_splits():
    sizes = (RET_QK, RET_QK, RET_WIDTH, RET_WIDTH, FOX_WIDTH, FOX_WIDTH, FOX_WIDTH)
    out, acc = [], 0
    for s in sizes:
        acc += s
        out.append(acc)
    return out


def rmsnorm(x, g):
    xf = x.astype(jnp.float32)
    y = xf * lax.rsqrt(jnp.mean(xf * xf, axis=-1, keepdims=True) + EPS)
    return (y * g.astype(jnp.float32)).astype(x.dtype)


def modulate(h, shift, scale):
    return h * (1.0 + scale[:, None, :]) + shift[:, None, :]


def rotary(x, positions):
    half = x.shape[-1] // 2
    inv_freq = ROPE_BASE ** (-jnp.arange(half, dtype=jnp.float32) / half)
    ang = positions.astype(jnp.float32)[..., None] * inv_freq
    cos = jnp.cos(ang)[:, :, None, :]
    sin = jnp.sin(ang)[:, :, None, :]
    xf = x.astype(jnp.float32)
    x1, x2 = xf[..., :half], xf[..., half:]
    return jnp.concatenate([x1 * cos - x2 * sin, x1 * sin + x2 * cos], axis=-1)


def retention_chunkwise(q, k, v, positions):
    B, S, H, dk = q.shape
    dv = v.shape[-1]
    C = RET_CHUNK
    nc = S // C
    q = rotary(q, positions)
    k = rotary(k, positions) * (dk ** -0.5)
    v = v.astype(jnp.float32)
    log_g = jnp.log1p(-jnp.exp2(-5.0 - jnp.arange(H, dtype=jnp.float32)))

    def chunks(t):
        return t.reshape(B, nc, C, H, t.shape[-1]).transpose(0, 3, 1, 2, 4)

    qc, kc, vc = chunks(q), chunks(k), chunks(v)
    i = jnp.arange(C, dtype=jnp.float32)
    diff = i[:, None] - i[None, :]
    decay_in = jnp.where(diff >= 0, jnp.exp(jnp.maximum(diff, 0.0)[None] * log_g[:, None, None]), 0.0)
    scores = jnp.einsum('bhncd,bhnsd->bhncs', qc, kc) * decay_in[None, :, None]
    intra = jnp.einsum('bhncs,bhnse->bhnce', scores, vc)
    k_dec = kc * jnp.exp((C - 1.0 - i)[None, :] * log_g[:, None])[None, :, None, :, None]
    kv = jnp.einsum('bhncd,bhnce->nbhde', k_dec, vc)
    chunk_decay = jnp.exp(C * log_g)[None, :, None, None]

    def step(state, kv_n):
        return state * chunk_decay + kv_n, state

    _, state_prev = lax.scan(step, jnp.zeros((B, H, dk, dv), jnp.float32), kv)
    q_dec = qc * jnp.exp((i + 1.0)[None, :] * log_g[:, None])[None, :, None, :, None]
    cross = jnp.einsum('bhncd,nbhde->bhnce', q_dec, state_prev)
    return (intra + cross).transpose(0, 2, 3, 1, 4).reshape(B, S, H, dv)


def forgetting_attention(q, k, v, f_logit, b_forget):
    B, S, H, dh = q.shape
    nq = S // FOX_BLOCK
    scale = dh ** -0.5
    log_f = jax.nn.log_sigmoid(f_logit.astype(jnp.float32) + b_forget.astype(jnp.float32))
    cum = jnp.cumsum(log_f, axis=1).transpose(0, 2, 1)
    pos = jnp.arange(S)
    q_blocks = q.reshape(B, nq, FOX_BLOCK, H, dh).transpose(1, 0, 2, 3, 4)
    c_blocks = cum.reshape(B, H, nq, FOX_BLOCK).transpose(2, 0, 1, 3)
    p_blocks = pos.reshape(nq, FOX_BLOCK)

    def block(args):
        q_i, c_i, p_i = args
        logits = jnp.einsum('bqhd,bkhd->bhqk', q_i, k, preferred_element_type=jnp.float32) * scale
        logits = logits + (c_i[..., :, None] - cum[..., None, :])
        causal = p_i[:, None] >= pos[None, :]
        probs = jax.nn.softmax(jnp.where(causal, logits, -jnp.inf), axis=-1)
        return jnp.einsum('bhqk,bkhd->bqhd', probs.astype(v.dtype), v)

    out = lax.map(block, (q_blocks, c_blocks, p_blocks))
    return out.transpose(1, 0, 2, 3, 4).reshape(B, S, H * dh)


def peer_ffn(h, w_q, sub_keys, u, v):
    B, S, D = h.shape
    T = B * S
    ht = h.reshape(T, D)
    q = (ht @ w_q).reshape(T, PEER_HEADS, 2, PEER_DQ // 2)
    s = jnp.einsum('thpd,hpkd->thpk', q, sub_keys, preferred_element_type=jnp.float32)
    s_top, i_top = lax.top_k(s, PEER_TOPK)
    cand = (s_top[:, :, 0, :, None] + s_top[:, :, 1, None, :]).reshape(T, PEER_HEADS, PEER_TOPK * PEER_TOPK)
    c_top, c_idx = lax.top_k(cand, PEER_TOPK)
    i1 = jnp.take_along_axis(i_top[:, :, 0], c_idx // PEER_TOPK, axis=-1)
    i2 = jnp.take_along_axis(i_top[:, :, 1], c_idx % PEER_TOPK, axis=-1)
    expert = (i1 * PEER_NKEYS + i2).reshape(T, PEER_HEADS * PEER_TOPK)
    gates = jax.nn.softmax(c_top, axis=-1).reshape(T, PEER_HEADS * PEER_TOPK)
    nch = T // PEER_CHUNK

    def block(args):
        h_c, e_c, g_c = args
        u_sel = u[e_c]
        a = jnp.einsum('td,ted->te', h_c, u_sel, preferred_element_type=jnp.float32)
        a = jax.nn.gelu(a) * g_c
        v_sel = v[e_c]
        return jnp.einsum('te,ted->td', a.astype(v_sel.dtype), v_sel)

    out = lax.map(block, (ht.reshape(nch, PEER_CHUNK, D),
                          expert.reshape(nch, PEER_CHUNK, PEER_HEADS * PEER_TOPK),
                          gates.reshape(nch, PEER_CHUNK, PEER_HEADS * PEER_TOPK)))
    return out.reshape(B, S, D).astype(h.dtype)


def setup_inputs(seed: int = 0) -> dict:
    key = jax.random.key(seed)
    ks = jax.random.split(key, 16)
    nrm = jax.random.normal
    x = nrm(ks[0], (BATCH, SEQ, D_MODEL), jnp.float32)
    c = nrm(ks[1], (BATCH, D_MODEL), jnp.float32)
    positions = jnp.broadcast_to(jnp.arange(SEQ, dtype=jnp.int32)[None, :], (BATCH, SEQ))
    w_ada = nrm(ks[2], (DEPTH, D_MODEL, 6 * D_MODEL), jnp.float32) * (0.3 * D_MODEL ** -0.5)
    b_ada = 0.01 * nrm(ks[3], (DEPTH, 6 * D_MODEL), jnp.float32)
    norm1_g = 1.0 + 0.02 * nrm(ks[4], (DEPTH, D_MODEL), jnp.float32)
    w_in = nrm(ks[5], (DEPTH, D_MODEL, IN_WIDTH), jnp.float32) * D_MODEL ** -0.5
    b_forget = jnp.linspace(1.0, 5.0, FOX_HEADS, dtype=jnp.float32)[None, :] + 0.1 * nrm(ks[6], (DEPTH, FOX_HEADS), jnp.float32)
    ret_gn_g = 1.0 + 0.02 * nrm(ks[7], (DEPTH, RET_WIDTH), jnp.float32)
    fox_norm_g = 1.0 + 0.02 * nrm(ks[8], (DEPTH, FOX_WIDTH), jnp.float32)
    w_out = nrm(ks[9], (DEPTH, MIX_WIDTH, D_MODEL), jnp.float32) * MIX_WIDTH ** -0.5
    norm2_g = 1.0 + 0.02 * nrm(ks[10], (DEPTH, D_MODEL), jnp.float32)
    w_peer_q = nrm(ks[11], (DEPTH, D_MODEL, PEER_HEADS * PEER_DQ), jnp.float32) * D_MODEL ** -0.5
    peer_sub_keys = nrm(ks[12], (DEPTH, PEER_HEADS, 2, PEER_NKEYS, PEER_DQ // 2), jnp.float32) * (PEER_DQ // 2) ** -0.5
    peer_u = nrm(ks[13], (DEPTH, PEER_NEXPERTS, D_MODEL), jnp.float32) * D_MODEL ** -0.5
    peer_v = nrm(ks[14], (DEPTH, PEER_NEXPERTS, D_MODEL), jnp.float32) * 0.5
    final_g = 1.0 + 0.02 * nrm(ks[15], (D_MODEL,), jnp.float32)
    return {'x': x, 'c': c, 'positions': positions, 'w_ada': w_ada, 'b_ada': b_ada,
            'norm1_g': norm1_g, 'w_in': w_in, 'b_forget': b_forget, 'ret_gn_g': ret_gn_g,
            'fox_norm_g': fox_norm_g, 'w_out': w_out, 'norm2_g': norm2_g, 'w_peer_q': w_peer_q,
            'peer_sub_keys': peer_sub_keys, 'peer_u': peer_u, 'peer_v': peer_v, 'final_g': final_g}


def reference(x, c, positions, w_ada, b_ada, norm1_g, w_in, b_forget, ret_gn_g, fox_norm_g,
              w_out, norm2_g, w_peer_q, peer_sub_keys, peer_u, peer_v, final_g):
    B, S, D = x.shape
    c_act = jax.nn.silu(c)
    splits = _in_splits()
    for l in range(DEPTH):
        mod = c_act @ w_ada[l] + b_ada[l]
        shift1, scale1, gate1, shift2, scale2, gate2 = jnp.split(mod, 6, axis=-1)
        h = modulate(rmsnorm(x, norm1_g[l]), shift1, scale1)
        proj = h @ w_in[l]
        rq, rk, rv, rg, fq, fk, fv, ff = jnp.split(proj, splits, axis=-1)
        y_ret = retention_chunkwise(rq.reshape(B, S, RET_HEADS, RET_DK), rk.reshape(B, S, RET_HEADS, RET_DK),
                                    rv.reshape(B, S, RET_HEADS, RET_DV), positions)
        mu = jnp.mean(y_ret, axis=-1, keepdims=True)
        var = jnp.mean(jnp.square(y_ret - mu), axis=-1, keepdims=True)
        y_ret = ((y_ret - mu) * lax.rsqrt(var + EPS)).reshape(B, S, RET_WIDTH)
        y_ret = y_ret * ret_gn_g[l].astype(jnp.float32) * jax.nn.silu(rg.astype(jnp.float32))
        y_fox = forgetting_attention(fq.reshape(B, S, FOX_HEADS, FOX_DH), fk.reshape(B, S, FOX_HEADS, FOX_DH),
                                     fv.reshape(B, S, FOX_HEADS, FOX_DH), ff, b_forget[l])
        y_fox = rmsnorm(y_fox, fox_norm_g[l])
        mixed = jnp.concatenate([y_ret.astype(x.dtype), y_fox.astype(x.dtype)], axis=-1) @ w_out[l]
        x = x + gate1[:, None, :] * mixed
        h = modulate(rmsnorm(x, norm2_g[l]), shift2, scale2)
        x = x + gate2[:, None, :] * peer_ffn(h, w_peer_q[l], peer_sub_keys[l], peer_u[l], peer_v[l])
    return rmsnorm(x, final_g)
```

```python
import functools

import numpy as np
import jax
import jax.numpy as jnp
from jax import lax
from jax.experimental import pallas as pl
from jax.experimental.pallas import tpu as pltpu

F32 = jnp.float32
BF16 = jnp.bfloat16

LANES = 128
RET_HEADS = 8
RET_DK = 128
RET_DV = 256
RET_QK = RET_HEADS * RET_DK
RET_WIDTH = RET_HEADS * RET_DV
RET_CHUNK = 128
FOX_HEADS = 16
FOX_DH = 128
FOX_WIDTH = FOX_HEADS * FOX_DH
IN_MAIN = 2 * RET_QK + 2 * RET_WIDTH + 3 * FOX_WIDTH
ROPE_BASE = 10000.0
PEER_HEADS = 8
PEER_NKEYS = 128
PEER_NEXPERTS = PEER_NKEYS * PEER_NKEYS
PEER_DQ = 256
PEER_TOPK = 16
EPS = 1e-6
NEG_BIG = -1e30
VMEM_LIMIT = 56 * 1024 * 1024

_NT = (((1,), (1,)), ((), ()))
_TN = (((0,), (0,)), ((), ()))


def _params(sem):
    return pltpu.CompilerParams(dimension_semantics=sem, vmem_limit_bytes=VMEM_LIMIT)


def _dot(a, b):
    return jnp.dot(a, b, preferred_element_type=F32)


def _dot_nt(a, b):
    return lax.dot_general(a, b, _NT, preferred_element_type=F32)


def _ada_kernel(c_ref, w_ref, b_ref, o_ref):
    c = c_ref[...]
    ca = (c / (1.0 + jnp.exp(-c))).astype(BF16)
    o_ref[...] = _dot(ca, w_ref[...].astype(BF16)) + b_ref[...]


def ada_mod(c8, w, b, *, bn=512):
    d, n = w.shape
    return pl.pallas_call(
        _ada_kernel,
        grid=(n // bn,),
        in_specs=[pl.BlockSpec((8, d), lambda j: (0, 0)),
                  pl.BlockSpec((d, bn), lambda j: (0, j)),
                  pl.BlockSpec((1, bn), lambda j: (0, j))],
        out_specs=pl.BlockSpec((8, bn), lambda j: (0, j)),
        out_shape=jax.ShapeDtypeStruct((8, n), F32),
        compiler_params=_params(("arbitrary",)),
        name="ada",
    )(c8, w, b)


def _norm_mod(x, g, scale, shift):
    ms = jnp.mean(x * x, axis=-1, keepdims=True)
    y = x * lax.rsqrt(ms + EPS) * g
    return y * (1.0 + scale) + shift


def _inpj_kernel(x_ref, g_ref, sc_ref, sh_ref, w_ref, wf_ref, o_ref, of_ref, h_scr):
    @pl.when(pl.program_id(1) == 0)
    def _():
        hb = _norm_mod(x_ref[...], g_ref[...], sc_ref[...], sh_ref[...]).astype(BF16)
        h_scr[...] = hb
        of_ref[...] = _dot(hb, wf_ref[...])

    o_ref[...] = _dot(h_scr[...], w_ref[...])


def in_proj(x, g, scale, shift, w_main, w_ff, *, tm=512, tn=1024):
    s, d = x.shape
    n = w_main.shape[1]
    vec = pl.BlockSpec((1, d), lambda i, j: (0, 0))
    return pl.pallas_call(
        _inpj_kernel,
        grid=(s // tm, n // tn),
        in_specs=[pl.BlockSpec((tm, d), lambda i, j: (i, 0)), vec, vec, vec,
                  pl.BlockSpec((d, tn), lambda i, j: (0, j)),
                  pl.BlockSpec((d, LANES), lambda i, j: (0, 0))],
        out_specs=[pl.BlockSpec((tm, tn), lambda i, j: (i, j)),
                   pl.BlockSpec((tm, LANES), lambda i, j: (i, 0))],
        out_shape=[jax.ShapeDtypeStruct((s, n), F32),
                   jax.ShapeDtypeStruct((s, LANES), F32)],
        scratch_shapes=[pltpu.VMEM((tm, d), BF16)],
        compiler_params=_params(("arbitrary", "arbitrary")),
        name="inpj",
    )(x, g, scale, shift, w_main, w_ff)


def _split3(v):
    hi = v.astype(BF16)
    r1 = v - hi.astype(F32)
    mid = r1.astype(BF16)
    lo = (r1 - mid.astype(F32)).astype(BF16)
    return hi, mid, lo


def _cum_kernel(ff_ref, b_ref, o_ref, carry):
    @pl.when(pl.program_id(0) == 0)
    def _():
        carry[...] = jnp.zeros_like(carry)

    z = ff_ref[...] + b_ref[...]
    logf = jnp.minimum(z, 0.0) - jnp.log(1.0 + jnp.exp(-jnp.abs(z)))
    tc = z.shape[0]
    row = lax.broadcasted_iota(jnp.int32, (tc, tc), 0)
    col = lax.broadcasted_iota(jnp.int32, (tc, tc), 1)
    tri = jnp.where(row >= col, 1.0, 0.0).astype(BF16)
    hi, mid, lo = _split3(logf)
    cum = _dot(tri, hi) + _dot(tri, mid) + _dot(tri, lo) + carry[...]
    carry[...] = cum[tc - 1:tc, :]
    chi, cmid, clo = _split3(cum)
    o_ref[:, 0:LANES] = chi.astype(F32)
    o_ref[:, LANES:2 * LANES] = cmid.astype(F32)
    o_ref[:, 2 * LANES:3 * LANES] = clo.astype(F32)


def cum_gate(ff, b128, *, tc=256):
    s = ff.shape[0]
    return pl.pallas_call(
        _cum_kernel,
        grid=(s // tc,),
        in_specs=[pl.BlockSpec((tc, LANES), lambda i: (i, 0)),
                  pl.BlockSpec((1, LANES), lambda i: (0, 0))],
        out_specs=pl.BlockSpec((tc, 3 * LANES), lambda i: (i, 0)),
        out_shape=jax.ShapeDtypeStruct((s, 3 * LANES), F32),
        scratch_shapes=[pltpu.VMEM((1, LANES), F32)],
        compiler_params=_params(("arbitrary",)),
        name="cum",
    )(ff, b128)


def _ret_consts():
    h = np.arange(RET_HEADS, dtype=np.float32)
    log_g = np.log1p(-np.exp2(-5.0 - h)).astype(np.float32)
    i = np.arange(RET_CHUNK, dtype=np.float32)
    diff = i[:, None] - i[None, :]
    dmat = np.where(diff >= 0, np.exp(np.maximum(diff, 0.0)[None] * log_g[:, None, None]), 0.0)
    kdec = np.exp((RET_CHUNK - 1.0 - i)[None, :] * log_g[:, None])
    qdec = np.exp((i + 1.0)[None, :] * log_g[:, None])
    cdec = np.exp(RET_CHUNK * log_g)
    kdec = np.broadcast_to(kdec[:, :, None], (RET_HEADS, RET_CHUNK, RET_DK))
    qdec = np.broadcast_to(qdec[:, :, None], (RET_HEADS, RET_CHUNK, RET_DK))
    half = RET_DK // 2
    inv = (ROPE_BASE ** (-np.arange(half, dtype=np.float32) / half)).astype(np.float32)
    inv_full = np.concatenate([inv, inv])[None, :]
    sign = np.concatenate([-np.ones(half, np.float32), np.ones(half, np.float32)])[None, :]
    return (dmat.astype(np.float32), np.ascontiguousarray(kdec, np.float32),
            np.ascontiguousarray(qdec, np.float32), [float(v) for v in cdec],
            inv_full.astype(np.float32), sign)


def _ret_kernel(cdec, rq_ref, rk_ref, rv_ref, rg_ref, pos_ref, inv_ref, sign_ref,
                dmat_ref, kdec_ref, qdec_ref, gn_ref, o_ref, state):
    @pl.when(pl.program_id(0) == 0)
    def _():
        state[...] = jnp.zeros_like(state)

    ang = pos_ref[...].astype(F32) * inv_ref[...]
    cosf = jnp.cos(ang)
    sinf = jnp.sin(ang) * sign_ref[...]
    kscale = RET_DK ** -0.5
    for h in range(RET_HEADS):
        q = rq_ref[:, h * RET_DK:(h + 1) * RET_DK]
        k = rk_ref[:, h * RET_DK:(h + 1) * RET_DK]
        qr = q * cosf + pltpu.roll(q, RET_DK // 2, 1) * sinf
        kr = (k * cosf + pltpu.roll(k, RET_DK // 2, 1) * sinf) * kscale
        vb = rv_ref[:, h * RET_DV:(h + 1) * RET_DV].astype(BF16)
        scores = _dot_nt(qr.astype(BF16), kr.astype(BF16)) * dmat_ref[h]
        intra = _dot(scores.astype(BF16), vb)
        st = state[h]
        cross = _dot((qr * qdec_ref[h]).astype(BF16), st.astype(BF16))
        kd = (kr * kdec_ref[h]).astype(BF16)
        kv = lax.dot_general(kd, vb, _TN, preferred_element_type=F32)
        state[h] = st * cdec[h] + kv
        y = intra + cross
        mu = jnp.mean(y, axis=-1, keepdims=True)
        dlt = y - mu
        var = jnp.mean(dlt * dlt, axis=-1, keepdims=True)
        g = rg_ref[:, h * RET_DV:(h + 1) * RET_DV]
        yn = dlt * lax.rsqrt(var + EPS) * gn_ref[:, h * RET_DV:(h + 1) * RET_DV] * (g / (1.0 + jnp.exp(-g)))
        o_ref[:, h * RET_DV:(h + 1) * RET_DV] = yn.astype(BF16)


def retention(proj, pos_col, gn_g):
    s = proj.shape[0]
    c = RET_CHUNK
    dmat, kdec, qdec, cdec, inv_full, sign = _ret_consts()
    full3 = lambda n: (0, 0, 0)
    return pl.pallas_call(
        functools.partial(_ret_kernel, cdec),
        grid=(s // c,),
        in_specs=[pl.BlockSpec((c, RET_QK), lambda n: (n, 0)),
                  pl.BlockSpec((c, RET_QK), lambda n: (n, 1)),
                  pl.BlockSpec((c, RET_WIDTH), lambda n: (n, 1)),
                  pl.BlockSpec((c, RET_WIDTH), lambda n: (n, 2)),
                  pl.BlockSpec((c, 1), lambda n: (n, 0)),
                  pl.BlockSpec((1, RET_DK), lambda n: (0, 0)),
                  pl.BlockSpec((1, RET_DK), lambda n: (0, 0)),
                  pl.BlockSpec((RET_HEADS, c, c), full3),
                  pl.BlockSpec((RET_HEADS, c, RET_DK), full3),
                  pl.BlockSpec((RET_HEADS, c, RET_DK), full3),
                  pl.BlockSpec((1, RET_WIDTH), lambda n: (0, 0))],
        out_specs=pl.BlockSpec((c, RET_WIDTH), lambda n: (n, 0)),
        out_shape=jax.ShapeDtypeStruct((s, RET_WIDTH), BF16),
        scratch_shapes=[pltpu.VMEM((RET_HEADS, RET_DK, RET_DV), F32)],
        compiler_params=_params(("arbitrary",)),
        name="ret",
    )(proj, proj, proj, proj, pos_col, jnp.asarray(inv_full), jnp.asarray(sign),
      jnp.asarray(dmat), jnp.asarray(kdec), jnp.asarray(qdec), gn_g)


def _fox_kernel(bk, q_ref, k_ref, c_ref, vt_ref, o_ref, m_scr, l_scr, acc_scr):
    i = pl.program_id(1)
    bq = q_ref.shape[0]
    lane = lax.broadcasted_iota(jnp.int32, (bq, FOX_DH), 1)
    qa = jnp.where(lane < 3, -1.0, 0.0).astype(BF16)
    q2 = jnp.concatenate([(q_ref[...] * (FOX_DH ** -0.5)).astype(BF16), qa], axis=1)
    m_scr[...] = jnp.full_like(m_scr, NEG_BIG)
    l_scr[...] = jnp.zeros_like(l_scr)
    acc_scr[...] = jnp.zeros_like(acc_scr)

    def step(j, masked):
        off = pl.multiple_of(j * bk, bk)
        k2 = jnp.concatenate([k_ref[pl.ds(off, bk), :], c_ref[0, pl.ds(off, bk), :]], axis=1)
        st = _dot_nt(k2, q2)
        if masked:
            kid = lax.broadcasted_iota(jnp.int32, (bk, bq), 0)
            qid = lax.broadcasted_iota(jnp.int32, (bk, bq), 1)
            st = jnp.where(kid <= qid, st, NEG_BIG)
        m_old = m_scr[...]
        m_new = jnp.maximum(m_old, jnp.max(st, axis=0, keepdims=True))
        alpha = jnp.exp(m_old - m_new)
        p = jnp.exp(st - m_new)
        l_scr[...] = alpha * l_scr[...] + jnp.sum(p, axis=0, keepdims=True)
        acc_scr[...] = alpha * acc_scr[...] + _dot(vt_ref[0, j], p.astype(BF16))
        m_scr[...] = m_new

    def body(j, carry):
        step(j, False)
        return carry

    lax.fori_loop(0, i, body, 0)
    step(i, True)
    o_ref[...] = (acc_scr[...] / l_scr[...]).T


def fox_attention(proj, kb, caug, vt4, *, bq=512):
    s = proj.shape[0]
    nq = s // bq
    qcol0 = (2 * RET_QK + 2 * RET_WIDTH) // FOX_DH
    return pl.pallas_call(
        functools.partial(_fox_kernel, bq),
        grid=(FOX_HEADS, nq),
        in_specs=[pl.BlockSpec((bq, FOX_DH), lambda h, i: (i, qcol0 + h)),
                  pl.BlockSpec((s, FOX_DH), lambda h, i: (0, h)),
                  pl.BlockSpec((1, s, LANES), lambda h, i: (h, 0, 0)),
                  pl.BlockSpec((1, nq, FOX_DH, bq), lambda h, i: (h, 0, 0, 0))],
        out_specs=pl.BlockSpec((bq, FOX_DH), lambda h, i: (i, h)),
        out_shape=jax.ShapeDtypeStruct((s, FOX_WIDTH), F32),
        scratch_shapes=[pltpu.VMEM((1, bq), F32), pltpu.VMEM((1, bq), F32),
                        pltpu.VMEM((FOX_DH, bq), F32)],
        compiler_params=_params(("arbitrary", "arbitrary")),
        name="fox",
    )(proj, kb, caug, vt4)


def _outp_kernel(yr_ref, yf_ref, fg_ref, w_ref, x_ref, g1_ref, o_ref, y_scr):
    @pl.when(pl.program_id(1) == 0)
    def _():
        yf = yf_ref[...]
        ms = jnp.mean(yf * yf, axis=-1, keepdims=True)
        yn = yf * lax.rsqrt(ms + EPS) * fg_ref[...]
        y_scr[:, 0:RET_WIDTH] = yr_ref[...]
        y_scr[:, RET_WIDTH:RET_WIDTH + FOX_WIDTH] = yn.astype(BF16)

    o_ref[...] = x_ref[...] + g1_ref[...] * _dot(y_scr[...], w_ref[...])


def out_proj(y_ret, y_fox, fox_g, w_out, x, gate1, *, tm=512, tn=1024):
    s, d = x.shape
    kdim = RET_WIDTH + FOX_WIDTH
    return pl.pallas_call(
        _outp_kernel,
        grid=(s // tm, d // tn),
        in_specs=[pl.BlockSpec((tm, RET_WIDTH), lambda i, j: (i, 0)),
                  pl.BlockSpec((tm, FOX_WIDTH), lambda i, j: (i, 0)),
                  pl.BlockSpec((1, FOX_WIDTH), lambda i, j: (0, 0)),
                  pl.BlockSpec((kdim, tn), lambda i, j: (0, j)),
                  pl.BlockSpec((tm, tn), lambda i, j: (i, j)),
                  pl.BlockSpec((1, tn), lambda i, j: (0, j))],
        out_specs=pl.BlockSpec((tm, tn), lambda i, j: (i, j)),
        out_shape=jax.ShapeDtypeStruct((s, d), F32),
        scratch_shapes=[pltpu.VMEM((tm, kdim), BF16)],
        compiler_params=_params(("arbitrary", "arbitrary")),
        name="outp",
    )(y_ret, y_fox, fox_g, w_out, x, gate1)


def _top_rows(s, t_scr):
    cur = s
    for r in range(PEER_TOPK):
        mx = jnp.max(cur, axis=0, keepdims=True)
        t_scr[r:r + 1, :] = mx
        if r + 1 < PEER_TOPK:
            cur = jnp.where(cur == mx, -jnp.inf, cur)


def _peerq_kernel(x_ref, g_ref, sc_ref, sh_ref, wq_ref, keys_ref,
                  h2_ref, s1_ref, s2_ref, tau_ref, mz_ref, h_scr, t1_scr, t2_scr, cand_scr):
    @pl.when(pl.program_id(1) == 0)
    def _():
        hb = _norm_mod(x_ref[...], g_ref[...], sc_ref[...], sh_ref[...]).astype(BF16)
        h_scr[...] = hb
        h2_ref[...] = hb

    half = PEER_DQ // 2
    qb = _dot(h_scr[...], wq_ref[...]).astype(BF16)
    s1 = _dot_nt(keys_ref[0, 0], qb[:, 0:half])
    s2 = _dot_nt(keys_ref[0, 1], qb[:, half:PEER_DQ])
    s1_ref[0] = s1
    s2_ref[0] = s2
    _top_rows(s1, t1_scr)
    _top_rows(s2, t2_scr)
    t2 = t2_scr[...]
    for a in range(PEER_TOPK):
        cand_scr[a * PEER_TOPK:(a + 1) * PEER_TOPK, :] = t1_scr[a:a + 1, :] + t2
    cand = cand_scr[...]
    cur = cand
    top = None
    for r in range(PEER_TOPK):
        mx = jnp.max(cur, axis=0, keepdims=True)
        if r == 0:
            top = mx
        if r + 1 < PEER_TOPK:
            cur = jnp.where(cur == mx, -jnp.inf, cur)
    tau = mx
    z = jnp.sum(jnp.where(cand >= tau, jnp.exp(cand - top), 0.0), axis=0, keepdims=True)
    tau_ref[0] = tau
    mz_ref[0] = top + jnp.log(z)


def peer_query(x1, g, scale, shift, wq, keys, *, tm=512):
    s, d = x1.shape
    vec = pl.BlockSpec((1, d), lambda i, h: (0, 0))
    stat = pl.BlockSpec((1, PEER_NKEYS, tm), lambda i, h: (h, 0, i))
    row = pl.BlockSpec((1, 1, tm), lambda i, h: (h, 0, i))
    return pl.pallas_call(
        _peerq_kernel,
        grid=(s // tm, PEER_HEADS),
        in_specs=[pl.BlockSpec((tm, d), lambda i, h: (i, 0)), vec, vec, vec,
                  pl.BlockSpec((d, PEER_DQ), lambda i, h: (0, h)),
                  pl.BlockSpec((1, 2, PEER_NKEYS, PEER_DQ // 2), lambda i, h: (h, 0, 0, 0))],
        out_specs=[pl.BlockSpec((tm, d), lambda i, h: (i, 0)), stat, stat, row, row],
        out_shape=[jax.ShapeDtypeStruct((s, d), BF16),
                   jax.ShapeDtypeStruct((PEER_HEADS, PEER_NKEYS, s), F32),
                   jax.ShapeDtypeStruct((PEER_HEADS, PEER_NKEYS, s), F32),
                   jax.ShapeDtypeStruct((PEER_HEADS, 1, s), F32),
                   jax.ShapeDtypeStruct((PEER_HEADS, 1, s), F32)],
        scratch_shapes=[pltpu.VMEM((tm, d), BF16),
                        pltpu.VMEM((PEER_TOPK, tm), F32), pltpu.VMEM((PEER_TOPK, tm), F32),
                        pltpu.VMEM((PEER_TOPK * PEER_TOPK, tm), F32)],
        compiler_params=_params(("arbitrary", "arbitrary")),
        name="peerq",
    )(x1, g, scale, shift, wq, keys)


def _gelu_tanh(x):
    return 0.5 * x * (1.0 + jnp.tanh(0.7978845608028654 * (x + 0.044715 * (x * x * x))))


def _peer_kernel(h2_ref, u_ref, vt_ref, s1_ref, s2_ref, tau_ref, mz_ref, o_ref, w_scr):
    n = pl.program_id(1)

    @pl.when(n == 0)
    def _():
        o_ref[...] = jnp.zeros_like(o_ref)

    tn = u_ref.shape[0]
    nrow = tn // PEER_NKEYS
    for c in range(nrow):
        i1 = n * nrow + c
        a = _dot_nt(u_ref[c * PEER_NKEYS:(c + 1) * PEER_NKEYS, :], h2_ref[...])
        gate = jnp.zeros_like(a)
        for h in range(PEER_HEADS):
            sm = s2_ref[h] + s1_ref[h, pl.ds(i1, 1), :]
            gate = gate + jnp.where(sm >= tau_ref[h], jnp.exp(sm - mz_ref[h]), 0.0)
        w_scr[c * PEER_NKEYS:(c + 1) * PEER_NKEYS, :] = (_gelu_tanh(a) * gate).astype(BF16)
    o_ref[...] += _dot(vt_ref[...], w_scr[...])


def peer_experts(h2, u, vt, s1, s2, tau, mz, *, tm=512, tn=512):
    s, d = h2.shape
    ne = u.shape[0]
    stat = pl.BlockSpec((PEER_HEADS, PEER_NKEYS, tm), lambda i, n: (0, 0, i))
    row = pl.BlockSpec((PEER_HEADS, 1, tm), lambda i, n: (0, 0, i))
    return pl.pallas_call(
        _peer_kernel,
        grid=(s // tm, ne // tn),
        in_specs=[pl.BlockSpec((tm, d), lambda i, n: (i, 0)),
                  pl.BlockSpec((tn, d), lambda i, n: (n, 0)),
                  pl.BlockSpec((d, tn), lambda i, n: (0, n)),
                  stat, stat, row, row],
        out_specs=pl.BlockSpec((d, tm), lambda i, n: (0, i)),
        out_shape=jax.ShapeDtypeStruct((d, s), F32),
        scratch_shapes=[pltpu.VMEM((tn, tm), BF16)],
        compiler_params=_params(("arbitrary", "arbitrary")),
        name="peer",
    )(h2, u, vt, s1, s2, tau, mz)


def _fin_kernel(x_ref, pt_ref, g2_ref, fg_ref, o_ref):
    x2 = x_ref[...] + g2_ref[...] * pt_ref[...].T
    ms = jnp.mean(x2 * x2, axis=-1, keepdims=True)
    o_ref[...] = x2 * lax.rsqrt(ms + EPS) * fg_ref[...]


def final_norm(x1, peer_t, gate2, final_g, *, tm=256):
    s, d = x1.shape
    vec = pl.BlockSpec((1, d), lambda i: (0, 0))
    return pl.pallas_call(
        _fin_kernel,
        grid=(s // tm,),
        in_specs=[pl.BlockSpec((tm, d), lambda i: (i, 0)),
                  pl.BlockSpec((d, tm), lambda i: (0, i)), vec, vec],
        out_specs=pl.BlockSpec((tm, d), lambda i: (i, 0)),
        out_shape=jax.ShapeDtypeStruct((s, d), F32),
        compiler_params=_params(("arbitrary",)),
        name="fin",
    )(x1, peer_t, gate2, final_g)


def _layer(x, mod, positions, norm1_g, w_in, b_forget, ret_gn_g, fox_norm_g, w_out, norm2_g,
           w_peer_q, peer_sub_keys, peer_u, peer_v, *, bq=512):
    s, d = x.shape
    shift1, scale1, gate1, shift2, scale2, gate2 = [mod[:, k * d:(k + 1) * d] for k in range(6)]
    row = lambda v: v.reshape(1, -1)

    w_main = w_in[:, :IN_MAIN].astype(BF16)
    n_ff = w_in.shape[1] - IN_MAIN
    w_ff = jnp.pad(w_in[:, IN_MAIN:], ((0, 0), (0, LANES - n_ff))).astype(BF16)
    proj, ff = in_proj(x, row(norm1_g), scale1, shift1, w_main, w_ff)

    b128 = jnp.pad(b_forget, (0, LANES - n_ff)).reshape(1, LANES)
    cum3 = cum_gate(ff, b128)
    pieces = cum3.reshape(s, 3, LANES)[:, :, :FOX_HEADS]
    caug = jnp.pad(pieces.transpose(2, 0, 1), ((0, 0), (0, 0), (0, LANES - 3))).astype(BF16)

    y_ret = retention(proj, positions.reshape(s, 1), row(ret_gn_g))

    k0 = 2 * RET_QK + 2 * RET_WIDTH + FOX_WIDTH
    kb = proj[:, k0:k0 + FOX_WIDTH].astype(BF16)
    vb = proj[:, k0 + FOX_WIDTH:k0 + 2 * FOX_WIDTH].astype(BF16)
    vt4 = vb.reshape(s // bq, bq, FOX_HEADS, FOX_DH).transpose(2, 0, 3, 1)
    y_fox = fox_attention(proj, kb, caug, vt4, bq=bq)

    x1 = out_proj(y_ret, y_fox, row(fox_norm_g), w_out.astype(BF16), x, gate1)

    h2, s1, s2, tau, mz = peer_query(x1, row(norm2_g), scale2, shift2,
                                     w_peer_q.astype(BF16), peer_sub_keys.astype(BF16))
    peer_t = peer_experts(h2, peer_u.astype(BF16), peer_v.T.astype(BF16), s1, s2, tau, mz)
    return x1, peer_t, gate2


def kernel(x, c, positions, w_ada, b_ada, norm1_g, w_in, b_forget, ret_gn_g, fox_norm_g, w_out,
           norm2_g, w_peer_q, peer_sub_keys, peer_u, peer_v, final_g):
    b, s, d = x.shape
    assert b == 1 and w_ada.shape[0] == 1, "one sequence, one layer"
    c8 = jnp.broadcast_to(c, (8, d))
    mod = ada_mod(c8, w_ada[0], b_ada[0].reshape(1, -1))[0:1]
    x1, peer_t, gate2 = _layer(x[0], mod, positions[0], norm1_g[0], w_in[0], b_forget[0], ret_gn_g[0],
                               fox_norm_g[0], w_out[0], norm2_g[0], w_peer_q[0], peer_sub_keys[0],
                               peer_u[0], peer_v[0])
    return final_norm(x1, peer_t, gate2, final_g.reshape(1, -1))[None]
```

```python
import functools

import numpy as np
import jax
import jax.numpy as jnp
from jax import lax
from jax.experimental import pallas as pl
from jax.experimental.pallas import tpu as pltpu

F32 = jnp.float32
BF16 = jnp.bfloat16

LANES = 128
RET_HEADS = 8
RET_DK = 128
RET_DV = 256
RET_QK = RET_HEADS * RET_DK
RET_WIDTH = RET_HEADS * RET_DV
RET_CHUNK = 128
FOX_HEADS = 16
FOX_DH = 128
FOX_WIDTH = FOX_HEADS * FOX_DH
IN_MAIN = 2 * RET_QK + 2 * RET_WIDTH + 3 * FOX_WIDTH
ROPE_BASE = 10000.0
PEER_HEADS = 8
PEER_NKEYS = 128
PEER_NEXPERTS = PEER_NKEYS * PEER_NKEYS
PEER_DQ = 256
PEER_TOPK = 16
EPS = 1e-6
NEG_BIG = -1e30
LOG2E = 1.4426950408889634
VMEM_LIMIT = 56 * 1024 * 1024

_NT = (((1,), (1,)), ((), ()))
_TN = (((0,), (0,)), ((), ()))


def _params(sem):
    return pltpu.CompilerParams(dimension_semantics=sem, vmem_limit_bytes=VMEM_LIMIT)


def _dot(a, b):
    return jnp.dot(a, b, preferred_element_type=F32)


def _dot_nt(a, b):
    return lax.dot_general(a, b, _NT, preferred_element_type=F32)


def _ada_kernel(c_ref, w_ref, b_ref, o_ref):
    c = c_ref[...]
    ca = (c / (1.0 + jnp.exp(-c))).astype(BF16)
    o_ref[...] = _dot(ca, w_ref[...].astype(BF16)) + b_ref[...]


def ada_mod(c8, w, b, *, bn=512):
    d, n = w.shape
    return pl.pallas_call(
        _ada_kernel,
        grid=(n // bn,),
        in_specs=[pl.BlockSpec((8, d), lambda j: (0, 0)),
                  pl.BlockSpec((d, bn), lambda j: (0, j)),
                  pl.BlockSpec((1, bn), lambda j: (0, j))],
        out_specs=pl.BlockSpec((8, bn), lambda j: (0, j)),
        out_shape=jax.ShapeDtypeStruct((8, n), F32),
        compiler_params=_params(("arbitrary",)),
        name="ada",
    )(c8, w, b)


def _norm_mod(x, g, scale, shift):
    ms = jnp.mean(x * x, axis=-1, keepdims=True)
    y = x * lax.rsqrt(ms + EPS) * g
    return y * (1.0 + scale) + shift


def _inpj_kernel(x_ref, g_ref, sc_ref, sh_ref, w_ref, wf_ref, o_ref, of_ref, h_scr):
    @pl.when(pl.program_id(1) == 0)
    def _():
        hb = _norm_mod(x_ref[...], g_ref[...], sc_ref[...], sh_ref[...]).astype(BF16)
        h_scr[...] = hb
        of_ref[...] = _dot(hb, wf_ref[...])

    o_ref[...] = _dot(h_scr[...], w_ref[...])


def in_proj(x, g, scale, shift, w_main, w_ff, *, tm=512, tn=1024):
    s, d = x.shape
    n = w_main.shape[1]
    vec = pl.BlockSpec((1, d), lambda i, j: (0, 0))
    return pl.pallas_call(
        _inpj_kernel,
        grid=(s // tm, n // tn),
        in_specs=[pl.BlockSpec((tm, d), lambda i, j: (i, 0)), vec, vec, vec,
                  pl.BlockSpec((d, tn), lambda i, j: (0, j)),
                  pl.BlockSpec((d, LANES), lambda i, j: (0, 0))],
        out_specs=[pl.BlockSpec((tm, tn), lambda i, j: (i, j)),
                   pl.BlockSpec((tm, LANES), lambda i, j: (i, 0))],
        out_shape=[jax.ShapeDtypeStruct((s, n), F32),
                   jax.ShapeDtypeStruct((s, LANES), F32)],
        scratch_shapes=[pltpu.VMEM((tm, d), BF16)],
        compiler_params=_params(("arbitrary", "arbitrary")),
        name="inpj",
    )(x, g, scale, shift, w_main, w_ff)


def _split3(v):
    hi = v.astype(BF16)
    r1 = v - hi.astype(F32)
    mid = r1.astype(BF16)
    lo = (r1 - mid.astype(F32)).astype(BF16)
    return hi, mid, lo


def _cum_kernel(ff_ref, b_ref, o_ref, carry):
    @pl.when(pl.program_id(0) == 0)
    def _():
        carry[...] = jnp.zeros_like(carry)

    z = ff_ref[...] + b_ref[...]
    logf = jnp.minimum(z, 0.0) - jnp.log(1.0 + jnp.exp(-jnp.abs(z)))
    tc = z.shape[0]
    row = lax.broadcasted_iota(jnp.int32, (tc, tc), 0)
    col = lax.broadcasted_iota(jnp.int32, (tc, tc), 1)
    tri = jnp.where(row >= col, 1.0, 0.0).astype(BF16)
    hi, mid, lo = _split3(logf)
    cum = _dot(tri, hi) + _dot(tri, mid) + _dot(tri, lo) + carry[...]
    carry[...] = cum[tc - 1:tc, :]
    chi, cmid, clo = _split3(cum * LOG2E)
    o_ref[:, 0:LANES] = chi.astype(F32)
    o_ref[:, LANES:2 * LANES] = cmid.astype(F32)
    o_ref[:, 2 * LANES:3 * LANES] = clo.astype(F32)


def cum_gate(ff, b128, *, tc=256):
    s = ff.shape[0]
    return pl.pallas_call(
        _cum_kernel,
        grid=(s // tc,),
        in_specs=[pl.BlockSpec((tc, LANES), lambda i: (i, 0)),
                  pl.BlockSpec((1, LANES), lambda i: (0, 0))],
        out_specs=pl.BlockSpec((tc, 3 * LANES), lambda i: (i, 0)),
        out_shape=jax.ShapeDtypeStruct((s, 3 * LANES), F32),
        scratch_shapes=[pltpu.VMEM((1, LANES), F32)],
        compiler_params=_params(("arbitrary",)),
        name="cum",
    )(ff, b128)


def _ret_consts():
    h = np.arange(RET_HEADS, dtype=np.float32)
    log_g = np.log1p(-np.exp2(-5.0 - h)).astype(np.float32)
    i = np.arange(RET_CHUNK, dtype=np.float32)
    diff = i[:, None] - i[None, :]
    dmat = np.where(diff >= 0, np.exp(np.maximum(diff, 0.0)[None] * log_g[:, None, None]), 0.0)
    kdec = np.exp((RET_CHUNK - 1.0 - i)[None, :] * log_g[:, None])
    qdec = np.exp((i + 1.0)[None, :] * log_g[:, None])
    cdec = np.exp(RET_CHUNK * log_g)
    kdec = np.broadcast_to(kdec[:, :, None], (RET_HEADS, RET_CHUNK, RET_DK))
    qdec = np.broadcast_to(qdec[:, :, None], (RET_HEADS, RET_CHUNK, RET_DK))
    half = RET_DK // 2
    inv = (ROPE_BASE ** (-np.arange(half, dtype=np.float32) / half)).astype(np.float32)
    inv_full = np.concatenate([inv, inv])[None, :]
    sign = np.concatenate([-np.ones(half, np.float32), np.ones(half, np.float32)])[None, :]
    return (dmat.astype(np.float32), np.ascontiguousarray(kdec, np.float32),
            np.ascontiguousarray(qdec, np.float32), [float(v) for v in cdec],
            inv_full.astype(np.float32), sign)


def _ret_kernel(cdec, rq_ref, rk_ref, rv_ref, rg_ref, pos_ref, inv_ref, sign_ref,
                dmat_ref, kdec_ref, qdec_ref, gn_ref, o_ref, state):
    @pl.when(pl.program_id(0) == 0)
    def _():
        state[...] = jnp.zeros_like(state)

    ang = pos_ref[...].astype(F32) * inv_ref[...]
    cosf = jnp.cos(ang)
    sinf = jnp.sin(ang) * sign_ref[...]
    kscale = RET_DK ** -0.5
    for h in range(RET_HEADS):
        q = rq_ref[:, h * RET_DK:(h + 1) * RET_DK]
        k = rk_ref[:, h * RET_DK:(h + 1) * RET_DK]
        qr = q * cosf + pltpu.roll(q, RET_DK // 2, 1) * sinf
        kr = (k * cosf + pltpu.roll(k, RET_DK // 2, 1) * sinf) * kscale
        vb = rv_ref[:, h * RET_DV:(h + 1) * RET_DV].astype(BF16)
        scores = _dot_nt(qr.astype(BF16), kr.astype(BF16)) * dmat_ref[h]
        intra = _dot(scores.astype(BF16), vb)
        st = state[h]
        cross = _dot((qr * qdec_ref[h]).astype(BF16), st.astype(BF16))
        kd = (kr * kdec_ref[h]).astype(BF16)
        kv = lax.dot_general(kd, vb, _TN, preferred_element_type=F32)
        state[h] = st * cdec[h] + kv
        y = intra + cross
        mu = jnp.mean(y, axis=-1, keepdims=True)
        dlt = y - mu
        var = jnp.mean(dlt * dlt, axis=-1, keepdims=True)
        g = rg_ref[:, h * RET_DV:(h + 1) * RET_DV]
        yn = dlt * lax.rsqrt(var + EPS) * gn_ref[:, h * RET_DV:(h + 1) * RET_DV] * (g / (1.0 + jnp.exp(-g)))
        o_ref[:, h * RET_DV:(h + 1) * RET_DV] = yn.astype(BF16)


def retention(proj, pos_col, gn_g):
    s = proj.shape[0]
    c = RET_CHUNK
    dmat, kdec, qdec, cdec, inv_full, sign = _ret_consts()
    full3 = lambda n: (0, 0, 0)
    return pl.pallas_call(
        functools.partial(_ret_kernel, cdec),
        grid=(s // c,),
        in_specs=[pl.BlockSpec((c, RET_QK), lambda n: (n, 0)),
                  pl.BlockSpec((c, RET_QK), lambda n: (n, 1)),
                  pl.BlockSpec((c, RET_WIDTH), lambda n: (n, 1)),
                  pl.BlockSpec((c, RET_WIDTH), lambda n: (n, 2)),
                  pl.BlockSpec((c, 1), lambda n: (n, 0)),
                  pl.BlockSpec((1, RET_DK), lambda n: (0, 0)),
                  pl.BlockSpec((1, RET_DK), lambda n: (0, 0)),
                  pl.BlockSpec((RET_HEADS, c, c), full3),
                  pl.BlockSpec((RET_HEADS, c, RET_DK), full3),
                  pl.BlockSpec((RET_HEADS, c, RET_DK), full3),
                  pl.BlockSpec((1, RET_WIDTH), lambda n: (0, 0))],
        out_specs=pl.BlockSpec((c, RET_WIDTH), lambda n: (n, 0)),
        out_shape=jax.ShapeDtypeStruct((s, RET_WIDTH), BF16),
        scratch_shapes=[pltpu.VMEM((RET_HEADS, RET_DK, RET_DV), F32)],
        compiler_params=_params(("arbitrary",)),
        name="ret",
    )(proj, proj, proj, proj, pos_col, jnp.asarray(inv_full), jnp.asarray(sign),
      jnp.asarray(dmat), jnp.asarray(kdec), jnp.asarray(qdec), gn_g)


def _fox_kernel(q_ref, k_ref, c_ref, vt_ref, o_ref, sa_scr, sb_scr, m_scr, l_scr, acc_scr):
    i = pl.program_id(1)
    bq = q_ref.shape[0]
    bk = sa_scr.shape[0]
    lane = lax.broadcasted_iota(jnp.int32, (bq, FOX_DH), 1)
    qa = jnp.where(lane < 3, -1.0, 0.0).astype(BF16)
    q2 = jnp.concatenate([(q_ref[...] * (FOX_DH ** -0.5 * LOG2E)).astype(BF16), qa], axis=1)
    m_scr[...] = jnp.full_like(m_scr, NEG_BIG)
    l_scr[...] = jnp.zeros_like(l_scr)
    acc_scr[...] = jnp.zeros_like(acc_scr)

    def logits(j, s_scr):
        off = pl.multiple_of(j * bk, bk)
        k2 = jnp.concatenate([k_ref[pl.ds(off, bk), :], c_ref[0, pl.ds(off, bk), :]], axis=1)
        s_scr[...] = _dot_nt(k2, q2)

    def update(j, s_scr, masked):
        st = s_scr[...]
        if masked:
            kid = j * bk + lax.broadcasted_iota(jnp.int32, (bk, bq), 0)
            qid = i * bq + lax.broadcasted_iota(jnp.int32, (bk, bq), 1)
            st = jnp.where(kid <= qid, st, NEG_BIG)
        m_old = m_scr[...]
        m_new = jnp.maximum(m_old, jnp.max(st, axis=0, keepdims=True))
        alpha = jnp.exp2(m_old - m_new)
        p = jnp.exp2(st - m_new)
        l_scr[...] = alpha * l_scr[...] + jnp.sum(p, axis=0, keepdims=True)
        acc_scr[...] = alpha * acc_scr[...] + _dot(vt_ref[0, j], p.astype(BF16))
        m_scr[...] = m_new

    logits(0, sa_scr)

    def body(jj, carry):
        logits(2 * jj + 1, sb_scr)
        update(2 * jj, sa_scr, False)
        logits(2 * jj + 2, sa_scr)
        update(2 * jj + 1, sb_scr, False)
        return carry

    lax.fori_loop(0, i, body, 0)
    logits(2 * i + 1, sb_scr)
    update(2 * i, sa_scr, True)
    update(2 * i + 1, sb_scr, True)
    o_ref[...] = (acc_scr[...] / l_scr[...]).T


def fox_attention(proj, kb, caug, vt4, *, bq=1024):
    s = proj.shape[0]
    bk = bq // 2
    qcol0 = (2 * RET_QK + 2 * RET_WIDTH) // FOX_DH
    return pl.pallas_call(
        _fox_kernel,
        grid=(FOX_HEADS, s // bq),
        in_specs=[pl.BlockSpec((bq, FOX_DH), lambda h, i: (i, qcol0 + h)),
                  pl.BlockSpec((s, FOX_DH), lambda h, i: (0, h)),
                  pl.BlockSpec((1, s, LANES), lambda h, i: (h, 0, 0)),
                  pl.BlockSpec((1, s // bk, FOX_DH, bk), lambda h, i: (h, 0, 0, 0))],
        out_specs=pl.BlockSpec((bq, FOX_DH), lambda h, i: (i, h)),
        out_shape=jax.ShapeDtypeStruct((s, FOX_WIDTH), F32),
        scratch_shapes=[pltpu.VMEM((bk, bq), F32), pltpu.VMEM((bk, bq), F32),
                        pltpu.VMEM((1, bq), F32), pltpu.VMEM((1, bq), F32),
                        pltpu.VMEM((FOX_DH, bq), F32)],
        compiler_params=_params(("arbitrary", "arbitrary")),
        name="fox",
    )(proj, kb, caug, vt4)


def _outp_kernel(yr_ref, yf_ref, fg_ref, w_ref, x_ref, g1_ref, o_ref, y_scr):
    @pl.when(pl.program_id(1) == 0)
    def _():
        yf = yf_ref[...]
        ms = jnp.mean(yf * yf, axis=-1, keepdims=True)
        yn = yf * lax.rsqrt(ms + EPS) * fg_ref[...]
        y_scr[:, 0:RET_WIDTH] = yr_ref[...]
        y_scr[:, RET_WIDTH:RET_WIDTH + FOX_WIDTH] = yn.astype(BF16)

    o_ref[...] = x_ref[...] + g1_ref[...] * _dot(y_scr[...], w_ref[...])


def out_proj(y_ret, y_fox, fox_g, w_out, x, gate1, *, tm=512, tn=1024):
    s, d = x.shape
    kdim = RET_WIDTH + FOX_WIDTH
    return pl.pallas_call(
        _outp_kernel,
        grid=(s // tm, d // tn),
        in_specs=[pl.BlockSpec((tm, RET_WIDTH), lambda i, j: (i, 0)),
                  pl.BlockSpec((tm, FOX_WIDTH), lambda i, j: (i, 0)),
                  pl.BlockSpec((1, FOX_WIDTH), lambda i, j: (0, 0)),
                  pl.BlockSpec((kdim, tn), lambda i, j: (0, j)),
                  pl.BlockSpec((tm, tn), lambda i, j: (i, j)),
                  pl.BlockSpec((1, tn), lambda i, j: (0, j))],
        out_specs=pl.BlockSpec((tm, tn), lambda i, j: (i, j)),
        out_shape=jax.ShapeDtypeStruct((s, d), F32),
        scratch_shapes=[pltpu.VMEM((tm, kdim), BF16)],
        compiler_params=_params(("arbitrary", "arbitrary")),
        name="outp",
    )(y_ret, y_fox, fox_g, w_out, x, gate1)


def _top_rows(s, t_scr):
    cur = s
    for r in range(PEER_TOPK):
        mx = jnp.max(cur, axis=0, keepdims=True)
        t_scr[r:r + 1, :] = mx
        if r + 1 < PEER_TOPK:
            cur = jnp.where(cur == mx, -jnp.inf, cur)


PEER_NCAND = 80
STAT_ROWS = 8


def _peerq_kernel(x_ref, g_ref, sc_ref, sh_ref, wq_ref, keys_ref,
                  h2t_ref, s1_ref, s2_ref, st_ref, h_scr, t1_scr, t2_scr, cand_scr):
    @pl.when(pl.program_id(1) == 0)
    def _():
        h = _norm_mod(x_ref[...], g_ref[...], sc_ref[...], sh_ref[...])
        h_scr[...] = h.astype(BF16)
        h2t_ref[...] = h.T.astype(BF16)

    half = PEER_DQ // 2
    qb = _dot(h_scr[...], wq_ref[...]).astype(BF16)
    s1 = _dot_nt(keys_ref[0, 0], qb[:, 0:half])
    s2 = _dot_nt(keys_ref[0, 1], qb[:, half:PEER_DQ])
    s1_ref[0] = s1
    s2_ref[0] = s2
    _top_rows(s1, t1_scr)
    _top_rows(s2, t2_scr)
    t1_lo = t1_scr[0:8, :]
    for b in range(8):
        cand_scr[b * 8:(b + 1) * 8, :] = t1_lo + t2_scr[b:b + 1, :]
    cand_scr[64:72, :] = t1_scr[0:1, :] + t2_scr[8:16, :]
    cand_scr[72:80, :] = t1_scr[8:16, :] + t2_scr[0:1, :]
    cand = cand_scr[...]
    cur = cand
    for r in range(PEER_TOPK):
        mx = jnp.max(cur, axis=0, keepdims=True)
        if r + 1 < PEER_TOPK:
            cur = jnp.where(cur == mx, -jnp.inf, cur)
    tau = mx
    m1 = t1_scr[0:1, :]
    m2 = t2_scr[0:1, :]
    z = jnp.sum(jnp.where(cand >= tau, jnp.exp(cand - (m1 + m2)), 0.0), axis=0, keepdims=True)
    st_ref[0, 0:1, :] = tau
    st_ref[0, 1:2, :] = m1 + jnp.log(z)
    st_ref[0, 2:3, :] = m2
    st_ref[0, 3:STAT_ROWS, :] = jnp.zeros((STAT_ROWS - 3, tau.shape[1]), F32)


def peer_query(x1, g, scale, shift, wq, keys, *, tm=512):
    s, d = x1.shape
    vec = pl.BlockSpec((1, d), lambda i, h: (0, 0))
    stat = pl.BlockSpec((1, PEER_NKEYS, tm), lambda i, h: (h, 0, i))
    return pl.pallas_call(
        _peerq_kernel,
        grid=(s // tm, PEER_HEADS),
        in_specs=[pl.BlockSpec((tm, d), lambda i, h: (i, 0)), vec, vec, vec,
                  pl.BlockSpec((d, PEER_DQ), lambda i, h: (0, h)),
                  pl.BlockSpec((1, 2, PEER_NKEYS, PEER_DQ // 2), lambda i, h: (h, 0, 0, 0))],
        out_specs=[pl.BlockSpec((d, tm), lambda i, h: (0, i)), stat, stat,
                   pl.BlockSpec((1, STAT_ROWS, tm), lambda i, h: (h, 0, i))],
        out_shape=[jax.ShapeDtypeStruct((d, s), BF16),
                   jax.ShapeDtypeStruct((PEER_HEADS, PEER_NKEYS, s), F32),
                   jax.ShapeDtypeStruct((PEER_HEADS, PEER_NKEYS, s), F32),
                   jax.ShapeDtypeStruct((PEER_HEADS, STAT_ROWS, s), F32)],
        scratch_shapes=[pltpu.VMEM((tm, d), BF16),
                        pltpu.VMEM((PEER_TOPK, tm), F32), pltpu.VMEM((PEER_TOPK, tm), F32),
                        pltpu.VMEM((PEER_NCAND, tm), F32)],
        compiler_params=_params(("arbitrary", "arbitrary")),
        name="peerq",
    )(x1, g, scale, shift, wq, keys)


def _gelu_tanh(x):
    return 0.5 * x * (1.0 + jnp.tanh(0.7978845608028654 * (x + 0.044715 * (x * x * x))))


def _peer_kernel(h2t_ref, u_ref, vt_ref, s1_ref, s2_ref, st_ref, o_ref, e1_scr, e2_scr, a_scr, w_scr):
    n = pl.program_id(1)
    tm = h2t_ref.shape[1]
    sub = 8

    @pl.when(n == 0)
    def _():
        o_ref[...] = jnp.zeros_like(o_ref)
        for h in range(PEER_HEADS):
            e1_scr[h] = jnp.exp(s1_ref[h] - st_ref[h, 1:2, :])
            e2_scr[h] = jnp.exp(s2_ref[h] - st_ref[h, 2:3, :])

    a_scr[...] = _dot(u_ref[...], h2t_ref[...])
    nrow = u_ref.shape[0] // PEER_NKEYS
    tau = [jnp.broadcast_to(st_ref[h, 0:1, :], (sub, tm)) for h in range(PEER_HEADS)]
    for c in range(nrow):
        i1 = n * nrow + c
        s1row = [jnp.broadcast_to(s1_ref[h, pl.ds(i1, 1), :], (sub, tm)) for h in range(PEER_HEADS)]
        e1row = [jnp.broadcast_to(e1_scr[h, pl.ds(i1, 1), :], (sub, tm)) for h in range(PEER_HEADS)]
        for g in range(PEER_NKEYS // (2 * sub)):
            parts = []
            for r0 in (2 * g * sub, (2 * g + 1) * sub):
                gate = None
                for h in range(PEER_HEADS):
                    sm = s2_ref[h, r0:r0 + sub, :] + s1row[h]
                    gv = jnp.where(sm >= tau[h], e2_scr[h, r0:r0 + sub, :] * e1row[h], 0.0)
                    gate = gv if gate is None else gate + gv
                rr = c * PEER_NKEYS + r0
                parts.append(_gelu_tanh(a_scr[rr:rr + sub, :]) * gate)
            rr = c * PEER_NKEYS + 2 * g * sub
            w_scr[rr:rr + 2 * sub, :] = jnp.concatenate(parts, axis=0).astype(BF16)
    o_ref[...] += _dot(vt_ref[...], w_scr[...])


def peer_experts(h2t, u, vt, s1, s2, st, *, tm=512, tn=512):
    d, s = h2t.shape
    ne = u.shape[0]
    once = pl.Buffered(1)
    stat = pl.BlockSpec((PEER_HEADS, PEER_NKEYS, tm), lambda i, n: (0, 0, i), pipeline_mode=once)
    return pl.pallas_call(
        _peer_kernel,
        grid=(s // tm, ne // tn),
        in_specs=[pl.BlockSpec((d, tm), lambda i, n: (0, i), pipeline_mode=once),
                  pl.BlockSpec((tn, d), lambda i, n: (n, 0)),
                  pl.BlockSpec((d, tn), lambda i, n: (0, n)),
                  stat, stat,
                  pl.BlockSpec((PEER_HEADS, STAT_ROWS, tm), lambda i, n: (0, 0, i), pipeline_mode=once)],
        out_specs=pl.BlockSpec((d, tm), lambda i, n: (0, i)),
        out_shape=jax.ShapeDtypeStruct((d, s), F32),
        scratch_shapes=[pltpu.VMEM((PEER_HEADS, PEER_NKEYS, tm), F32),
                        pltpu.VMEM((PEER_HEADS, PEER_NKEYS, tm), F32),
                        pltpu.VMEM((tn, tm), F32), pltpu.VMEM((tn, tm), BF16)],
        compiler_params=_params(("arbitrary", "arbitrary")),
        name="peer",
    )(h2t, u, vt, s1, s2, st)


def _fin_kernel(x_ref, pt_ref, g2_ref, fg_ref, o_ref):
    x2 = x_ref[...] + g2_ref[...] * pt_ref[...].T
    ms = jnp.mean(x2 * x2, axis=-1, keepdims=True)
    o_ref[...] = x2 * lax.rsqrt(ms + EPS) * fg_ref[...]


def final_norm(x1, peer_t, gate2, final_g, *, tm=256):
    s, d = x1.shape
    vec = pl.BlockSpec((1, d), lambda i: (0, 0))
    return pl.pallas_call(
        _fin_kernel,
        grid=(s // tm,),
        in_specs=[pl.BlockSpec((tm, d), lambda i: (i, 0)),
                  pl.BlockSpec((d, tm), lambda i: (0, i)), vec, vec],
        out_specs=pl.BlockSpec((tm, d), lambda i: (i, 0)),
        out_shape=jax.ShapeDtypeStruct((s, d), F32),
        compiler_params=_params(("arbitrary",)),
        name="fin",
    )(x1, peer_t, gate2, final_g)


def _layer(x, mod, positions, norm1_g, w_in, b_forget, ret_gn_g, fox_norm_g, w_out, norm2_g,
           w_peer_q, peer_sub_keys, peer_u, peer_v, *, bq=1024):
    s, d = x.shape
    shift1, scale1, gate1, shift2, scale2, gate2 = [mod[:, k * d:(k + 1) * d] for k in range(6)]
    row = lambda v: v.reshape(1, -1)

    w_main = w_in[:, :IN_MAIN].astype(BF16)
    n_ff = w_in.shape[1] - IN_MAIN
    w_ff = jnp.pad(w_in[:, IN_MAIN:], ((0, 0), (0, LANES - n_ff))).astype(BF16)
    proj, ff = in_proj(x, row(norm1_g), scale1, shift1, w_main, w_ff)

    b128 = jnp.pad(b_forget, (0, LANES - n_ff)).reshape(1, LANES)
    cum3 = cum_gate(ff, b128)
    pieces = cum3.reshape(s, 3, LANES)[:, :, :FOX_HEADS]
    caug = jnp.pad(pieces.transpose(2, 0, 1), ((0, 0), (0, 0), (0, LANES - 3))).astype(BF16)

    y_ret = retention(proj, positions.reshape(s, 1), row(ret_gn_g))

    k0 = 2 * RET_QK + 2 * RET_WIDTH + FOX_WIDTH
    kb = proj[:, k0:k0 + FOX_WIDTH].astype(BF16)
    vb = proj[:, k0 + FOX_WIDTH:k0 + 2 * FOX_WIDTH].astype(BF16)
    bk = bq // 2
    vt4 = vb.reshape(s // bk, bk, FOX_HEADS, FOX_DH).transpose(2, 0, 3, 1)
    y_fox = fox_attention(proj, kb, caug, vt4, bq=bq)

    x1 = out_proj(y_ret, y_fox, row(fox_norm_g), w_out.astype(BF16), x, gate1)

    h2t, s1, s2, st = peer_query(x1, row(norm2_g), scale2, shift2,
                                 w_peer_q.astype(BF16), peer_sub_keys.astype(BF16))
    peer_t = peer_experts(h2t, peer_u.astype(BF16), peer_v.T.astype(BF16), s1, s2, st)
    return x1, peer_t, gate2


def kernel(x, c, positions, w_ada, b_ada, norm1_g, w_in, b_forget, ret_gn_g, fox_norm_g, w_out,
           norm2_g, w_peer_q, peer_sub_keys, peer_u, peer_v, final_g):
    b, s, d = x.shape
    assert b == 1 and w_ada.shape[0] == 1, "one sequence, one layer"
    c8 = jnp.broadcast_to(c, (8, d))
    mod = ada_mod(c8, w_ada[0], b_ada[0].reshape(1, -1))[0:1]
    x1, peer_t, gate2 = _layer(x[0], mod, positions[0], norm1_g[0], w_in[0], b_forget[0], ret_gn_g[0],
                               fox_norm_g[0], w_out[0], norm2_g[0], w_peer_q[0], peer_sub_keys[0],
                               peer_u[0], peer_v[0])
    return final_norm(x1, peer_t, gate2, final_g.reshape(1, -1))[None]
```

```python
import functools

import numpy as np
import jax
import jax.numpy as jnp
from jax import lax
from jax.experimental import pallas as pl
from jax.experimental.pallas import tpu as pltpu

F32 = jnp.float32
BF16 = jnp.bfloat16

LANES = 128
RET_HEADS = 8
RET_DK = 128
RET_DV = 256
RET_QK = RET_HEADS * RET_DK
RET_WIDTH = RET_HEADS * RET_DV
RET_CHUNK = 128
FOX_HEADS = 16
FOX_DH = 128
FOX_WIDTH = FOX_HEADS * FOX_DH
IN_MAIN = 2 * RET_QK + 2 * RET_WIDTH + 3 * FOX_WIDTH
N_F32_COLS = 2 * RET_QK + 2 * RET_WIDTH + FOX_WIDTH
ROPE_BASE = 10000.0
PEER_HEADS = 8
PEER_NKEYS = 128
PEER_NEXPERTS = PEER_NKEYS * PEER_NKEYS
PEER_DQ = 256
PEER_TOPK = 16
PEER_NCAND = 80
STAT_ROWS = 8
EPS = 1e-6
NEG_BIG = -1e30
LOG2E = 1.4426950408889634
VMEM_LIMIT = 56 * 1024 * 1024

_NT = (((1,), (1,)), ((), ()))
_TN = (((0,), (0,)), ((), ()))


def _params(sem):
    return pltpu.CompilerParams(dimension_semantics=sem, vmem_limit_bytes=VMEM_LIMIT)


def _dot(a, b):
    return jnp.dot(a, b, preferred_element_type=F32)


def _dot_nt(a, b):
    return lax.dot_general(a, b, _NT, preferred_element_type=F32)


def _ada_kernel(c_ref, w_ref, b_ref, o_ref):
    c = c_ref[...]
    ca = (c / (1.0 + jnp.exp(-c))).astype(BF16)
    o_ref[...] = _dot(ca, w_ref[...].astype(BF16)) + b_ref[...]


def ada_mod(c8, w, b, *, bn=512):
    d, n = w.shape
    return pl.pallas_call(
        _ada_kernel,
        grid=(n // bn,),
        in_specs=[pl.BlockSpec((8, d), lambda j: (0, 0)),
                  pl.BlockSpec((d, bn), lambda j: (0, j)),
                  pl.BlockSpec((1, bn), lambda j: (0, j))],
        out_specs=pl.BlockSpec((8, bn), lambda j: (0, j)),
        out_shape=jax.ShapeDtypeStruct((8, n), F32),
        compiler_params=_params(("arbitrary",)),
        name="ada",
    )(c8, w, b)


def _cast_kernel(x_ref, o_ref):
    o_ref[...] = x_ref[...].astype(BF16)


def _cast_t_kernel(x_ref, o_ref):
    o_ref[...] = x_ref[...].T.astype(BF16)


def cast_bf16(x, *, cols=None, br=512, bc=2048, transpose=False, name="cast"):
    r, c = x.shape
    c = c if cols is None else cols
    br, bc = min(br, r), min(bc, c)
    if transpose:
        return pl.pallas_call(
            _cast_t_kernel, grid=(r // br, c // bc),
            in_specs=[pl.BlockSpec((br, bc), lambda i, j: (i, j))],
            out_specs=pl.BlockSpec((bc, br), lambda i, j: (j, i)),
            out_shape=jax.ShapeDtypeStruct((c, r), BF16),
            compiler_params=_params(("arbitrary", "arbitrary")), name=name)(x)
    return pl.pallas_call(
        _cast_kernel, grid=(r // br, c // bc),
        in_specs=[pl.BlockSpec((br, bc), lambda i, j: (i, j))],
        out_specs=pl.BlockSpec((br, bc), lambda i, j: (i, j)),
        out_shape=jax.ShapeDtypeStruct((r, c), BF16),
        compiler_params=_params(("arbitrary", "arbitrary")), name=name)(x)


def _norm_mod(x, g, scale, shift):
    ms = jnp.mean(x * x, axis=-1, keepdims=True)
    y = x * lax.rsqrt(ms + EPS) * g
    return y * (1.0 + scale) + shift


def _inpj_kernel(nf, nk, x_ref, g_ref, sc_ref, sh_ref, w_ref, wf_ref, o_ref, ok_ref, ovt_ref, of_ref, h_scr):
    j = pl.program_id(1)

    @pl.when(j == 0)
    def _():
        hb = _norm_mod(x_ref[...], g_ref[...], sc_ref[...], sh_ref[...]).astype(BF16)
        h_scr[...] = hb
        of_ref[...] = _dot(hb, wf_ref[...])

    r = _dot(h_scr[...], w_ref[...])

    @pl.when(j < nf)
    def _():
        o_ref[...] = r

    @pl.when(jnp.logical_and(j >= nf, j < nf + nk))
    def _():
        ok_ref[...] = r.astype(BF16)

    @pl.when(j >= nf + nk)
    def _():
        ovt_ref[0] = r.T.astype(BF16)


def in_proj(x, g, scale, shift, w_main, w_ff, *, tm=512, tn=1024):
    s, d = x.shape
    n = w_main.shape[1]
    nf, nk = N_F32_COLS // tn, FOX_WIDTH // tn
    vec = pl.BlockSpec((1, d), lambda i, j: (0, 0))
    return pl.pallas_call(
        functools.partial(_inpj_kernel, nf, nk),
        grid=(s // tm, n // tn),
        in_specs=[pl.BlockSpec((tm, d), lambda i, j: (i, 0)), vec, vec, vec,
                  pl.BlockSpec((d, tn), lambda i, j: (0, j)),
                  pl.BlockSpec((d, LANES), lambda i, j: (0, 0))],
        out_specs=[pl.BlockSpec((tm, tn), lambda i, j: (i, jnp.minimum(j, nf - 1))),
                   pl.BlockSpec((tm, tn), lambda i, j: (i, jnp.clip(j - nf, 0, nk - 1))),
                   pl.BlockSpec((1, tn, tm), lambda i, j: (i, jnp.clip(j - nf - nk, 0, nk - 1), 0)),
                   pl.BlockSpec((tm, LANES), lambda i, j: (i, 0))],
        out_shape=[jax.ShapeDtypeStruct((s, N_F32_COLS), F32),
                   jax.ShapeDtypeStruct((s, FOX_WIDTH), BF16),
                   jax.ShapeDtypeStruct((s // tm, FOX_WIDTH, tm), BF16),
                   jax.ShapeDtypeStruct((s, LANES), F32)],
        scratch_shapes=[pltpu.VMEM((tm, d), BF16)],
        compiler_params=_params(("arbitrary", "arbitrary")),
        name="inpj",
    )(x, g, scale, shift, w_main, w_ff)


def _split3(v):
    hi = v.astype(BF16)
    r1 = v - hi.astype(F32)
    mid = r1.astype(BF16)
    lo = (r1 - mid.astype(F32)).astype(BF16)
    return hi, mid, lo


def _cum_place():
    m = np.zeros((3 * LANES, FOX_HEADS * LANES), np.float32)
    for p in range(3):
        for h in range(FOX_HEADS):
            m[p * LANES + h, h * LANES + p] = 1.0
    return m


def _cum_kernel(ff_ref, b_ref, place_ref, o_ref, carry):
    @pl.when(pl.program_id(0) == 0)
    def _():
        carry[...] = jnp.zeros_like(carry)

    z = ff_ref[...] + b_ref[...]
    logf = jnp.minimum(z, 0.0) - jnp.log(1.0 + jnp.exp(-jnp.abs(z)))
    tc = z.shape[0]
    row = lax.broadcasted_iota(jnp.int32, (tc, tc), 0)
    col = lax.broadcasted_iota(jnp.int32, (tc, tc), 1)
    tri = jnp.where(row >= col, 1.0, 0.0).astype(BF16)
    hi, mid, lo = _split3(logf)
    cum = _dot(tri, hi) + _dot(tri, mid) + _dot(tri, lo) + carry[...]
    carry[...] = cum[tc - 1:tc, :]
    pieces = jnp.concatenate(_split3(cum * LOG2E), axis=1)
    o_ref[...] = _dot(pieces, place_ref[...]).astype(BF16)


def cum_gate(ff, b128, *, tc=256):
    s = ff.shape[0]
    place = jnp.asarray(_cum_place(), BF16)
    return pl.pallas_call(
        _cum_kernel,
        grid=(s // tc,),
        in_specs=[pl.BlockSpec((tc, LANES), lambda i: (i, 0)),
                  pl.BlockSpec((1, LANES), lambda i: (0, 0)),
                  pl.BlockSpec(place.shape, lambda i: (0, 0))],
        out_specs=pl.BlockSpec((tc, FOX_HEADS * LANES), lambda i: (i, 0)),
        out_shape=jax.ShapeDtypeStruct((s, FOX_HEADS * LANES), BF16),
        scratch_shapes=[pltpu.VMEM((1, LANES), F32)],
        compiler_params=_params(("arbitrary",)),
        name="cum",
    )(ff, b128, place)


def _ret_consts():
    h = np.arange(RET_HEADS, dtype=np.float32)
    log_g = np.log1p(-np.exp2(-5.0 - h)).astype(np.float32)
    i = np.arange(RET_CHUNK, dtype=np.float32)
    diff = i[:, None] - i[None, :]
    dmat = np.where(diff >= 0, np.exp(np.maximum(diff, 0.0)[None] * log_g[:, None, None]), 0.0)
    kdec = np.exp((RET_CHUNK - 1.0 - i)[None, :] * log_g[:, None])
    qdec = np.exp((i + 1.0)[None, :] * log_g[:, None])
    cdec = np.exp(RET_CHUNK * log_g)
    kdec = np.broadcast_to(kdec[:, :, None], (RET_HEADS, RET_CHUNK, RET_DK))
    qdec = np.broadcast_to(qdec[:, :, None], (RET_HEADS, RET_CHUNK, RET_DK))
    half = RET_DK // 2
    inv = (ROPE_BASE ** (-np.arange(half, dtype=np.float32) / half)).astype(np.float32)
    inv_full = np.concatenate([inv, inv])[None, :]
    sign = np.concatenate([-np.ones(half, np.float32), np.ones(half, np.float32)])[None, :]
    return (dmat.astype(np.float32), np.ascontiguousarray(kdec, np.float32),
            np.ascontiguousarray(qdec, np.float32), [float(v) for v in cdec],
            inv_full.astype(np.float32), sign)


def _ret_kernel(cdec, rq_ref, rk_ref, rv_ref, rg_ref, pos_ref, inv_ref, sign_ref,
                dmat_ref, kdec_ref, qdec_ref, gn_ref, o_ref, state):
    @pl.when(pl.program_id(0) == 0)
    def _():
        state[...] = jnp.zeros_like(state)

    ang = pos_ref[...].astype(F32) * inv_ref[...]
    cosf = jnp.cos(ang)
    sinf = jnp.sin(ang) * sign_ref[...]
    kscale = RET_DK ** -0.5
    for h in range(RET_HEADS):
        q = rq_ref[:, h * RET_DK:(h + 1) * RET_DK]
        k = rk_ref[:, h * RET_DK:(h + 1) * RET_DK]
        qr = q * cosf + pltpu.roll(q, RET_DK // 2, 1) * sinf
        kr = (k * cosf + pltpu.roll(k, RET_DK // 2, 1) * sinf) * kscale
        vb = rv_ref[:, h * RET_DV:(h + 1) * RET_DV].astype(BF16)
        scores = _dot_nt(qr.astype(BF16), kr.astype(BF16)) * dmat_ref[h]
        intra = _dot(scores.astype(BF16), vb)
        st = state[h]
        cross = _dot((qr * qdec_ref[h]).astype(BF16), st.astype(BF16))
        kd = (kr * kdec_ref[h]).astype(BF16)
        kv = lax.dot_general(kd, vb, _TN, preferred_element_type=F32)
        state[h] = st * cdec[h] + kv
        y = intra + cross
        mu = jnp.mean(y, axis=-1, keepdims=True)
        dlt = y - mu
        var = jnp.mean(dlt * dlt, axis=-1, keepdims=True)
        g = rg_ref[:, h * RET_DV:(h + 1) * RET_DV]
        yn = dlt * lax.rsqrt(var + EPS) * gn_ref[:, h * RET_DV:(h + 1) * RET_DV] * (g / (1.0 + jnp.exp(-g)))
        o_ref[:, h * RET_DV:(h + 1) * RET_DV] = yn.astype(BF16)


def retention(proj, pos_col, gn_g):
    s = proj.shape[0]
    c = RET_CHUNK
    dmat, kdec, qdec, cdec, inv_full, sign = _ret_consts()
    full3 = lambda n: (0, 0, 0)
    return pl.pallas_call(
        functools.partial(_ret_kernel, cdec),
        grid=(s // c,),
        in_specs=[pl.BlockSpec((c, RET_QK), lambda n: (n, 0)),
                  pl.BlockSpec((c, RET_QK), lambda n: (n, 1)),
                  pl.BlockSpec((c, RET_WIDTH), lambda n: (n, 1)),
                  pl.BlockSpec((c, RET_WIDTH), lambda n: (n, 2)),
                  pl.BlockSpec((c, 1), lambda n: (n, 0)),
                  pl.BlockSpec((1, RET_DK), lambda n: (0, 0)),
                  pl.BlockSpec((1, RET_DK), lambda n: (0, 0)),
                  pl.BlockSpec((RET_HEADS, c, c), full3),
                  pl.BlockSpec((RET_HEADS, c, RET_DK), full3),
                  pl.BlockSpec((RET_HEADS, c, RET_DK), full3),
                  pl.BlockSpec((1, RET_WIDTH), lambda n: (0, 0))],
        out_specs=pl.BlockSpec((c, RET_WIDTH), lambda n: (n, 0)),
        out_shape=jax.ShapeDtypeStruct((s, RET_WIDTH), BF16),
        scratch_shapes=[pltpu.VMEM((RET_HEADS, RET_DK, RET_DV), F32)],
        compiler_params=_params(("arbitrary",)),
        name="ret",
    )(proj, proj, proj, proj, pos_col, jnp.asarray(inv_full), jnp.asarray(sign),
      jnp.asarray(dmat), jnp.asarray(kdec), jnp.asarray(qdec), gn_g)


def _fox_kernel(q_ref, k_ref, c_ref, vt_ref, o_ref, sa_scr, sb_scr, m_scr, l_scr, acc_scr):
    i = pl.program_id(1)
    bq = q_ref.shape[0]
    bk = sa_scr.shape[0]
    sub = 8
    lane = lax.broadcasted_iota(jnp.int32, (bq, FOX_DH), 1)
    qa = jnp.where(lane < 3, -1.0, 0.0).astype(BF16)
    q2 = jnp.concatenate([(q_ref[...] * (FOX_DH ** -0.5 * LOG2E)).astype(BF16), qa], axis=1)
    m_scr[...] = jnp.full_like(m_scr, NEG_BIG)
    l_scr[...] = jnp.zeros_like(l_scr)
    acc_scr[...] = jnp.zeros_like(acc_scr)

    def logits(j, s_scr):
        off = pl.multiple_of(j * bk, bk)
        k2 = jnp.concatenate([k_ref[pl.ds(off, bk), :], c_ref[pl.ds(off, bk), :]], axis=1)
        s_scr[...] = _dot_nt(k2, q2)

    def update(j, s_scr, masked):
        st = s_scr[...].reshape(bk // sub, sub, bq)
        if masked:
            kid = (j * bk + lax.broadcasted_iota(jnp.int32, (bk, bq), 0)).reshape(bk // sub, sub, bq)
            qid = (i * bq + lax.broadcasted_iota(jnp.int32, (bk, bq), 1)).reshape(bk // sub, sub, bq)
            st = jnp.where(kid <= qid, st, NEG_BIG)
        m_old = m_scr[...]
        m_new = jnp.maximum(m_old, jnp.max(jnp.max(st, axis=0), axis=0, keepdims=True))
        alpha = jnp.exp2(m_old - m_new)
        p = jnp.exp2(st - jnp.broadcast_to(m_new, (sub, bq))[None])
        l_scr[...] = alpha * l_scr[...] + jnp.sum(jnp.sum(p, axis=0), axis=0, keepdims=True)
        pv = _dot(vt_ref[j], p.reshape(bk, bq).astype(BF16))
        a8 = jnp.broadcast_to(alpha, (sub, bq))[None]
        acc_scr[...] = (a8 * acc_scr[...].reshape(FOX_DH // sub, sub, bq)).reshape(FOX_DH, bq) + pv
        m_scr[...] = m_new

    logits(0, sa_scr)

    def body(jj, carry):
        logits(2 * jj + 1, sb_scr)
        update(2 * jj, sa_scr, False)
        logits(2 * jj + 2, sa_scr)
        update(2 * jj + 1, sb_scr, False)
        return carry

    lax.fori_loop(0, i, body, 0)
    logits(2 * i + 1, sb_scr)
    update(2 * i, sa_scr, True)
    update(2 * i + 1, sb_scr, True)
    o_ref[...] = (acc_scr[...] / l_scr[...]).T


def fox_attention(pf32, kb, caug, vt3, *, bq=1024):
    s = pf32.shape[0]
    bk = bq // 2
    assert vt3.shape == (s // bk, FOX_WIDTH, bk)
    qcol0 = (N_F32_COLS - FOX_WIDTH) // FOX_DH
    return pl.pallas_call(
        _fox_kernel,
        grid=(FOX_HEADS, s // bq),
        in_specs=[pl.BlockSpec((bq, FOX_DH), lambda h, i: (i, qcol0 + h)),
                  pl.BlockSpec((s, FOX_DH), lambda h, i: (0, h)),
                  pl.BlockSpec((s, LANES), lambda h, i: (0, h)),
                  pl.BlockSpec((s // bk, FOX_DH, bk), lambda h, i: (0, h, 0))],
        out_specs=pl.BlockSpec((bq, FOX_DH), lambda h, i: (i, h)),
        out_shape=jax.ShapeDtypeStruct((s, FOX_WIDTH), F32),
        scratch_shapes=[pltpu.VMEM((bk, bq), F32), pltpu.VMEM((bk, bq), F32),
                        pltpu.VMEM((1, bq), F32), pltpu.VMEM((1, bq), F32),
                        pltpu.VMEM((FOX_DH, bq), F32)],
        compiler_params=_params(("arbitrary", "arbitrary")),
        name="fox",
    )(pf32, kb, caug, vt3)


def _outp_kernel(yr_ref, yf_ref, fg_ref, w_ref, x_ref, g1_ref, o_ref, y_scr):
    @pl.when(pl.program_id(1) == 0)
    def _():
        yf = yf_ref[...]
        ms = jnp.mean(yf * yf, axis=-1, keepdims=True)
        yn = yf * lax.rsqrt(ms + EPS) * fg_ref[...]
        y_scr[:, 0:RET_WIDTH] = yr_ref[...]
        y_scr[:, RET_WIDTH:RET_WIDTH + FOX_WIDTH] = yn.astype(BF16)

    o_ref[...] = x_ref[...] + g1_ref[...] * _dot(y_scr[...], w_ref[...])


def out_proj(y_ret, y_fox, fox_g, w_out, x, gate1, *, tm=512, tn=1024):
    s, d = x.shape
    kdim = RET_WIDTH + FOX_WIDTH
    return pl.pallas_call(
        _outp_kernel,
        grid=(s // tm, d // tn),
        in_specs=[pl.BlockSpec((tm, RET_WIDTH), lambda i, j: (i, 0)),
                  pl.BlockSpec((tm, FOX_WIDTH), lambda i, j: (i, 0)),
                  pl.BlockSpec((1, FOX_WIDTH), lambda i, j: (0, 0)),
                  pl.BlockSpec((kdim, tn), lambda i, j: (0, j)),
                  pl.BlockSpec((tm, tn), lambda i, j: (i, j)),
                  pl.BlockSpec((1, tn), lambda i, j: (0, j))],
        out_specs=pl.BlockSpec((tm, tn), lambda i, j: (i, j)),
        out_shape=jax.ShapeDtypeStruct((s, d), F32),
        scratch_shapes=[pltpu.VMEM((tm, kdim), BF16)],
        compiler_params=_params(("arbitrary", "arbitrary")),
        name="outp",
    )(y_ret, y_fox, fox_g, w_out, x, gate1)


def _drop_max(cur):
    mx = jnp.max(jnp.max(cur, axis=0), axis=0, keepdims=True)
    return mx, jnp.where(cur == jnp.broadcast_to(mx, cur.shape[1:])[None], -jnp.inf, cur)


def _top_rows(s, t_scr):
    cur = s.reshape(s.shape[0] // 8, 8, s.shape[1])
    for r in range(PEER_TOPK):
        mx, cur = _drop_max(cur)
        t_scr[r:r + 1, :] = mx


def _peerq_kernel(x_ref, g_ref, sc_ref, sh_ref, wq_ref, keys_ref,
                  h2t_ref, s1_ref, s2_ref, st_ref, h_scr, t1_scr, t2_scr, cand_scr):
    @pl.when(pl.program_id(1) == 0)
    def _():
        h = _norm_mod(x_ref[...], g_ref[...], sc_ref[...], sh_ref[...])
        h_scr[...] = h.astype(BF16)
        h2t_ref[...] = h.T.astype(BF16)

    half = PEER_DQ // 2
    qb = _dot(h_scr[...], wq_ref[...]).astype(BF16)
    s1 = _dot_nt(keys_ref[0, 0], qb[:, 0:half])
    s2 = _dot_nt(keys_ref[0, 1], qb[:, half:PEER_DQ])
    s1_ref[0] = s1
    s2_ref[0] = s2
    _top_rows(s1, t1_scr)
    _top_rows(s2, t2_scr)
    t1_lo = t1_scr[0:8, :]
    for b in range(8):
        cand_scr[b * 8:(b + 1) * 8, :] = t1_lo + t2_scr[b:b + 1, :]
    cand_scr[64:72, :] = t1_scr[0:1, :] + t2_scr[8:16, :]
    cand_scr[72:80, :] = t1_scr[8:16, :] + t2_scr[0:1, :]
    tm = s1.shape[1]
    cand = cand_scr[...].reshape(PEER_NCAND // 8, 8, tm)
    cur = cand
    for r in range(PEER_TOPK):
        tau, cur = _drop_max(cur)
    m1 = t1_scr[0:1, :]
    m2 = t2_scr[0:1, :]
    top8 = jnp.broadcast_to(m1 + m2, (8, tm))[None]
    tau8 = jnp.broadcast_to(tau, (8, tm))[None]
    z = jnp.sum(jnp.sum(jnp.where(cand >= tau8, jnp.exp(cand - top8), 0.0), axis=0), axis=0, keepdims=True)
    st_ref[0, 0:1, :] = tau
    st_ref[0, 1:2, :] = m1 + jnp.log(z)
    st_ref[0, 2:3, :] = m2
    st_ref[0, 3:STAT_ROWS, :] = jnp.zeros((STAT_ROWS - 3, tm), F32)


def peer_query(x1, g, scale, shift, wq, keys, *, tm=512):
    s, d = x1.shape
    vec = pl.BlockSpec((1, d), lambda i, h: (0, 0))
    stat = pl.BlockSpec((1, PEER_NKEYS, tm), lambda i, h: (h, 0, i))
    return pl.pallas_call(
        _peerq_kernel,
        grid=(s // tm, PEER_HEADS),
        in_specs=[pl.BlockSpec((tm, d), lambda i, h: (i, 0)), vec, vec, vec,
                  pl.BlockSpec((d, PEER_DQ), lambda i, h: (0, h)),
                  pl.BlockSpec((1, 2, PEER_NKEYS, PEER_DQ // 2), lambda i, h: (h, 0, 0, 0))],
        out_specs=[pl.BlockSpec((d, tm), lambda i, h: (0, i)), stat, stat,
                   pl.BlockSpec((1, STAT_ROWS, tm), lambda i, h: (h, 0, i))],
        out_shape=[jax.ShapeDtypeStruct((d, s), BF16),
                   jax.ShapeDtypeStruct((PEER_HEADS, PEER_NKEYS, s), F32),
                   jax.ShapeDtypeStruct((PEER_HEADS, PEER_NKEYS, s), F32),
                   jax.ShapeDtypeStruct((PEER_HEADS, STAT_ROWS, s), F32)],
        scratch_shapes=[pltpu.VMEM((tm, d), BF16),
                        pltpu.VMEM((PEER_TOPK, tm), F32), pltpu.VMEM((PEER_TOPK, tm), F32),
                        pltpu.VMEM((PEER_NCAND, tm), F32)],
        compiler_params=_params(("arbitrary", "arbitrary")),
        name="peerq",
    )(x1, g, scale, shift, wq, keys)


def _gelu_tanh(x):
    return 0.5 * x * (1.0 + jnp.tanh(0.7978845608028654 * (x + 0.044715 * (x * x * x))))


def _peer_kernel(h2t_ref, u_ref, vt_ref, s1_ref, s2_ref, st_ref, o_ref, e1_scr, e2_scr, g_scr, a_scr, w_scr):
    n = pl.program_id(1)
    tm = h2t_ref.shape[1]
    tn = u_ref.shape[0]
    sub = 8
    nrow = tn // PEER_NKEYS

    @pl.when(n == 0)
    def _():
        o_ref[...] = jnp.zeros_like(o_ref)
        for h in range(PEER_HEADS):
            e1_scr[h] = jnp.exp(s1_ref[h] - st_ref[h, 1:2, :])
            e2_scr[h] = jnp.exp(s2_ref[h] - st_ref[h, 2:3, :])

    tau = [jnp.broadcast_to(st_ref[h, 0:1, :], (sub, tm)) for h in range(PEER_HEADS)]
    for c in range(nrow):
        i1 = n * nrow + c
        s1row = [jnp.broadcast_to(s1_ref[h, pl.ds(i1, 1), :], (sub, tm)) for h in range(PEER_HEADS)]
        e1row = [jnp.broadcast_to(e1_scr[h, pl.ds(i1, 1), :], (sub, tm)) for h in range(PEER_HEADS)]
        for r0 in range(0, PEER_NKEYS, sub):
            gate = None
            for h in range(PEER_HEADS):
                sm = s2_ref[h, r0:r0 + sub, :] + s1row[h]
                gv = jnp.where(sm >= tau[h], e2_scr[h, r0:r0 + sub, :] * e1row[h], 0.0)
                gate = gv if gate is None else gate + gv
            g_scr[c * PEER_NKEYS + r0:c * PEER_NKEYS + r0 + sub, :] = gate

    a_scr[...] = _dot(u_ref[...], h2t_ref[...])
    rows = 2 * sub
    for r0 in range(0, tn, rows):
        w_scr[r0:r0 + rows, :] = (_gelu_tanh(a_scr[r0:r0 + rows, :]) * g_scr[r0:r0 + rows, :]).astype(BF16)
    o_ref[...] += _dot(vt_ref[...], w_scr[...])


def peer_experts(h2t, u, vt, s1, s2, st, *, tm=512, tn=512):
    d, s = h2t.shape
    ne = u.shape[0]
    once = pl.Buffered(1)
    stat = pl.BlockSpec((PEER_HEADS, PEER_NKEYS, tm), lambda i, n: (0, 0, i), pipeline_mode=once)
    return pl.pallas_call(
        _peer_kernel,
        grid=(s // tm, ne // tn),
        in_specs=[pl.BlockSpec((d, tm), lambda i, n: (0, i), pipeline_mode=once),
                  pl.BlockSpec((tn, d), lambda i, n: (n, 0)),
                  pl.BlockSpec((d, tn), lambda i, n: (0, n)),
                  stat, stat,
                  pl.BlockSpec((PEER_HEADS, STAT_ROWS, tm), lambda i, n: (0, 0, i), pipeline_mode=once)],
        out_specs=pl.BlockSpec((d, tm), lambda i, n: (0, i)),
        out_shape=jax.ShapeDtypeStruct((d, s), F32),
        scratch_shapes=[pltpu.VMEM((PEER_HEADS, PEER_NKEYS, tm), F32),
                        pltpu.VMEM((PEER_HEADS, PEER_NKEYS, tm), F32),
                        pltpu.VMEM((tn, tm), F32), pltpu.VMEM((tn, tm), F32), pltpu.VMEM((tn, tm), BF16)],
        compiler_params=_params(("arbitrary", "arbitrary")),
        name="peer",
    )(h2t, u, vt, s1, s2, st)


def _fin_kernel(x_ref, pt_ref, g2_ref, fg_ref, o_ref):
    x2 = x_ref[...] + g2_ref[...] * pt_ref[...].T
    ms = jnp.mean(x2 * x2, axis=-1, keepdims=True)
    o_ref[...] = x2 * lax.rsqrt(ms + EPS) * fg_ref[...]


def final_norm(x1, peer_t, gate2, final_g, *, tm=256):
    s, d = x1.shape
    vec = pl.BlockSpec((1, d), lambda i: (0, 0))
    return pl.pallas_call(
        _fin_kernel,
        grid=(s // tm,),
        in_specs=[pl.BlockSpec((tm, d), lambda i: (i, 0)),
                  pl.BlockSpec((d, tm), lambda i: (0, i)), vec, vec],
        out_specs=pl.BlockSpec((tm, d), lambda i: (i, 0)),
        out_shape=jax.ShapeDtypeStruct((s, d), F32),
        compiler_params=_params(("arbitrary",)),
        name="fin",
    )(x1, peer_t, gate2, final_g)


def _layer(x, mod, positions, norm1_g, w_in, b_forget, ret_gn_g, fox_norm_g, w_out, norm2_g,
           w_peer_q, peer_sub_keys, peer_u, peer_v, *, bq=1024):
    s, d = x.shape
    shift1, scale1, gate1, shift2, scale2, gate2 = [mod[:, k * d:(k + 1) * d] for k in range(6)]
    row = lambda v: v.reshape(1, -1)

    w_main = cast_bf16(w_in, cols=IN_MAIN, name="cast_in")
    n_ff = w_in.shape[1] - IN_MAIN
    w_ff = jnp.pad(w_in[:, IN_MAIN:], ((0, 0), (0, LANES - n_ff))).astype(BF16)
    pf32, kb, vt3, ff = in_proj(x, row(norm1_g), scale1, shift1, w_main, w_ff, tm=bq // 2)

    b128 = jnp.pad(b_forget, (0, LANES - n_ff)).reshape(1, LANES)
    caug = cum_gate(ff, b128)
    y_ret = retention(pf32, positions.reshape(s, 1), row(ret_gn_g))
    y_fox = fox_attention(pf32, kb, caug, vt3, bq=bq)
    x1 = out_proj(y_ret, y_fox, row(fox_norm_g), cast_bf16(w_out, name="cast_out"), x, gate1)

    h2t, s1, s2, st = peer_query(x1, row(norm2_g), scale2, shift2,
                                 cast_bf16(w_peer_q, name="cast_q"), peer_sub_keys.astype(BF16))
    peer_t = peer_experts(h2t, cast_bf16(peer_u, name="cast_u"),
                          cast_bf16(peer_v, transpose=True, br=512, bc=1024, name="cast_v"), s1, s2, st)
    return x1, peer_t, gate2


def kernel(x, c, positions, w_ada, b_ada, norm1_g, w_in, b_forget, ret_gn_g, fox_norm_g, w_out,
           norm2_g, w_peer_q, peer_sub_keys, peer_u, peer_v, final_g):
    b, s, d = x.shape
    assert b == 1 and w_ada.shape[0] == 1, "one sequence, one layer"
    c8 = jnp.broadcast_to(c, (8, d))
    mod = ada_mod(c8, w_ada[0], b_ada[0].reshape(1, -1))[0:1]
    x1, peer_t, gate2 = _layer(x[0], mod, positions[0], norm1_g[0], w_in[0], b_forget[0], ret_gn_g[0],
                               fox_norm_g[0], w_out[0], norm2_g[0], w_peer_q[0], peer_sub_keys[0],
                               peer_u[0], peer_v[0])
    return final_norm(x1, peer_t, gate2, final_g.reshape(1, -1))[None]
```

```python
import functools

import numpy as np
import jax
import jax.numpy as jnp
from jax import lax
from jax.experimental import pallas as pl
from jax.experimental.pallas import tpu as pltpu

F32 = jnp.float32
BF16 = jnp.bfloat16

LANES = 128
RET_HEADS = 8
RET_DK = 128
RET_DV = 256
RET_QK = RET_HEADS * RET_DK
RET_WIDTH = RET_HEADS * RET_DV
RET_CHUNK = 128
FOX_HEADS = 16
FOX_DH = 128
FOX_WIDTH = FOX_HEADS * FOX_DH
IN_MAIN = 2 * RET_QK + 2 * RET_WIDTH + 3 * FOX_WIDTH
N_F32_COLS = 2 * RET_QK + 2 * RET_WIDTH + FOX_WIDTH
ROPE_BASE = 10000.0
PEER_HEADS = 8
PEER_NKEYS = 128
PEER_NEXPERTS = PEER_NKEYS * PEER_NKEYS
PEER_DQ = 256
PEER_TOPK = 16
PEER_NCAND = 80
STAT_ROWS = 8
EPS = 1e-6
NEG_BIG = -1e30
LOG2E = 1.4426950408889634
VMEM_LIMIT = 56 * 1024 * 1024

_NT = (((1,), (1,)), ((), ()))
_TN = (((0,), (0,)), ((), ()))


def _params(sem):
    return pltpu.CompilerParams(dimension_semantics=sem, vmem_limit_bytes=VMEM_LIMIT)


def _dot(a, b):
    return jnp.dot(a, b, preferred_element_type=F32)


def _dot_nt(a, b):
    return lax.dot_general(a, b, _NT, preferred_element_type=F32)


def _ada_kernel(c_ref, w_ref, b_ref, o_ref):
    c = c_ref[...]
    ca = (c / (1.0 + jnp.exp(-c))).astype(BF16)
    o_ref[...] = _dot(ca, w_ref[...].astype(BF16)) + b_ref[...]


def ada_mod(c8, w, b, *, bn=512):
    d, n = w.shape
    return pl.pallas_call(
        _ada_kernel,
        grid=(n // bn,),
        in_specs=[pl.BlockSpec((8, d), lambda j: (0, 0)),
                  pl.BlockSpec((d, bn), lambda j: (0, j)),
                  pl.BlockSpec((1, bn), lambda j: (0, j))],
        out_specs=pl.BlockSpec((8, bn), lambda j: (0, j)),
        out_shape=jax.ShapeDtypeStruct((8, n), F32),
        compiler_params=_params(("arbitrary",)),
        name="ada",
    )(c8, w, b)


def _cast_kernel(x_ref, o_ref):
    o_ref[...] = x_ref[...].astype(BF16)


def _cast_t_kernel(x_ref, o_ref):
    o_ref[...] = x_ref[...].T.astype(BF16)


def cast_bf16(x, *, cols=None, br=512, bc=2048, transpose=False, name="cast"):
    r, c = x.shape
    c = c if cols is None else cols
    br, bc = min(br, r), min(bc, c)
    if transpose:
        return pl.pallas_call(
            _cast_t_kernel, grid=(r // br, c // bc),
            in_specs=[pl.BlockSpec((br, bc), lambda i, j: (i, j))],
            out_specs=pl.BlockSpec((bc, br), lambda i, j: (j, i)),
            out_shape=jax.ShapeDtypeStruct((c, r), BF16),
            compiler_params=_params(("arbitrary", "arbitrary")), name=name)(x)
    return pl.pallas_call(
        _cast_kernel, grid=(r // br, c // bc),
        in_specs=[pl.BlockSpec((br, bc), lambda i, j: (i, j))],
        out_specs=pl.BlockSpec((br, bc), lambda i, j: (i, j)),
        out_shape=jax.ShapeDtypeStruct((r, c), BF16),
        compiler_params=_params(("arbitrary", "arbitrary")), name=name)(x)


def _norm_mod(x, g, scale, shift):
    ms = jnp.mean(x * x, axis=-1, keepdims=True)
    y = x * lax.rsqrt(ms + EPS) * g
    return y * (1.0 + scale) + shift


def _inpj_kernel(nf, nk, x_ref, g_ref, sc_ref, sh_ref, w_ref, wf_ref, o_ref, ok_ref, ovt_ref, of_ref, h_scr):
    j = pl.program_id(1)

    @pl.when(j == 0)
    def _():
        hb = _norm_mod(x_ref[...], g_ref[...], sc_ref[...], sh_ref[...]).astype(BF16)
        h_scr[...] = hb
        of_ref[...] = _dot(hb, wf_ref[...])

    r = _dot(h_scr[...], w_ref[...])

    @pl.when(j < nf)
    def _():
        o_ref[...] = r

    @pl.when(jnp.logical_and(j >= nf, j < nf + nk))
    def _():
        ok_ref[...] = r.astype(BF16)

    @pl.when(j >= nf + nk)
    def _():
        ovt_ref[0] = r.T.astype(BF16)


def in_proj(x, g, scale, shift, w_main, w_ff, *, tm=512, tn=1024):
    s, d = x.shape
    n = w_main.shape[1]
    nf, nk = N_F32_COLS // tn, FOX_WIDTH // tn
    vec = pl.BlockSpec((1, d), lambda i, j: (0, 0))
    return pl.pallas_call(
        functools.partial(_inpj_kernel, nf, nk),
        grid=(s // tm, n // tn),
        in_specs=[pl.BlockSpec((tm, d), lambda i, j: (i, 0)), vec, vec, vec,
                  pl.BlockSpec((d, tn), lambda i, j: (0, j)),
                  pl.BlockSpec((d, LANES), lambda i, j: (0, 0))],
        out_specs=[pl.BlockSpec((tm, tn), lambda i, j: (i, jnp.minimum(j, nf - 1))),
                   pl.BlockSpec((tm, tn), lambda i, j: (i, jnp.clip(j - nf, 0, nk - 1))),
                   pl.BlockSpec((1, tn, tm), lambda i, j: (i, jnp.clip(j - nf - nk, 0, nk - 1), 0)),
                   pl.BlockSpec((tm, LANES), lambda i, j: (i, 0))],
        out_shape=[jax.ShapeDtypeStruct((s, N_F32_COLS), F32),
                   jax.ShapeDtypeStruct((s, FOX_WIDTH), BF16),
                   jax.ShapeDtypeStruct((s // tm, FOX_WIDTH, tm), BF16),
                   jax.ShapeDtypeStruct((s, LANES), F32)],
        scratch_shapes=[pltpu.VMEM((tm, d), BF16)],
        compiler_params=_params(("arbitrary", "arbitrary")),
        name="inpj",
    )(x, g, scale, shift, w_main, w_ff)


def _split3(v):
    hi = v.astype(BF16)
    r1 = v - hi.astype(F32)
    mid = r1.astype(BF16)
    lo = (r1 - mid.astype(F32)).astype(BF16)
    return hi, mid, lo


def _cum_place():
    m = np.zeros((3 * LANES, FOX_HEADS * LANES), np.float32)
    for p in range(3):
        for h in range(FOX_HEADS):
            m[p * LANES + h, h * LANES + p] = 1.0
    return m


def _cum_kernel(ff_ref, b_ref, place_ref, o_ref, carry):
    @pl.when(pl.program_id(0) == 0)
    def _():
        carry[...] = jnp.zeros_like(carry)

    z = ff_ref[...] + b_ref[...]
    logf = jnp.minimum(z, 0.0) - jnp.log(1.0 + jnp.exp(-jnp.abs(z)))
    tc = z.shape[0]
    row = lax.broadcasted_iota(jnp.int32, (tc, tc), 0)
    col = lax.broadcasted_iota(jnp.int32, (tc, tc), 1)
    tri = jnp.where(row >= col, 1.0, 0.0).astype(BF16)
    hi, mid, lo = _split3(logf)
    cum = _dot(tri, hi) + _dot(tri, mid) + _dot(tri, lo) + carry[...]
    carry[...] = cum[tc - 1:tc, :]
    pieces = jnp.concatenate(_split3(cum * LOG2E), axis=1)
    o_ref[...] = _dot(pieces, place_ref[...]).astype(BF16)


def cum_gate(ff, b128, *, tc=256):
    s = ff.shape[0]
    place = jnp.asarray(_cum_place(), BF16)
    return pl.pallas_call(
        _cum_kernel,
        grid=(s // tc,),
        in_specs=[pl.BlockSpec((tc, LANES), lambda i: (i, 0)),
                  pl.BlockSpec((1, LANES), lambda i: (0, 0)),
                  pl.BlockSpec(place.shape, lambda i: (0, 0))],
        out_specs=pl.BlockSpec((tc, FOX_HEADS * LANES), lambda i: (i, 0)),
        out_shape=jax.ShapeDtypeStruct((s, FOX_HEADS * LANES), BF16),
        scratch_shapes=[pltpu.VMEM((1, LANES), F32)],
        compiler_params=_params(("arbitrary",)),
        name="cum",
    )(ff, b128, place)


def _ret_consts():
    h = np.arange(RET_HEADS, dtype=np.float32)
    log_g = np.log1p(-np.exp2(-5.0 - h)).astype(np.float32)
    i = np.arange(RET_CHUNK, dtype=np.float32)
    diff = i[:, None] - i[None, :]
    dmat = np.where(diff >= 0, np.exp(np.maximum(diff, 0.0)[None] * log_g[:, None, None]), 0.0)
    kdec = np.exp((RET_CHUNK - 1.0 - i)[None, :] * log_g[:, None])
    qdec = np.exp((i + 1.0)[None, :] * log_g[:, None])
    cdec = np.exp(RET_CHUNK * log_g)
    kdec = np.broadcast_to(kdec[:, :, None], (RET_HEADS, RET_CHUNK, RET_DK))
    qdec = np.broadcast_to(qdec[:, :, None], (RET_HEADS, RET_CHUNK, RET_DK))
    half = RET_DK // 2
    inv = (ROPE_BASE ** (-np.arange(half, dtype=np.float32) / half)).astype(np.float32)
    inv_full = np.concatenate([inv, inv])[None, :]
    sign = np.concatenate([-np.ones(half, np.float32), np.ones(half, np.float32)])[None, :]
    return (dmat.astype(np.float32), np.ascontiguousarray(kdec, np.float32),
            np.ascontiguousarray(qdec, np.float32), [float(v) for v in cdec],
            inv_full.astype(np.float32), sign)


def _ret_kernel(cdec, rq_ref, rk_ref, rv_ref, rg_ref, pos_ref, inv_ref, sign_ref,
                dmat_ref, kdec_ref, qdec_ref, gn_ref, o_ref, state):
    @pl.when(pl.program_id(0) == 0)
    def _():
        state[...] = jnp.zeros_like(state)

    ang = pos_ref[...].astype(F32) * inv_ref[...]
    cosf = jnp.cos(ang)
    sinf = jnp.sin(ang) * sign_ref[...]
    kscale = RET_DK ** -0.5
    for h in range(RET_HEADS):
        q = rq_ref[:, h * RET_DK:(h + 1) * RET_DK]
        k = rk_ref[:, h * RET_DK:(h + 1) * RET_DK]
        qr = q * cosf + pltpu.roll(q, RET_DK // 2, 1) * sinf
        kr = (k * cosf + pltpu.roll(k, RET_DK // 2, 1) * sinf) * kscale
        vb = rv_ref[:, h * RET_DV:(h + 1) * RET_DV].astype(BF16)
        scores = _dot_nt(qr.astype(BF16), kr.astype(BF16)) * dmat_ref[h]
        intra = _dot(scores.astype(BF16), vb)
        st = state[h]
        cross = _dot((qr * qdec_ref[h]).astype(BF16), st.astype(BF16))
        kd = (kr * kdec_ref[h]).astype(BF16)
        kv = lax.dot_general(kd, vb, _TN, preferred_element_type=F32)
        state[h] = st * cdec[h] + kv
        y = intra + cross
        mu = jnp.mean(y, axis=-1, keepdims=True)
        dlt = y - mu
        var = jnp.mean(dlt * dlt, axis=-1, keepdims=True)
        g = rg_ref[:, h * RET_DV:(h + 1) * RET_DV]
        yn = dlt * lax.rsqrt(var + EPS) * gn_ref[:, h * RET_DV:(h + 1) * RET_DV] * (g / (1.0 + jnp.exp(-g)))
        o_ref[:, h * RET_DV:(h + 1) * RET_DV] = yn.astype(BF16)


def retention(proj, pos_col, gn_g):
    s = proj.shape[0]
    c = RET_CHUNK
    dmat, kdec, qdec, cdec, inv_full, sign = _ret_consts()
    full3 = lambda n: (0, 0, 0)
    return pl.pallas_call(
        functools.partial(_ret_kernel, cdec),
        grid=(s // c,),
        in_specs=[pl.BlockSpec((c, RET_QK), lambda n: (n, 0)),
                  pl.BlockSpec((c, RET_QK), lambda n: (n, 1)),
                  pl.BlockSpec((c, RET_WIDTH), lambda n: (n, 1)),
                  pl.BlockSpec((c, RET_WIDTH), lambda n: (n, 2)),
                  pl.BlockSpec((c, 1), lambda n: (n, 0)),
                  pl.BlockSpec((1, RET_DK), lambda n: (0, 0)),
                  pl.BlockSpec((1, RET_DK), lambda n: (0, 0)),
                  pl.BlockSpec((RET_HEADS, c, c), full3),
                  pl.BlockSpec((RET_HEADS, c, RET_DK), full3),
                  pl.BlockSpec((RET_HEADS, c, RET_DK), full3),
                  pl.BlockSpec((1, RET_WIDTH), lambda n: (0, 0))],
        out_specs=pl.BlockSpec((c, RET_WIDTH), lambda n: (n, 0)),
        out_shape=jax.ShapeDtypeStruct((s, RET_WIDTH), BF16),
        scratch_shapes=[pltpu.VMEM((RET_HEADS, RET_DK, RET_DV), F32)],
        compiler_params=_params(("arbitrary",)),
        name="ret",
    )(proj, proj, proj, proj, pos_col, jnp.asarray(inv_full), jnp.asarray(sign),
      jnp.asarray(dmat), jnp.asarray(kdec), jnp.asarray(qdec), gn_g)


def _fox_kernel(q_ref, k_ref, c_ref, vt_ref, o_ref, m_scr, l_scr, acc_scr, *bufs):
    i = pl.program_id(1)
    bq = q_ref.shape[0]
    bk = bufs[0].shape[0]
    nd = len(bufs)
    assert nd * bk == bq
    sub = 8
    lane = lax.broadcasted_iota(jnp.int32, (bq, FOX_DH), 1)
    qa = jnp.where(lane < 3, -1.0, 0.0).astype(BF16)
    q2 = jnp.concatenate([(q_ref[...] * (FOX_DH ** -0.5 * LOG2E)).astype(BF16), qa], axis=1)
    m_scr[...] = jnp.full_like(m_scr, NEG_BIG)
    l_scr[...] = jnp.zeros_like(l_scr)
    acc_scr[...] = jnp.zeros_like(acc_scr)
    ones_rows = jnp.where(lax.broadcasted_iota(jnp.int32, (2 * sub, bk), 0) == 0, 1.0, 0.0).astype(BF16)

    def logits(j, s_scr, c0=0):
        off = pl.multiple_of(j * bk, bk)
        k2 = jnp.concatenate([k_ref[pl.ds(off, bk), :], c_ref[pl.ds(off, bk), :]], axis=1)
        s_scr[:, c0:bq] = _dot_nt(k2, q2[c0:bq, :])

    def update(j, s_scr, c0=0, diagonal=False):
        w = bq - c0
        st = s_scr[:, c0:bq].reshape(bk // sub, sub, w)
        if diagonal:
            kid = lax.broadcasted_iota(jnp.int32, (bk, bk), 0).reshape(bk // sub, sub, bk)
            qid = lax.broadcasted_iota(jnp.int32, (bk, bk), 1).reshape(bk // sub, sub, bk)
            first = jnp.where(kid <= qid, st[:, :, 0:bk], NEG_BIG)
            st = first if w == bk else jnp.concatenate([first, st[:, :, bk:w]], axis=2)
        m_old = m_scr[:, c0:bq]
        m_new = jnp.maximum(m_old, jnp.max(jnp.max(st, axis=0), axis=0, keepdims=True))
        alpha = jnp.exp2(m_old - m_new)
        p = jnp.exp2(st - jnp.broadcast_to(m_new, (sub, w))[None])
        pv = _dot(jnp.concatenate([vt_ref[j], ones_rows], axis=0), p.reshape(bk, w).astype(BF16))
        l_scr[:, c0:bq] = alpha * l_scr[:, c0:bq] + pv[FOX_DH:FOX_DH + 1, :]
        a8 = jnp.broadcast_to(alpha, (sub, w))[None]
        acc_scr[:, c0:bq] = (a8 * acc_scr[:, c0:bq].reshape(FOX_DH // sub, sub, w)).reshape(FOX_DH, w) + pv[0:FOX_DH, :]
        m_scr[:, c0:bq] = m_new

    ahead = 2
    for k in range(ahead):
        logits(k, bufs[k])

    def body(jj, carry):
        for k in range(nd):
            logits(nd * jj + k + ahead, bufs[(k + ahead) % nd])
            update(nd * jj + k, bufs[k])
        return carry

    lax.fori_loop(0, i, body, 0)
    for k in range(nd):
        if k + ahead < nd:
            logits(nd * i + k + ahead, bufs[k + ahead], (k + ahead) * bk)
        update(nd * i + k, bufs[k], k * bk, diagonal=True)
    o_ref[...] = (acc_scr[...] / l_scr[...]).T


def fox_attention(pf32, kb, caug, vt3, *, bq=2048):
    s = pf32.shape[0]
    bk = vt3.shape[2]
    assert vt3.shape == (s // bk, FOX_WIDTH, bk) and bq % (2 * bk) == 0
    qcol0 = (N_F32_COLS - FOX_WIDTH) // FOX_DH
    return pl.pallas_call(
        _fox_kernel,
        grid=(FOX_HEADS, s // bq),
        in_specs=[pl.BlockSpec((bq, FOX_DH), lambda h, i: (i, qcol0 + h)),
                  pl.BlockSpec((s, FOX_DH), lambda h, i: (0, h)),
                  pl.BlockSpec((s, LANES), lambda h, i: (0, h)),
                  pl.BlockSpec((s // bk, FOX_DH, bk), lambda h, i: (0, h, 0))],
        out_specs=pl.BlockSpec((bq, FOX_DH), lambda h, i: (i, h)),
        out_shape=jax.ShapeDtypeStruct((s, FOX_WIDTH), F32),
        scratch_shapes=[pltpu.VMEM((1, bq), F32), pltpu.VMEM((1, bq), F32), pltpu.VMEM((FOX_DH, bq), F32)]
        + [pltpu.VMEM((bk, bq), F32)] * (bq // bk),
        compiler_params=_params(("arbitrary", "arbitrary")),
        name="fox",
    )(pf32, kb, caug, vt3)


def _outp_kernel(yr_ref, yf_ref, fg_ref, w_ref, x_ref, g1_ref, o_ref, y_scr):
    @pl.when(pl.program_id(1) == 0)
    def _():
        yf = yf_ref[...]
        ms = jnp.mean(yf * yf, axis=-1, keepdims=True)
        yn = yf * lax.rsqrt(ms + EPS) * fg_ref[...]
        y_scr[:, 0:RET_WIDTH] = yr_ref[...]
        y_scr[:, RET_WIDTH:RET_WIDTH + FOX_WIDTH] = yn.astype(BF16)

    o_ref[...] = x_ref[...] + g1_ref[...] * _dot(y_scr[...], w_ref[...])


def out_proj(y_ret, y_fox, fox_g, w_out, x, gate1, *, tm=512, tn=1024):
    s, d = x.shape
    kdim = RET_WIDTH + FOX_WIDTH
    return pl.pallas_call(
        _outp_kernel,
        grid=(s // tm, d // tn),
        in_specs=[pl.BlockSpec((tm, RET_WIDTH), lambda i, j: (i, 0)),
                  pl.BlockSpec((tm, FOX_WIDTH), lambda i, j: (i, 0)),
                  pl.BlockSpec((1, FOX_WIDTH), lambda i, j: (0, 0)),
                  pl.BlockSpec((kdim, tn), lambda i, j: (0, j)),
                  pl.BlockSpec((tm, tn), lambda i, j: (i, j)),
                  pl.BlockSpec((1, tn), lambda i, j: (0, j))],
        out_specs=pl.BlockSpec((tm, tn), lambda i, j: (i, j)),
        out_shape=jax.ShapeDtypeStruct((s, d), F32),
        scratch_shapes=[pltpu.VMEM((tm, kdim), BF16)],
        compiler_params=_params(("arbitrary", "arbitrary")),
        name="outp",
    )(y_ret, y_fox, fox_g, w_out, x, gate1)


def _drop_max(cur):
    mx = jnp.max(jnp.max(cur, axis=0), axis=0, keepdims=True)
    return mx, jnp.where(cur == jnp.broadcast_to(mx, cur.shape[1:])[None], -jnp.inf, cur)


def _top_rows(s, t_scr):
    cur = s.reshape(s.shape[0] // 8, 8, s.shape[1])
    for r in range(PEER_TOPK):
        mx, cur = _drop_max(cur)
        t_scr[r:r + 1, :] = mx


def _peerq_kernel(x_ref, g_ref, sc_ref, sh_ref, wq_ref, keys_ref,
                  h2t_ref, s1_ref, s2_ref, st_ref, h_scr, t1_scr, t2_scr, cand_scr):
    @pl.when(pl.program_id(1) == 0)
    def _():
        h = _norm_mod(x_ref[...], g_ref[...], sc_ref[...], sh_ref[...])
        h_scr[...] = h.astype(BF16)
        h2t_ref[...] = h.T.astype(BF16)

    half = PEER_DQ // 2
    qb = _dot(h_scr[...], wq_ref[...]).astype(BF16)
    s1 = _dot_nt(keys_ref[0, 0], qb[:, 0:half])
    s2 = _dot_nt(keys_ref[0, 1], qb[:, half:PEER_DQ])
    s1_ref[0] = s1
    s2_ref[0] = s2
    _top_rows(s1, t1_scr)
    _top_rows(s2, t2_scr)
    t1_lo = t1_scr[0:8, :]
    for b in range(8):
        cand_scr[b * 8:(b + 1) * 8, :] = t1_lo + t2_scr[b:b + 1, :]
    cand_scr[64:72, :] = t1_scr[0:1, :] + t2_scr[8:16, :]
    cand_scr[72:80, :] = t1_scr[8:16, :] + t2_scr[0:1, :]
    tm = s1.shape[1]
    cand = cand_scr[...].reshape(PEER_NCAND // 8, 8, tm)
    cur = cand
    for r in range(PEER_TOPK):
        tau, cur = _drop_max(cur)
    m1 = t1_scr[0:1, :]
    m2 = t2_scr[0:1, :]
    top8 = jnp.broadcast_to(m1 + m2, (8, tm))[None]
    tau8 = jnp.broadcast_to(tau, (8, tm))[None]
    z = jnp.sum(jnp.sum(jnp.where(cand >= tau8, jnp.exp(cand - top8), 0.0), axis=0), axis=0, keepdims=True)
    st_ref[0, 0:1, :] = tau
    st_ref[0, 1:2, :] = m1 + jnp.log(z)
    st_ref[0, 2:3, :] = m2
    st_ref[0, 3:STAT_ROWS, :] = jnp.zeros((STAT_ROWS - 3, tm), F32)


def peer_query(x1, g, scale, shift, wq, keys, *, tm=512):
    s, d = x1.shape
    vec = pl.BlockSpec((1, d), lambda i, h: (0, 0))
    stat = pl.BlockSpec((1, PEER_NKEYS, tm), lambda i, h: (h, 0, i))
    return pl.pallas_call(
        _peerq_kernel,
        grid=(s // tm, PEER_HEADS),
        in_specs=[pl.BlockSpec((tm, d), lambda i, h: (i, 0)), vec, vec, vec,
                  pl.BlockSpec((d, PEER_DQ), lambda i, h: (0, h)),
                  pl.BlockSpec((1, 2, PEER_NKEYS, PEER_DQ // 2), lambda i, h: (h, 0, 0, 0))],
        out_specs=[pl.BlockSpec((d, tm), lambda i, h: (0, i)), stat, stat,
                   pl.BlockSpec((1, STAT_ROWS, tm), lambda i, h: (h, 0, i))],
        out_shape=[jax.ShapeDtypeStruct((d, s), BF16),
                   jax.ShapeDtypeStruct((PEER_HEADS, PEER_NKEYS, s), F32),
                   jax.ShapeDtypeStruct((PEER_HEADS, PEER_NKEYS, s), F32),
                   jax.ShapeDtypeStruct((PEER_HEADS, STAT_ROWS, s), F32)],
        scratch_shapes=[pltpu.VMEM((tm, d), BF16),
                        pltpu.VMEM((PEER_TOPK, tm), F32), pltpu.VMEM((PEER_TOPK, tm), F32),
                        pltpu.VMEM((PEER_NCAND, tm), F32)],
        compiler_params=_params(("arbitrary", "arbitrary")),
        name="peerq",
    )(x1, g, scale, shift, wq, keys)


def _gelu_tanh(x):
    return 0.5 * x * (1.0 + jnp.tanh(0.7978845608028654 * (x + 0.044715 * (x * x * x))))


def _peer_kernel(h2t_ref, u_ref, vt_ref, s1_ref, s2_ref, st_ref, o_ref, e1_scr, e2_scr, g_scr, a_scr, w_scr):
    n = pl.program_id(1)
    tm = h2t_ref.shape[1]
    tn = u_ref.shape[0]
    sub = 8
    nrow = tn // PEER_NKEYS

    @pl.when(n == 0)
    def _():
        o_ref[...] = jnp.zeros_like(o_ref)
        for h in range(PEER_HEADS):
            e1_scr[h] = jnp.exp(s1_ref[h] - st_ref[h, 1:2, :])
            e2_scr[h] = jnp.exp(s2_ref[h] - st_ref[h, 2:3, :])

    tau = [jnp.broadcast_to(st_ref[h, 0:1, :], (sub, tm)) for h in range(PEER_HEADS)]
    for c in range(nrow):
        i1 = n * nrow + c
        s1row = [jnp.broadcast_to(s1_ref[h, pl.ds(i1, 1), :], (sub, tm)) for h in range(PEER_HEADS)]
        e1row = [jnp.broadcast_to(e1_scr[h, pl.ds(i1, 1), :], (sub, tm)) for h in range(PEER_HEADS)]
        for r0 in range(0, PEER_NKEYS, sub):
            gate = None
            for h in range(PEER_HEADS):
                sm = s2_ref[h, r0:r0 + sub, :] + s1row[h]
                gv = jnp.where(sm >= tau[h], e2_scr[h, r0:r0 + sub, :] * e1row[h], 0.0)
                gate = gv if gate is None else gate + gv
            g_scr[c * PEER_NKEYS + r0:c * PEER_NKEYS + r0 + sub, :] = gate

    a_scr[...] = _dot(u_ref[...], h2t_ref[...])
    rows = 2 * sub
    for r0 in range(0, tn, rows):
        w_scr[r0:r0 + rows, :] = (_gelu_tanh(a_scr[r0:r0 + rows, :]) * g_scr[r0:r0 + rows, :]).astype(BF16)
    o_ref[...] += _dot(vt_ref[...], w_scr[...])


def peer_experts(h2t, u, vt, s1, s2, st, *, tm=512, tn=512):
    d, s = h2t.shape
    ne = u.shape[0]
    once = pl.Buffered(1)
    stat = pl.BlockSpec((PEER_HEADS, PEER_NKEYS, tm), lambda i, n: (0, 0, i), pipeline_mode=once)
    return pl.pallas_call(
        _peer_kernel,
        grid=(s // tm, ne // tn),
        in_specs=[pl.BlockSpec((d, tm), lambda i, n: (0, i), pipeline_mode=once),
                  pl.BlockSpec((tn, d), lambda i, n: (n, 0)),
                  pl.BlockSpec((d, tn), lambda i, n: (0, n)),
                  stat, stat,
                  pl.BlockSpec((PEER_HEADS, STAT_ROWS, tm), lambda i, n: (0, 0, i), pipeline_mode=once)],
        out_specs=pl.BlockSpec((d, tm), lambda i, n: (0, i)),
        out_shape=jax.ShapeDtypeStruct((d, s), F32),
        scratch_shapes=[pltpu.VMEM((PEER_HEADS, PEER_NKEYS, tm), F32),
                        pltpu.VMEM((PEER_HEADS, PEER_NKEYS, tm), F32),
                        pltpu.VMEM((tn, tm), F32), pltpu.VMEM((tn, tm), F32), pltpu.VMEM((tn, tm), BF16)],
        compiler_params=_params(("arbitrary", "arbitrary")),
        name="peer",
    )(h2t, u, vt, s1, s2, st)


def _fin_kernel(x_ref, pt_ref, g2_ref, fg_ref, o_ref):
    x2 = x_ref[...] + g2_ref[...] * pt_ref[...].T
    ms = jnp.mean(x2 * x2, axis=-1, keepdims=True)
    o_ref[...] = x2 * lax.rsqrt(ms + EPS) * fg_ref[...]


def final_norm(x1, peer_t, gate2, final_g, *, tm=256):
    s, d = x1.shape
    vec = pl.BlockSpec((1, d), lambda i: (0, 0))
    return pl.pallas_call(
        _fin_kernel,
        grid=(s // tm,),
        in_specs=[pl.BlockSpec((tm, d), lambda i: (i, 0)),
                  pl.BlockSpec((d, tm), lambda i: (0, i)), vec, vec],
        out_specs=pl.BlockSpec((tm, d), lambda i: (i, 0)),
        out_shape=jax.ShapeDtypeStruct((s, d), F32),
        compiler_params=_params(("arbitrary",)),
        name="fin",
    )(x1, peer_t, gate2, final_g)


def _layer(x, mod, positions, norm1_g, w_in, b_forget, ret_gn_g, fox_norm_g, w_out, norm2_g,
           w_peer_q, peer_sub_keys, peer_u, peer_v, *, bq=2048, bk=512):
    s, d = x.shape
    shift1, scale1, gate1, shift2, scale2, gate2 = [mod[:, k * d:(k + 1) * d] for k in range(6)]
    row = lambda v: v.reshape(1, -1)

    w_main = cast_bf16(w_in, cols=IN_MAIN, name="cast_in")
    n_ff = w_in.shape[1] - IN_MAIN
    w_ff = jnp.pad(w_in[:, IN_MAIN:], ((0, 0), (0, LANES - n_ff))).astype(BF16)
    pf32, kb, vt3, ff = in_proj(x, row(norm1_g), scale1, shift1, w_main, w_ff, tm=bk)

    b128 = jnp.pad(b_forget, (0, LANES - n_ff)).reshape(1, LANES)
    caug = cum_gate(ff, b128)
    y_ret = retention(pf32, positions.reshape(s, 1), row(ret_gn_g))
    y_fox = fox_attention(pf32, kb, caug, vt3, bq=bq)
    x1 = out_proj(y_ret, y_fox, row(fox_norm_g), cast_bf16(w_out, name="cast_out"), x, gate1)

    h2t, s1, s2, st = peer_query(x1, row(norm2_g), scale2, shift2,
                                 cast_bf16(w_peer_q, name="cast_q"), peer_sub_keys.astype(BF16))
    peer_t = peer_experts(h2t, cast_bf16(peer_u, name="cast_u"),
                          cast_bf16(peer_v, transpose=True, br=512, bc=1024, name="cast_v"), s1, s2, st)
    return x1, peer_t, gate2


def kernel(x, c, positions, w_ada, b_ada, norm1_g, w_in, b_forget, ret_gn_g, fox_norm_g, w_out,
           norm2_g, w_peer_q, peer_sub_keys, peer_u, peer_v, final_g):
    b, s, d = x.shape
    assert b == 1 and w_ada.shape[0] == 1, "one sequence, one layer"
    c8 = jnp.broadcast_to(c, (8, d))
    mod = ada_mod(c8, w_ada[0], b_ada[0].reshape(1, -1))[0:1]
    x1, peer_t, gate2 = _layer(x[0], mod, positions[0], norm1_g[0], w_in[0], b_forget[0], ret_gn_g[0],
                               fox_norm_g[0], w_out[0], norm2_g[0], w_peer_q[0], peer_sub_keys[0],
                               peer_u[0], peer_v[0])
    return final_norm(x1, peer_t, gate2, final_g.reshape(1, -1))[None]
```

```python
import functools

import numpy as np
import jax
import jax.numpy as jnp
from jax import lax
from jax.experimental import pallas as pl
from jax.experimental.pallas import tpu as pltpu

F32 = jnp.float32
BF16 = jnp.bfloat16

LANES = 128
RET_HEADS = 8
RET_DK = 128
RET_DV = 256
RET_QK = RET_HEADS * RET_DK
RET_WIDTH = RET_HEADS * RET_DV
RET_CHUNK = 128
FOX_HEADS = 16
FOX_DH = 128
FOX_WIDTH = FOX_HEADS * FOX_DH
IN_MAIN = 2 * RET_QK + 2 * RET_WIDTH + 3 * FOX_WIDTH
N_F32_COLS = 2 * RET_QK + 2 * RET_WIDTH + FOX_WIDTH
ROPE_BASE = 10000.0
PEER_HEADS = 8
PEER_NKEYS = 128
PEER_NEXPERTS = PEER_NKEYS * PEER_NKEYS
PEER_DQ = 256
PEER_TOPK = 16
PEER_NCAND = 80
EPS = 1e-6
NEG_BIG = -1e30
LOG2E = 1.4426950408889634
VMEM_LIMIT = 56 * 1024 * 1024

_NT = (((1,), (1,)), ((), ()))
_TN = (((0,), (0,)), ((), ()))


def _params(sem):
    return pltpu.CompilerParams(dimension_semantics=sem, vmem_limit_bytes=VMEM_LIMIT)


def _dot(a, b):
    return jnp.dot(a, b, preferred_element_type=F32)


def _dot_nt(a, b):
    return lax.dot_general(a, b, _NT, preferred_element_type=F32)


def _ada_kernel(c_ref, w_ref, b_ref, o_ref):
    c = c_ref[...]
    ca = (c / (1.0 + jnp.exp(-c))).astype(BF16)
    o_ref[...] = _dot(ca, w_ref[...].astype(BF16)) + b_ref[...]


def ada_mod(c8, w, b, *, bn=512):
    d, n = w.shape
    return pl.pallas_call(
        _ada_kernel,
        grid=(n // bn,),
        in_specs=[pl.BlockSpec((8, d), lambda j: (0, 0)),
                  pl.BlockSpec((d, bn), lambda j: (0, j)),
                  pl.BlockSpec((1, bn), lambda j: (0, j))],
        out_specs=pl.BlockSpec((8, bn), lambda j: (0, j)),
        out_shape=jax.ShapeDtypeStruct((8, n), F32),
        compiler_params=_params(("arbitrary",)),
        name="ada",
    )(c8, w, b)


def _cast_kernel(x_ref, o_ref):
    o_ref[...] = x_ref[...].astype(BF16)


def _cast_t_kernel(x_ref, o_ref):
    o_ref[...] = x_ref[...].T.astype(BF16)


def cast_bf16(x, *, cols=None, br=512, bc=2048, transpose=False, name="cast"):
    r, c = x.shape
    c = c if cols is None else cols
    br, bc = min(br, r), min(bc, c)
    if transpose:
        return pl.pallas_call(
            _cast_t_kernel, grid=(r // br, c // bc),
            in_specs=[pl.BlockSpec((br, bc), lambda i, j: (i, j))],
            out_specs=pl.BlockSpec((bc, br), lambda i, j: (j, i)),
            out_shape=jax.ShapeDtypeStruct((c, r), BF16),
            compiler_params=_params(("arbitrary", "arbitrary")), name=name)(x)
    return pl.pallas_call(
        _cast_kernel, grid=(r // br, c // bc),
        in_specs=[pl.BlockSpec((br, bc), lambda i, j: (i, j))],
        out_specs=pl.BlockSpec((br, bc), lambda i, j: (i, j)),
        out_shape=jax.ShapeDtypeStruct((r, c), BF16),
        compiler_params=_params(("arbitrary", "arbitrary")), name=name)(x)


def _norm_mod(x, g, scale, shift):
    ms = jnp.mean(x * x, axis=-1, keepdims=True)
    y = x * lax.rsqrt(ms + EPS) * g
    return y * (1.0 + scale) + shift


def _inpj_kernel(nf, nk, x_ref, g_ref, sc_ref, sh_ref, w_ref, wf_ref, o_ref, ok_ref, ovt_ref, of_ref, h_scr):
    j = pl.program_id(1)

    @pl.when(j == 0)
    def _():
        hb = _norm_mod(x_ref[...], g_ref[...], sc_ref[...], sh_ref[...]).astype(BF16)
        h_scr[...] = hb
        of_ref[...] = _dot(hb, wf_ref[...])

    r = _dot(h_scr[...], w_ref[...])

    @pl.when(j < nf)
    def _():
        o_ref[...] = r

    @pl.when(jnp.logical_and(j >= nf, j < nf + nk))
    def _():
        ok_ref[...] = r.astype(BF16)

    @pl.when(j >= nf + nk)
    def _():
        ovt_ref[0] = r.T.astype(BF16)


def in_proj(x, g, scale, shift, w_main, w_ff, *, tm=512, tn=1024):
    s, d = x.shape
    n = w_main.shape[1]
    nf, nk = N_F32_COLS // tn, FOX_WIDTH // tn
    vec = pl.BlockSpec((1, d), lambda i, j: (0, 0))
    return pl.pallas_call(
        functools.partial(_inpj_kernel, nf, nk),
        grid=(s // tm, n // tn),
        in_specs=[pl.BlockSpec((tm, d), lambda i, j: (i, 0)), vec, vec, vec,
                  pl.BlockSpec((d, tn), lambda i, j: (0, j)),
                  pl.BlockSpec((d, LANES), lambda i, j: (0, 0))],
        out_specs=[pl.BlockSpec((tm, tn), lambda i, j: (i, jnp.minimum(j, nf - 1))),
                   pl.BlockSpec((tm, tn), lambda i, j: (i, jnp.clip(j - nf, 0, nk - 1))),
                   pl.BlockSpec((1, tn, tm), lambda i, j: (i, jnp.clip(j - nf - nk, 0, nk - 1), 0)),
                   pl.BlockSpec((tm, LANES), lambda i, j: (i, 0))],
        out_shape=[jax.ShapeDtypeStruct((s, N_F32_COLS), F32),
                   jax.ShapeDtypeStruct((s, FOX_WIDTH), BF16),
                   jax.ShapeDtypeStruct((s // tm, FOX_WIDTH, tm), BF16),
                   jax.ShapeDtypeStruct((s, LANES), F32)],
        scratch_shapes=[pltpu.VMEM((tm, d), BF16)],
        compiler_params=_params(("arbitrary", "arbitrary")),
        name="inpj",
    )(x, g, scale, shift, w_main, w_ff)


def _split3(v):
    hi = v.astype(BF16)
    r1 = v - hi.astype(F32)
    mid = r1.astype(BF16)
    lo = (r1 - mid.astype(F32)).astype(BF16)
    return hi, mid, lo


def _cum_place():
    m = np.zeros((3 * LANES, FOX_HEADS * LANES), np.float32)
    for p in range(3):
        for h in range(FOX_HEADS):
            m[p * LANES + h, h * LANES + p] = 1.0
    return m


def _cum_kernel(ff_ref, b_ref, place_ref, o_ref, carry):
    @pl.when(pl.program_id(0) == 0)
    def _():
        carry[...] = jnp.zeros_like(carry)

    z = ff_ref[...] + b_ref[...]
    logf = jnp.minimum(z, 0.0) - jnp.log(1.0 + jnp.exp(-jnp.abs(z)))
    tc = z.shape[0]
    row = lax.broadcasted_iota(jnp.int32, (tc, tc), 0)
    col = lax.broadcasted_iota(jnp.int32, (tc, tc), 1)
    tri = jnp.where(row >= col, 1.0, 0.0).astype(BF16)
    hi, mid, lo = _split3(logf)
    cum = _dot(tri, hi) + _dot(tri, mid) + _dot(tri, lo) + carry[...]
    carry[...] = cum[tc - 1:tc, :]
    pieces = jnp.concatenate(_split3(cum * LOG2E), axis=1)
    o_ref[...] = _dot(pieces, place_ref[...]).astype(BF16)


def cum_gate(ff, b128, *, tc=256):
    s = ff.shape[0]
    place = jnp.asarray(_cum_place(), BF16)
    return pl.pallas_call(
        _cum_kernel,
        grid=(s // tc,),
        in_specs=[pl.BlockSpec((tc, LANES), lambda i: (i, 0)),
                  pl.BlockSpec((1, LANES), lambda i: (0, 0)),
                  pl.BlockSpec(place.shape, lambda i: (0, 0))],
        out_specs=pl.BlockSpec((tc, FOX_HEADS * LANES), lambda i: (i, 0)),
        out_shape=jax.ShapeDtypeStruct((s, FOX_HEADS * LANES), BF16),
        scratch_shapes=[pltpu.VMEM((1, LANES), F32)],
        compiler_params=_params(("arbitrary",)),
        name="cum",
    )(ff, b128, place)


def _ret_consts():
    h = np.arange(RET_HEADS, dtype=np.float32)
    log_g = np.log1p(-np.exp2(-5.0 - h)).astype(np.float32)
    i = np.arange(RET_CHUNK, dtype=np.float32)
    diff = i[:, None] - i[None, :]
    dmat = np.where(diff >= 0, np.exp(np.maximum(diff, 0.0)[None] * log_g[:, None, None]), 0.0)
    kdec = np.exp((RET_CHUNK - 1.0 - i)[None, :] * log_g[:, None])
    qdec = np.exp((i + 1.0)[None, :] * log_g[:, None])
    cdec = np.exp(RET_CHUNK * log_g)
    kdec = np.broadcast_to(kdec[:, :, None], (RET_HEADS, RET_CHUNK, RET_DK))
    qdec = np.broadcast_to(qdec[:, :, None], (RET_HEADS, RET_CHUNK, RET_DK))
    half = RET_DK // 2
    inv = (ROPE_BASE ** (-np.arange(half, dtype=np.float32) / half)).astype(np.float32)
    inv_full = np.concatenate([inv, inv])[None, :]
    sign = np.concatenate([-np.ones(half, np.float32), np.ones(half, np.float32)])[None, :]
    return (dmat.astype(np.float32), np.ascontiguousarray(kdec, np.float32),
            np.ascontiguousarray(qdec, np.float32), [float(v) for v in cdec],
            inv_full.astype(np.float32), sign)


def _ret_kernel(cdec, rq_ref, rk_ref, rv_ref, rg_ref, pos_ref, inv_ref, sign_ref,
                dmat_ref, kdec_ref, qdec_ref, gn_ref, o_ref, state):
    @pl.when(pl.program_id(0) == 0)
    def _():
        state[...] = jnp.zeros_like(state)

    ang = pos_ref[...].astype(F32) * inv_ref[...]
    cosf = jnp.cos(ang)
    sinf = jnp.sin(ang) * sign_ref[...]
    kscale = RET_DK ** -0.5
    for h in range(RET_HEADS):
        q = rq_ref[:, h * RET_DK:(h + 1) * RET_DK]
        k = rk_ref[:, h * RET_DK:(h + 1) * RET_DK]
        qr = q * cosf + pltpu.roll(q, RET_DK // 2, 1) * sinf
        kr = (k * cosf + pltpu.roll(k, RET_DK // 2, 1) * sinf) * kscale
        vb = rv_ref[:, h * RET_DV:(h + 1) * RET_DV].astype(BF16)
        scores = _dot_nt(qr.astype(BF16), kr.astype(BF16)) * dmat_ref[h]
        intra = _dot(scores.astype(BF16), vb)
        st = state[h]
        cross = _dot((qr * qdec_ref[h]).astype(BF16), st.astype(BF16))
        kd = (kr * kdec_ref[h]).astype(BF16)
        kv = lax.dot_general(kd, vb, _TN, preferred_element_type=F32)
        state[h] = st * cdec[h] + kv
        y = intra + cross
        mu = jnp.mean(y, axis=-1, keepdims=True)
        dlt = y - mu
        var = jnp.mean(dlt * dlt, axis=-1, keepdims=True)
        g = rg_ref[:, h * RET_DV:(h + 1) * RET_DV]
        yn = dlt * lax.rsqrt(var + EPS) * gn_ref[:, h * RET_DV:(h + 1) * RET_DV] * (g / (1.0 + jnp.exp(-g)))
        o_ref[:, h * RET_DV:(h + 1) * RET_DV] = yn.astype(BF16)


def retention(proj, pos_col, gn_g):
    s = proj.shape[0]
    c = RET_CHUNK
    dmat, kdec, qdec, cdec, inv_full, sign = _ret_consts()
    full3 = lambda n: (0, 0, 0)
    return pl.pallas_call(
        functools.partial(_ret_kernel, cdec),
        grid=(s // c,),
        in_specs=[pl.BlockSpec((c, RET_QK), lambda n: (n, 0)),
                  pl.BlockSpec((c, RET_QK), lambda n: (n, 1)),
                  pl.BlockSpec((c, RET_WIDTH), lambda n: (n, 1)),
                  pl.BlockSpec((c, RET_WIDTH), lambda n: (n, 2)),
                  pl.BlockSpec((c, 1), lambda n: (n, 0)),
                  pl.BlockSpec((1, RET_DK), lambda n: (0, 0)),
                  pl.BlockSpec((1, RET_DK), lambda n: (0, 0)),
                  pl.BlockSpec((RET_HEADS, c, c), full3),
                  pl.BlockSpec((RET_HEADS, c, RET_DK), full3),
                  pl.BlockSpec((RET_HEADS, c, RET_DK), full3),
                  pl.BlockSpec((1, RET_WIDTH), lambda n: (0, 0))],
        out_specs=pl.BlockSpec((c, RET_WIDTH), lambda n: (n, 0)),
        out_shape=jax.ShapeDtypeStruct((s, RET_WIDTH), BF16),
        scratch_shapes=[pltpu.VMEM((RET_HEADS, RET_DK, RET_DV), F32)],
        compiler_params=_params(("arbitrary",)),
        name="ret",
    )(proj, proj, proj, proj, pos_col, jnp.asarray(inv_full), jnp.asarray(sign),
      jnp.asarray(dmat), jnp.asarray(kdec), jnp.asarray(qdec), gn_g)


def _fox_kernel(q_ref, k_ref, c_ref, vt_ref, o_ref, m_scr, l_scr, acc_scr, *bufs):
    i = pl.program_id(1)
    bq = q_ref.shape[0]
    bk = bufs[0].shape[0]
    nd = len(bufs)
    assert nd * bk == bq
    sub = 8
    lane = lax.broadcasted_iota(jnp.int32, (bq, FOX_DH), 1)
    qa = jnp.where(lane < 3, -1.0, 0.0).astype(BF16)
    q2 = jnp.concatenate([(q_ref[...] * (FOX_DH ** -0.5 * LOG2E)).astype(BF16), qa], axis=1)
    m_scr[...] = jnp.full_like(m_scr, NEG_BIG)
    l_scr[...] = jnp.zeros_like(l_scr)
    acc_scr[...] = jnp.zeros_like(acc_scr)
    ones_rows = jnp.where(lax.broadcasted_iota(jnp.int32, (2 * sub, bk), 0) == 0, 1.0, 0.0).astype(BF16)

    def logits(j, s_scr, c0=0):
        off = pl.multiple_of(j * bk, bk)
        k2 = jnp.concatenate([k_ref[pl.ds(off, bk), :], c_ref[pl.ds(off, bk), :]], axis=1)
        s_scr[:, c0:bq] = _dot_nt(k2, q2[c0:bq, :])

    def update(j, s_scr, c0=0, diagonal=False):
        w = bq - c0
        st = s_scr[:, c0:bq].reshape(bk // sub, sub, w)
        if diagonal:
            kid = lax.broadcasted_iota(jnp.int32, (bk, bk), 0).reshape(bk // sub, sub, bk)
            qid = lax.broadcasted_iota(jnp.int32, (bk, bk), 1).reshape(bk // sub, sub, bk)
            first = jnp.where(kid <= qid, st[:, :, 0:bk], NEG_BIG)
            st = first if w == bk else jnp.concatenate([first, st[:, :, bk:w]], axis=2)
        m_old = m_scr[:, c0:bq]
        m_new = jnp.maximum(m_old, jnp.max(jnp.max(st, axis=0), axis=0, keepdims=True))
        alpha = jnp.exp2(m_old - m_new)
        p = jnp.exp2(st - jnp.broadcast_to(m_new, (sub, w))[None])
        pv = _dot(jnp.concatenate([vt_ref[j], ones_rows], axis=0), p.reshape(bk, w).astype(BF16))
        l_scr[:, c0:bq] = alpha * l_scr[:, c0:bq] + pv[FOX_DH:FOX_DH + 1, :]
        a8 = jnp.broadcast_to(alpha, (sub, w))[None]
        acc_scr[:, c0:bq] = (a8 * acc_scr[:, c0:bq].reshape(FOX_DH // sub, sub, w)).reshape(FOX_DH, w) + pv[0:FOX_DH, :]
        m_scr[:, c0:bq] = m_new

    ahead = 2
    for k in range(ahead):
        logits(k, bufs[k])

    def body(jj, carry):
        for k in range(nd):
            logits(nd * jj + k + ahead, bufs[(k + ahead) % nd])
            update(nd * jj + k, bufs[k])
        return carry

    lax.fori_loop(0, i, body, 0)
    for k in range(nd):
        if k + ahead < nd:
            logits(nd * i + k + ahead, bufs[k + ahead], (k + ahead) * bk)
        update(nd * i + k, bufs[k], k * bk, diagonal=True)
    o_ref[...] = (acc_scr[...] / l_scr[...]).T


def fox_attention(pf32, kb, caug, vt3, *, bq=2048):
    s = pf32.shape[0]
    bk = vt3.shape[2]
    assert vt3.shape == (s // bk, FOX_WIDTH, bk) and bq % (2 * bk) == 0
    qcol0 = (N_F32_COLS - FOX_WIDTH) // FOX_DH
    return pl.pallas_call(
        _fox_kernel,
        grid=(FOX_HEADS, s // bq),
        in_specs=[pl.BlockSpec((bq, FOX_DH), lambda h, i: (i, qcol0 + h)),
                  pl.BlockSpec((s, FOX_DH), lambda h, i: (0, h)),
                  pl.BlockSpec((s, LANES), lambda h, i: (0, h)),
                  pl.BlockSpec((s // bk, FOX_DH, bk), lambda h, i: (0, h, 0))],
        out_specs=pl.BlockSpec((bq, FOX_DH), lambda h, i: (i, h)),
        out_shape=jax.ShapeDtypeStruct((s, FOX_WIDTH), F32),
        scratch_shapes=[pltpu.VMEM((1, bq), F32), pltpu.VMEM((1, bq), F32), pltpu.VMEM((FOX_DH, bq), F32)]
        + [pltpu.VMEM((bk, bq), F32)] * (bq // bk),
        compiler_params=_params(("arbitrary", "arbitrary")),
        name="fox",
    )(pf32, kb, caug, vt3)


def _outp_kernel(yr_ref, yf_ref, fg_ref, w_ref, x_ref, g1_ref, o_ref, y_scr):
    @pl.when(pl.program_id(1) == 0)
    def _():
        yf = yf_ref[...]
        ms = jnp.mean(yf * yf, axis=-1, keepdims=True)
        yn = yf * lax.rsqrt(ms + EPS) * fg_ref[...]
        y_scr[:, 0:RET_WIDTH] = yr_ref[...]
        y_scr[:, RET_WIDTH:RET_WIDTH + FOX_WIDTH] = yn.astype(BF16)

    o_ref[...] = x_ref[...] + g1_ref[...] * _dot(y_scr[...], w_ref[...])


def out_proj(y_ret, y_fox, fox_g, w_out, x, gate1, *, tm=512, tn=1024):
    s, d = x.shape
    kdim = RET_WIDTH + FOX_WIDTH
    return pl.pallas_call(
        _outp_kernel,
        grid=(s // tm, d // tn),
        in_specs=[pl.BlockSpec((tm, RET_WIDTH), lambda i, j: (i, 0)),
                  pl.BlockSpec((tm, FOX_WIDTH), lambda i, j: (i, 0)),
                  pl.BlockSpec((1, FOX_WIDTH), lambda i, j: (0, 0)),
                  pl.BlockSpec((kdim, tn), lambda i, j: (0, j)),
                  pl.BlockSpec((tm, tn), lambda i, j: (i, j)),
                  pl.BlockSpec((1, tn), lambda i, j: (0, j))],
        out_specs=pl.BlockSpec((tm, tn), lambda i, j: (i, j)),
        out_shape=jax.ShapeDtypeStruct((s, d), F32),
        scratch_shapes=[pltpu.VMEM((tm, kdim), BF16)],
        compiler_params=_params(("arbitrary", "arbitrary")),
        name="outp",
    )(y_ret, y_fox, fox_g, w_out, x, gate1)


def _drop_max(cur):
    mx = jnp.max(jnp.max(cur, axis=0), axis=0, keepdims=True)
    hit = cur == jnp.broadcast_to(mx, cur.shape[1:])[None]
    return mx, hit, jnp.where(hit, -jnp.inf, cur)


def _top_rows(s, t_scr):
    cur = s.reshape(s.shape[0] // 8, 8, s.shape[1])
    level = jnp.full(cur.shape, float(PEER_TOPK), F32)
    for r in range(PEER_TOPK):
        mx, hit, cur = _drop_max(cur)
        t_scr[r:r + 1, :] = mx
        level = jnp.where(hit, float(r), level)
    return level.reshape(s.shape)


def _peerq_kernel(x_ref, g_ref, sc_ref, sh_ref, wq_ref, keys_ref,
                  h2t_ref, kc_ref, e1_ref, l2_ref, e2_ref, h_scr, t1_scr, t2_scr, cand_scr):
    @pl.when(pl.program_id(1) == 0)
    def _():
        h = _norm_mod(x_ref[...], g_ref[...], sc_ref[...], sh_ref[...])
        h_scr[...] = h.astype(BF16)
        h2t_ref[...] = h.T.astype(BF16)

    half = PEER_DQ // 2
    qb = _dot(h_scr[...], wq_ref[...]).astype(BF16)
    s1 = _dot_nt(keys_ref[0, 0], qb[:, 0:half])
    s2 = _dot_nt(keys_ref[0, 1], qb[:, half:PEER_DQ])
    tm = s1.shape[1]
    _top_rows(s1, t1_scr)
    l2_ref[0] = _top_rows(s2, t2_scr).astype(BF16)
    t1_lo = t1_scr[0:8, :]
    for b in range(8):
        cand_scr[b * 8:(b + 1) * 8, :] = t1_lo + t2_scr[b:b + 1, :]
    cand_scr[64:72, :] = t1_scr[0:1, :] + t2_scr[8:16, :]
    cand_scr[72:80, :] = t1_scr[8:16, :] + t2_scr[0:1, :]
    cand = cand_scr[...].reshape(PEER_NCAND // 8, 8, tm)
    cur = cand
    for r in range(PEER_TOPK):
        tau, _, cur = _drop_max(cur)
    m1 = t1_scr[0:1, :]
    m2 = t2_scr[0:1, :]
    top8 = jnp.broadcast_to(m1 + m2, (8, tm))[None]
    tau8 = jnp.broadcast_to(tau, (8, tm))[None]
    z = jnp.sum(jnp.sum(jnp.where(cand >= tau8, jnp.exp(cand - top8), 0.0), axis=0), axis=0, keepdims=True)
    s1g = s1.reshape(PEER_NKEYS // 8, 8, tm)
    kc = jnp.zeros(s1g.shape, F32)
    for b in range(PEER_TOPK):
        t2b = jnp.broadcast_to(t2_scr[b:b + 1, :], (8, tm))[None]
        kc = jnp.where(s1g + t2b >= tau8, float(b + 1), kc)
    kc_ref[0] = kc.reshape(PEER_NKEYS, tm)
    e1_ref[0] = jnp.exp(s1 - (m1 + jnp.log(z)))
    e2_ref[0] = jnp.exp(s2 - m2).astype(BF16)


def peer_query(x1, g, scale, shift, wq, keys, *, tm=512):
    s, d = x1.shape
    vec = pl.BlockSpec((1, d), lambda i, h: (0, 0))
    stat = pl.BlockSpec((1, PEER_NKEYS, tm), lambda i, h: (h, 0, i))
    table = lambda dt: jax.ShapeDtypeStruct((PEER_HEADS, PEER_NKEYS, s), dt)
    return pl.pallas_call(
        _peerq_kernel,
        grid=(s // tm, PEER_HEADS),
        in_specs=[pl.BlockSpec((tm, d), lambda i, h: (i, 0)), vec, vec, vec,
                  pl.BlockSpec((d, PEER_DQ), lambda i, h: (0, h)),
                  pl.BlockSpec((1, 2, PEER_NKEYS, PEER_DQ // 2), lambda i, h: (h, 0, 0, 0))],
        out_specs=[pl.BlockSpec((d, tm), lambda i, h: (0, i)), stat, stat, stat, stat],
        out_shape=[jax.ShapeDtypeStruct((d, s), BF16), table(F32), table(F32), table(BF16), table(BF16)],
        scratch_shapes=[pltpu.VMEM((tm, d), BF16),
                        pltpu.VMEM((PEER_TOPK, tm), F32), pltpu.VMEM((PEER_TOPK, tm), F32),
                        pltpu.VMEM((PEER_NCAND, tm), F32)],
        compiler_params=_params(("arbitrary", "arbitrary")),
        name="peerq",
    )(x1, g, scale, shift, wq, keys)


def _gelu_tanh(x):
    return 0.5 * x * (1.0 + jnp.tanh(0.7978845608028654 * (x + 0.044715 * (x * x * x))))


def _peer_kernel(h2t_ref, u_ref, vt_ref, kc_ref, e1_ref, l2_ref, e2_ref, o_ref, a_scr, w_scr):
    n = pl.program_id(1)
    tm = h2t_ref.shape[1]
    tn = u_ref.shape[0]
    rows = 16
    nrow = tn // PEER_NKEYS

    @pl.when(n == 0)
    def _():
        o_ref[...] = jnp.zeros_like(o_ref)

    a_scr[...] = _dot(u_ref[...], h2t_ref[...])
    for c in range(nrow):
        i1 = n * nrow + c
        krow = [jnp.broadcast_to(kc_ref[h, pl.ds(i1, 1), :], (rows, tm)).astype(BF16) for h in range(PEER_HEADS)]
        e1row = [jnp.broadcast_to(e1_ref[h, pl.ds(i1, 1), :], (rows, tm)).astype(BF16) for h in range(PEER_HEADS)]
        for r0 in range(0, PEER_NKEYS, rows):
            gate = None
            for h in range(PEER_HEADS):
                gv = jnp.where(l2_ref[h, r0:r0 + rows, :] < krow[h], e2_ref[h, r0:r0 + rows, :] * e1row[h],
                               jnp.zeros((), BF16))
                gate = gv if gate is None else gate + gv
            rr = c * PEER_NKEYS + r0
            w_scr[rr:rr + rows, :] = _gelu_tanh(a_scr[rr:rr + rows, :]).astype(BF16) * gate
    o_ref[...] += _dot(vt_ref[...], w_scr[...])


def peer_experts(h2t, u, vt, kc, e1, l2, e2, *, tm=512, tn=512):
    d, s = h2t.shape
    ne = u.shape[0]
    once = pl.Buffered(1)
    table = pl.BlockSpec((PEER_HEADS, PEER_NKEYS, tm), lambda i, n: (0, 0, i), pipeline_mode=once)
    return pl.pallas_call(
        _peer_kernel,
        grid=(s // tm, ne // tn),
        in_specs=[pl.BlockSpec((d, tm), lambda i, n: (0, i), pipeline_mode=once),
                  pl.BlockSpec((tn, d), lambda i, n: (n, 0)),
                  pl.BlockSpec((d, tn), lambda i, n: (0, n)),
                  table, table, table, table],
        out_specs=pl.BlockSpec((d, tm), lambda i, n: (0, i)),
        out_shape=jax.ShapeDtypeStruct((d, s), F32),
        scratch_shapes=[pltpu.VMEM((tn, tm), F32), pltpu.VMEM((tn, tm), BF16)],
        compiler_params=_params(("arbitrary", "arbitrary")),
        name="peer",
    )(h2t, u, vt, kc, e1, l2, e2)


def _fin_kernel(x_ref, pt_ref, g2_ref, fg_ref, o_ref):
    x2 = x_ref[...] + g2_ref[...] * pt_ref[...].T
    ms = jnp.mean(x2 * x2, axis=-1, keepdims=True)
    o_ref[...] = x2 * lax.rsqrt(ms + EPS) * fg_ref[...]


def final_norm(x1, peer_t, gate2, final_g, *, tm=256):
    s, d = x1.shape
    vec = pl.BlockSpec((1, d), lambda i: (0, 0))
    return pl.pallas_call(
        _fin_kernel,
        grid=(s // tm,),
        in_specs=[pl.BlockSpec((tm, d), lambda i: (i, 0)),
                  pl.BlockSpec((d, tm), lambda i: (0, i)), vec, vec],
        out_specs=pl.BlockSpec((tm, d), lambda i: (i, 0)),
        out_shape=jax.ShapeDtypeStruct((s, d), F32),
        compiler_params=_params(("arbitrary",)),
        name="fin",
    )(x1, peer_t, gate2, final_g)


def _layer(x, mod, positions, norm1_g, w_in, b_forget, ret_gn_g, fox_norm_g, w_out, norm2_g,
           w_peer_q, peer_sub_keys, peer_u, peer_v, *, bq=2048, bk=512):
    s, d = x.shape
    shift1, scale1, gate1, shift2, scale2, gate2 = [mod[:, k * d:(k + 1) * d] for k in range(6)]
    row = lambda v: v.reshape(1, -1)

    w_main = cast_bf16(w_in, cols=IN_MAIN, name="cast_in")
    n_ff = w_in.shape[1] - IN_MAIN
    w_ff = jnp.pad(w_in[:, IN_MAIN:], ((0, 0), (0, LANES - n_ff))).astype(BF16)
    pf32, kb, vt3, ff = in_proj(x, row(norm1_g), scale1, shift1, w_main, w_ff, tm=bk)

    b128 = jnp.pad(b_forget, (0, LANES - n_ff)).reshape(1, LANES)
    caug = cum_gate(ff, b128)
    y_ret = retention(pf32, positions.reshape(s, 1), row(ret_gn_g))
    y_fox = fox_attention(pf32, kb, caug, vt3, bq=bq)
    x1 = out_proj(y_ret, y_fox, row(fox_norm_g), cast_bf16(w_out, name="cast_out"), x, gate1)

    h2t, kc, e1, l2, e2 = peer_query(x1, row(norm2_g), scale2, shift2,
                                     cast_bf16(w_peer_q, name="cast_q"), peer_sub_keys.astype(BF16))
    peer_t = peer_experts(h2t, cast_bf16(peer_u, name="cast_u"),
                          cast_bf16(peer_v, transpose=True, br=512, bc=1024, name="cast_v"), kc, e1, l2, e2)
    return x1, peer_t, gate2


def kernel(x, c, positions, w_ada, b_ada, norm1_g, w_in, b_forget, ret_gn_g, fox_norm_g, w_out,
           norm2_g, w_peer_q, peer_sub_keys, peer_u, peer_v, final_g):
    b, s, d = x.shape
    assert b == 1 and w_ada.shape[0] == 1, "one sequence, one layer"
    c8 = jnp.broadcast_to(c, (8, d))
    mod = ada_mod(c8, w_ada[0], b_ada[0].reshape(1, -1))[0:1]
    x1, peer_t, gate2 = _layer(x[0], mod, positions[0], norm1_g[0], w_in[0], b_forget[0], ret_gn_g[0],
                               fox_norm_g[0], w_out[0], norm2_g[0], w_peer_q[0], peer_sub_keys[0],
                               peer_u[0], peer_v[0])
    return final_norm(x1, peer_t, gate2, final_g.reshape(1, -1))[None]
```

```python
import functools

import numpy as np
import jax
import jax.numpy as jnp
from jax import lax
from jax.experimental import pallas as pl
from jax.experimental.pallas import tpu as pltpu

F32 = jnp.float32
BF16 = jnp.bfloat16

LANES = 128
RET_HEADS = 8
RET_DK = 128
RET_DV = 256
RET_QK = RET_HEADS * RET_DK
RET_WIDTH = RET_HEADS * RET_DV
RET_CHUNK = 128
FOX_HEADS = 16
FOX_DH = 128
FOX_WIDTH = FOX_HEADS * FOX_DH
IN_MAIN = 2 * RET_QK + 2 * RET_WIDTH + 3 * FOX_WIDTH
N_F32_COLS = 2 * RET_QK + 2 * RET_WIDTH + FOX_WIDTH
ROPE_BASE = 10000.0
PEER_HEADS = 8
PEER_NKEYS = 128
PEER_NEXPERTS = PEER_NKEYS * PEER_NKEYS
PEER_DQ = 256
PEER_TOPK = 16
PEER_NCAND = 80
EPS = 1e-6
NEG_BIG = -1e30
LOG2E = 1.4426950408889634
VMEM_LIMIT = 56 * 1024 * 1024

_NT = (((1,), (1,)), ((), ()))
_TN = (((0,), (0,)), ((), ()))


def _params(sem):
    return pltpu.CompilerParams(dimension_semantics=sem, vmem_limit_bytes=VMEM_LIMIT)


def _dot(a, b):
    return jnp.dot(a, b, preferred_element_type=F32)


def _dot_nt(a, b):
    return lax.dot_general(a, b, _NT, preferred_element_type=F32)


def _ada_kernel(c_ref, w_ref, b_ref, o_ref):
    c = c_ref[...]
    ca = (c / (1.0 + jnp.exp(-c))).astype(BF16)
    o_ref[...] = _dot(ca, w_ref[...].astype(BF16)) + b_ref[...]


def ada_mod(c8, w, b, *, bn=512):
    d, n = w.shape
    return pl.pallas_call(
        _ada_kernel,
        grid=(n // bn,),
        in_specs=[pl.BlockSpec((8, d), lambda j: (0, 0)),
                  pl.BlockSpec((d, bn), lambda j: (0, j)),
                  pl.BlockSpec((1, bn), lambda j: (0, j))],
        out_specs=pl.BlockSpec((8, bn), lambda j: (0, j)),
        out_shape=jax.ShapeDtypeStruct((8, n), F32),
        compiler_params=_params(("arbitrary",)),
        name="ada",
    )(c8, w, b)


def _cast_kernel(x_ref, o_ref):
    o_ref[...] = x_ref[...].astype(BF16)


def _cast_t_kernel(x_ref, o_ref):
    o_ref[...] = x_ref[...].T.astype(BF16)


def cast_bf16(x, *, rows=None, cols=None, br=512, bc=2048, transpose=False, name="cast"):
    r, c = x.shape
    r = r if rows is None else rows
    c = c if cols is None else cols
    br, bc = min(br, r), min(bc, c)
    if transpose:
        return pl.pallas_call(
            _cast_t_kernel, grid=(r // br, c // bc),
            in_specs=[pl.BlockSpec((br, bc), lambda i, j: (i, j))],
            out_specs=pl.BlockSpec((bc, br), lambda i, j: (j, i)),
            out_shape=jax.ShapeDtypeStruct((c, r), BF16),
            compiler_params=_params(("arbitrary", "arbitrary")), name=name)(x)
    return pl.pallas_call(
        _cast_kernel, grid=(r // br, c // bc),
        in_specs=[pl.BlockSpec((br, bc), lambda i, j: (i, j))],
        out_specs=pl.BlockSpec((br, bc), lambda i, j: (i, j)),
        out_shape=jax.ShapeDtypeStruct((r, c), BF16),
        compiler_params=_params(("arbitrary", "arbitrary")), name=name)(x)


def _tail_kernel(x_ref, o_ref):
    o_ref[...] = jnp.zeros_like(o_ref)
    o_ref[:, 0:x_ref.shape[0]] = x_ref[...].T.astype(BF16)


def tail_weight(w_t, row0, nrows):
    d = w_t.shape[1]
    assert row0 % nrows == 0 and nrows % 8 == 0
    return pl.pallas_call(
        _tail_kernel, grid=(1,),
        in_specs=[pl.BlockSpec((nrows, d), lambda i: (row0 // nrows, 0))],
        out_specs=pl.BlockSpec((d, LANES), lambda i: (0, 0)),
        out_shape=jax.ShapeDtypeStruct((d, LANES), BF16),
        compiler_params=_params(("arbitrary",)), name="cast_tail")(w_t)


def _norm_mod(x, g, scale, shift):
    ms = jnp.mean(x * x, axis=-1, keepdims=True)
    y = x * lax.rsqrt(ms + EPS) * g
    return y * (1.0 + scale) + shift


def _inpj_kernel(nf, nk, x_ref, g_ref, sc_ref, sh_ref, w_ref, wf_ref, o_ref, ok_ref, ovt_ref, of_ref, h_scr):
    j = pl.program_id(1)

    @pl.when(j == 0)
    def _():
        hb = _norm_mod(x_ref[...], g_ref[...], sc_ref[...], sh_ref[...]).astype(BF16)
        h_scr[...] = hb
        of_ref[...] = _dot(hb, wf_ref[...])

    r = _dot(h_scr[...], w_ref[...])

    @pl.when(j < nf)
    def _():
        o_ref[...] = r

    @pl.when(jnp.logical_and(j >= nf, j < nf + nk))
    def _():
        ok_ref[...] = r.astype(BF16)

    @pl.when(j >= nf + nk)
    def _():
        ovt_ref[0] = r.T.astype(BF16)


def in_proj(x, g, scale, shift, w_main, w_ff, *, tm=512, tn=1024):
    s, d = x.shape
    n = w_main.shape[1]
    nf, nk = N_F32_COLS // tn, FOX_WIDTH // tn
    vec = pl.BlockSpec((1, d), lambda i, j: (0, 0))
    return pl.pallas_call(
        functools.partial(_inpj_kernel, nf, nk),
        grid=(s // tm, n // tn),
        in_specs=[pl.BlockSpec((tm, d), lambda i, j: (i, 0)), vec, vec, vec,
                  pl.BlockSpec((d, tn), lambda i, j: (0, j)),
                  pl.BlockSpec((d, LANES), lambda i, j: (0, 0))],
        out_specs=[pl.BlockSpec((tm, tn), lambda i, j: (i, jnp.minimum(j, nf - 1))),
                   pl.BlockSpec((tm, tn), lambda i, j: (i, jnp.clip(j - nf, 0, nk - 1))),
                   pl.BlockSpec((1, tn, tm), lambda i, j: (i, jnp.clip(j - nf - nk, 0, nk - 1), 0)),
                   pl.BlockSpec((tm, LANES), lambda i, j: (i, 0))],
        out_shape=[jax.ShapeDtypeStruct((s, N_F32_COLS), F32),
                   jax.ShapeDtypeStruct((s, FOX_WIDTH), BF16),
                   jax.ShapeDtypeStruct((s // tm, FOX_WIDTH, tm), BF16),
                   jax.ShapeDtypeStruct((s, LANES), F32)],
        scratch_shapes=[pltpu.VMEM((tm, d), BF16)],
        compiler_params=_params(("arbitrary", "arbitrary")),
        name="inpj",
    )(x, g, scale, shift, w_main, w_ff)


def _split3(v):
    hi = v.astype(BF16)
    r1 = v - hi.astype(F32)
    mid = r1.astype(BF16)
    lo = (r1 - mid.astype(F32)).astype(BF16)
    return hi, mid, lo


def _cum_place():
    m = np.zeros((3 * LANES, FOX_HEADS * LANES), np.float32)
    for p in range(3):
        for h in range(FOX_HEADS):
            m[p * LANES + h, h * LANES + p] = 1.0
    return m


def _cum_kernel(ff_ref, b_ref, place_ref, o_ref, carry):
    @pl.when(pl.program_id(0) == 0)
    def _():
        carry[...] = jnp.zeros_like(carry)

    z = ff_ref[...] + b_ref[...]
    logf = jnp.minimum(z, 0.0) - jnp.log(1.0 + jnp.exp(-jnp.abs(z)))
    tc = z.shape[0]
    row = lax.broadcasted_iota(jnp.int32, (tc, tc), 0)
    col = lax.broadcasted_iota(jnp.int32, (tc, tc), 1)
    tri = jnp.where(row >= col, 1.0, 0.0).astype(BF16)
    hi, mid, lo = _split3(logf)
    cum = _dot(tri, hi) + _dot(tri, mid) + _dot(tri, lo) + carry[...]
    carry[...] = cum[tc - 1:tc, :]
    pieces = jnp.concatenate(_split3(cum * LOG2E), axis=1)
    o_ref[...] = _dot(pieces, place_ref[...]).astype(BF16)


def cum_gate(ff, b128, *, tc=256):
    s = ff.shape[0]
    place = jnp.asarray(_cum_place(), BF16)
    return pl.pallas_call(
        _cum_kernel,
        grid=(s // tc,),
        in_specs=[pl.BlockSpec((tc, LANES), lambda i: (i, 0)),
                  pl.BlockSpec((1, LANES), lambda i: (0, 0)),
                  pl.BlockSpec(place.shape, lambda i: (0, 0))],
        out_specs=pl.BlockSpec((tc, FOX_HEADS * LANES), lambda i: (i, 0)),
        out_shape=jax.ShapeDtypeStruct((s, FOX_HEADS * LANES), BF16),
        scratch_shapes=[pltpu.VMEM((1, LANES), F32)],
        compiler_params=_params(("arbitrary",)),
        name="cum",
    )(ff, b128, place)


def _ret_consts():
    h = np.arange(RET_HEADS, dtype=np.float32)
    log_g = np.log1p(-np.exp2(-5.0 - h)).astype(np.float32)
    i = np.arange(RET_CHUNK, dtype=np.float32)
    diff = i[:, None] - i[None, :]
    dmat = np.where(diff >= 0, np.exp(np.maximum(diff, 0.0)[None] * log_g[:, None, None]), 0.0)
    kdec = np.exp((RET_CHUNK - 1.0 - i)[None, :] * log_g[:, None])
    qdec = np.exp((i + 1.0)[None, :] * log_g[:, None])
    cdec = np.exp(RET_CHUNK * log_g)
    kdec = np.broadcast_to(kdec[:, :, None], (RET_HEADS, RET_CHUNK, RET_DK))
    qdec = np.broadcast_to(qdec[:, :, None], (RET_HEADS, RET_CHUNK, RET_DK))
    half = RET_DK // 2
    inv = (ROPE_BASE ** (-np.arange(half, dtype=np.float32) / half)).astype(np.float32)
    inv_full = np.concatenate([inv, inv])[None, :]
    sign = np.concatenate([-np.ones(half, np.float32), np.ones(half, np.float32)])[None, :]
    return (dmat.astype(np.float32), np.ascontiguousarray(kdec, np.float32),
            np.ascontiguousarray(qdec, np.float32), [float(v) for v in cdec],
            inv_full.astype(np.float32), sign)


def _ret_kernel(cdec, rq_ref, rk_ref, rv_ref, rg_ref, pos_ref, inv_ref, sign_ref,
                dmat_ref, kdec_ref, qdec_ref, gn_ref, o_ref, state):
    @pl.when(pl.program_id(0) == 0)
    def _():
        state[...] = jnp.zeros_like(state)

    ang = pos_ref[...].astype(F32) * inv_ref[...]
    cosf = jnp.cos(ang)
    sinf = jnp.sin(ang) * sign_ref[...]
    kscale = RET_DK ** -0.5
    for h in range(RET_HEADS):
        q = rq_ref[:, h * RET_DK:(h + 1) * RET_DK]
        k = rk_ref[:, h * RET_DK:(h + 1) * RET_DK]
        qr = q * cosf + pltpu.roll(q, RET_DK // 2, 1) * sinf
        kr = (k * cosf + pltpu.roll(k, RET_DK // 2, 1) * sinf) * kscale
        vb = rv_ref[:, h * RET_DV:(h + 1) * RET_DV].astype(BF16)
        scores = _dot_nt(qr.astype(BF16), kr.astype(BF16)) * dmat_ref[h]
        intra = _dot(scores.astype(BF16), vb)
        st = state[h]
        cross = _dot((qr * qdec_ref[h]).astype(BF16), st.astype(BF16))
        kd = (kr * kdec_ref[h]).astype(BF16)
        kv = lax.dot_general(kd, vb, _TN, preferred_element_type=F32)
        state[h] = st * cdec[h] + kv
        y = intra + cross
        mu = jnp.mean(y, axis=-1, keepdims=True)
        dlt = y - mu
        var = jnp.mean(dlt * dlt, axis=-1, keepdims=True)
        g = rg_ref[:, h * RET_DV:(h + 1) * RET_DV]
        yn = dlt * lax.rsqrt(var + EPS) * gn_ref[:, h * RET_DV:(h + 1) * RET_DV] * (g / (1.0 + jnp.exp(-g)))
        o_ref[:, h * RET_DV:(h + 1) * RET_DV] = yn.astype(BF16)


def retention(proj, pos_col, gn_g):
    s = proj.shape[0]
    c = RET_CHUNK
    dmat, kdec, qdec, cdec, inv_full, sign = _ret_consts()
    full3 = lambda n: (0, 0, 0)
    return pl.pallas_call(
        functools.partial(_ret_kernel, cdec),
        grid=(s // c,),
        in_specs=[pl.BlockSpec((c, RET_QK), lambda n: (n, 0)),
                  pl.BlockSpec((c, RET_QK), lambda n: (n, 1)),
                  pl.BlockSpec((c, RET_WIDTH), lambda n: (n, 1)),
                  pl.BlockSpec((c, RET_WIDTH), lambda n: (n, 2)),
                  pl.BlockSpec((c, 1), lambda n: (n, 0)),
                  pl.BlockSpec((1, RET_DK), lambda n: (0, 0)),
                  pl.BlockSpec((1, RET_DK), lambda n: (0, 0)),
                  pl.BlockSpec((RET_HEADS, c, c), full3),
                  pl.BlockSpec((RET_HEADS, c, RET_DK), full3),
                  pl.BlockSpec((RET_HEADS, c, RET_DK), full3),
                  pl.BlockSpec((1, RET_WIDTH), lambda n: (0, 0))],
        out_specs=pl.BlockSpec((c, RET_WIDTH), lambda n: (n, 0)),
        out_shape=jax.ShapeDtypeStruct((s, RET_WIDTH), BF16),
        scratch_shapes=[pltpu.VMEM((RET_HEADS, RET_DK, RET_DV), F32)],
        compiler_params=_params(("arbitrary",)),
        name="ret",
    )(proj, proj, proj, proj, pos_col, jnp.asarray(inv_full), jnp.asarray(sign),
      jnp.asarray(dmat), jnp.asarray(kdec), jnp.asarray(qdec), gn_g)


def _fox_kernel(q_ref, k_ref, c_ref, vt_ref, o_ref, m_scr, l_scr, acc_scr, *bufs):
    i = pl.program_id(1)
    bq = q_ref.shape[0]
    bk = bufs[0].shape[0]
    nd = len(bufs)
    assert nd * bk == bq
    sub = 8
    lane = lax.broadcasted_iota(jnp.int32, (bq, FOX_DH), 1)
    qa = jnp.where(lane < 3, -1.0, 0.0).astype(BF16)
    q2 = jnp.concatenate([(q_ref[...] * (FOX_DH ** -0.5 * LOG2E)).astype(BF16), qa], axis=1)
    m_scr[...] = jnp.full_like(m_scr, NEG_BIG)
    l_scr[...] = jnp.zeros_like(l_scr)
    acc_scr[...] = jnp.zeros_like(acc_scr)
    ones_rows = jnp.where(lax.broadcasted_iota(jnp.int32, (2 * sub, bk), 0) == 0, 1.0, 0.0).astype(BF16)

    def logits(j, s_scr, c0=0):
        off = pl.multiple_of(j * bk, bk)
        k2 = jnp.concatenate([k_ref[pl.ds(off, bk), :], c_ref[pl.ds(off, bk), :]], axis=1)
        s_scr[:, c0:bq] = _dot_nt(k2, q2[c0:bq, :])

    def update(j, s_scr, c0=0, diagonal=False):
        w = bq - c0
        st = s_scr[:, c0:bq].reshape(bk // sub, sub, w)
        if diagonal:
            kid = lax.broadcasted_iota(jnp.int32, (bk, bk), 0).reshape(bk // sub, sub, bk)
            qid = lax.broadcasted_iota(jnp.int32, (bk, bk), 1).reshape(bk // sub, sub, bk)
            first = jnp.where(kid <= qid, st[:, :, 0:bk], NEG_BIG)
            st = first if w == bk else jnp.concatenate([first, st[:, :, bk:w]], axis=2)
        m_old = m_scr[:, c0:bq]
        m_new = jnp.maximum(m_old, jnp.max(jnp.max(st, axis=0), axis=0, keepdims=True))
        alpha = jnp.exp2(m_old - m_new)
        p = jnp.exp2(st - jnp.broadcast_to(m_new, (sub, w))[None])
        pv = _dot(jnp.concatenate([vt_ref[j], ones_rows], axis=0), p.reshape(bk, w).astype(BF16))
        l_scr[:, c0:bq] = alpha * l_scr[:, c0:bq] + pv[FOX_DH:FOX_DH + 1, :]
        a8 = jnp.broadcast_to(alpha, (sub, w))[None]
        acc_scr[:, c0:bq] = (a8 * acc_scr[:, c0:bq].reshape(FOX_DH // sub, sub, w)).reshape(FOX_DH, w) + pv[0:FOX_DH, :]
        m_scr[:, c0:bq] = m_new

    ahead = 2
    for k in range(ahead):
        logits(k, bufs[k])

    def body(jj, carry):
        for k in range(nd):
            logits(nd * jj + k + ahead, bufs[(k + ahead) % nd])
            update(nd * jj + k, bufs[k])
        return carry

    lax.fori_loop(0, i, body, 0)
    for k in range(nd):
        if k + ahead < nd:
            logits(nd * i + k + ahead, bufs[k + ahead], (k + ahead) * bk)
        update(nd * i + k, bufs[k], k * bk, diagonal=True)
    o_ref[...] = (acc_scr[...] / l_scr[...]).T


def fox_attention(pf32, kb, caug, vt3, *, bq=2048):
    s = pf32.shape[0]
    bk = vt3.shape[2]
    assert vt3.shape == (s // bk, FOX_WIDTH, bk) and bq % (2 * bk) == 0
    qcol0 = (N_F32_COLS - FOX_WIDTH) // FOX_DH
    return pl.pallas_call(
        _fox_kernel,
        grid=(FOX_HEADS, s // bq),
        in_specs=[pl.BlockSpec((bq, FOX_DH), lambda h, i: (i, qcol0 + h)),
                  pl.BlockSpec((s, FOX_DH), lambda h, i: (0, h)),
                  pl.BlockSpec((s, LANES), lambda h, i: (0, h)),
                  pl.BlockSpec((s // bk, FOX_DH, bk), lambda h, i: (0, h, 0))],
        out_specs=pl.BlockSpec((bq, FOX_DH), lambda h, i: (i, h)),
        out_shape=jax.ShapeDtypeStruct((s, FOX_WIDTH), F32),
        scratch_shapes=[pltpu.VMEM((1, bq), F32), pltpu.VMEM((1, bq), F32), pltpu.VMEM((FOX_DH, bq), F32)]
        + [pltpu.VMEM((bk, bq), F32)] * (bq // bk),
        compiler_params=_params(("arbitrary", "arbitrary")),
        name="fox",
    )(pf32, kb, caug, vt3)


def _outp_kernel(yr_ref, yf_ref, fg_ref, w_ref, x_ref, g1_ref, o_ref, y_scr):
    @pl.when(pl.program_id(1) == 0)
    def _():
        yf = yf_ref[...]
        ms = jnp.mean(yf * yf, axis=-1, keepdims=True)
        yn = yf * lax.rsqrt(ms + EPS) * fg_ref[...]
        y_scr[:, 0:RET_WIDTH] = yr_ref[...]
        y_scr[:, RET_WIDTH:RET_WIDTH + FOX_WIDTH] = yn.astype(BF16)

    o_ref[...] = x_ref[...] + g1_ref[...] * _dot(y_scr[...], w_ref[...])


def out_proj(y_ret, y_fox, fox_g, w_out, x, gate1, *, tm=512, tn=1024):
    s, d = x.shape
    kdim = RET_WIDTH + FOX_WIDTH
    return pl.pallas_call(
        _outp_kernel,
        grid=(s // tm, d // tn),
        in_specs=[pl.BlockSpec((tm, RET_WIDTH), lambda i, j: (i, 0)),
                  pl.BlockSpec((tm, FOX_WIDTH), lambda i, j: (i, 0)),
                  pl.BlockSpec((1, FOX_WIDTH), lambda i, j: (0, 0)),
                  pl.BlockSpec((kdim, tn), lambda i, j: (0, j)),
                  pl.BlockSpec((tm, tn), lambda i, j: (i, j)),
                  pl.BlockSpec((1, tn), lambda i, j: (0, j))],
        out_specs=pl.BlockSpec((tm, tn), lambda i, j: (i, j)),
        out_shape=jax.ShapeDtypeStruct((s, d), F32),
        scratch_shapes=[pltpu.VMEM((tm, kdim), BF16)],
        compiler_params=_params(("arbitrary", "arbitrary")),
        name="outp",
    )(y_ret, y_fox, fox_g, w_out, x, gate1)


def _drop_max(cur):
    mx = jnp.max(jnp.max(cur, axis=0), axis=0, keepdims=True)
    hit = cur == jnp.broadcast_to(mx, cur.shape[1:])[None]
    return mx, hit, jnp.where(hit, -jnp.inf, cur)


def _top_rows(s, t_scr):
    cur = s.reshape(s.shape[0] // 8, 8, s.shape[1])
    level = jnp.full(cur.shape, float(PEER_TOPK), F32)
    for r in range(PEER_TOPK):
        mx, hit, cur = _drop_max(cur)
        t_scr[r:r + 1, :] = mx
        level = jnp.where(hit, float(r), level)
    return level.reshape(s.shape)


def _peerq_kernel(x_ref, g_ref, sc_ref, sh_ref, wq_ref, keys_ref,
                  h2t_ref, kc_ref, e1_ref, l2_ref, e2_ref, h_scr, t1_scr, t2_scr, cand_scr):
    @pl.when(pl.program_id(1) == 0)
    def _():
        h = _norm_mod(x_ref[...], g_ref[...], sc_ref[...], sh_ref[...])
        h_scr[...] = h.astype(BF16)
        h2t_ref[...] = h.T.astype(BF16)

    half = PEER_DQ // 2
    qb = _dot(h_scr[...], wq_ref[...]).astype(BF16)
    s1 = _dot_nt(keys_ref[0, 0], qb[:, 0:half])
    s2 = _dot_nt(keys_ref[0, 1], qb[:, half:PEER_DQ])
    tm = s1.shape[1]
    _top_rows(s1, t1_scr)
    l2_ref[0] = _top_rows(s2, t2_scr).astype(BF16)
    t1_lo = t1_scr[0:8, :]
    for b in range(8):
        cand_scr[b * 8:(b + 1) * 8, :] = t1_lo + t2_scr[b:b + 1, :]
    cand_scr[64:72, :] = t1_scr[0:1, :] + t2_scr[8:16, :]
    cand_scr[72:80, :] = t1_scr[8:16, :] + t2_scr[0:1, :]
    cand = cand_scr[...].reshape(PEER_NCAND // 8, 8, tm)
    cur = cand
    for r in range(PEER_TOPK):
        tau, _, cur = _drop_max(cur)
    m1 = t1_scr[0:1, :]
    m2 = t2_scr[0:1, :]
    top8 = jnp.broadcast_to(m1 + m2, (8, tm))[None]
    tau8 = jnp.broadcast_to(tau, (8, tm))[None]
    z = jnp.sum(jnp.sum(jnp.where(cand >= tau8, jnp.exp(cand - top8), 0.0), axis=0), axis=0, keepdims=True)
    s1g = s1.reshape(PEER_NKEYS // 8, 8, tm)
    kc = jnp.zeros(s1g.shape, F32)
    for b in range(PEER_TOPK):
        t2b = jnp.broadcast_to(t2_scr[b:b + 1, :], (8, tm))[None]
        kc = jnp.where(s1g + t2b >= tau8, float(b + 1), kc)
    kc_ref[0] = kc.reshape(PEER_NKEYS, tm)
    e1_ref[0] = jnp.exp(s1 - (m1 + jnp.log(z)))
    e2_ref[0] = jnp.exp(s2 - m2).astype(BF16)


def peer_query(x1, g, scale, shift, wq, keys, *, tm=512):
    s, d = x1.shape
    vec = pl.BlockSpec((1, d), lambda i, h: (0, 0))
    stat = pl.BlockSpec((1, PEER_NKEYS, tm), lambda i, h: (h, 0, i))
    table = lambda dt: jax.ShapeDtypeStruct((PEER_HEADS, PEER_NKEYS, s), dt)
    return pl.pallas_call(
        _peerq_kernel,
        grid=(s // tm, PEER_HEADS),
        in_specs=[pl.BlockSpec((tm, d), lambda i, h: (i, 0)), vec, vec, vec,
                  pl.BlockSpec((d, PEER_DQ), lambda i, h: (0, h)),
                  pl.BlockSpec((1, 2, PEER_NKEYS, PEER_DQ // 2), lambda i, h: (h, 0, 0, 0))],
        out_specs=[pl.BlockSpec((d, tm), lambda i, h: (0, i)), stat, stat, stat, stat],
        out_shape=[jax.ShapeDtypeStruct((d, s), BF16), table(F32), table(F32), table(BF16), table(BF16)],
        scratch_shapes=[pltpu.VMEM((tm, d), BF16),
                        pltpu.VMEM((PEER_TOPK, tm), F32), pltpu.VMEM((PEER_TOPK, tm), F32),
                        pltpu.VMEM((PEER_NCAND, tm), F32)],
        compiler_params=_params(("arbitrary", "arbitrary")),
        name="peerq",
    )(x1, g, scale, shift, wq, keys)


_GELU_C = 0.7978845608028654


def _gelu_tanh(x):
    k1 = -2.0 * _GELU_C * LOG2E
    z2 = x * (k1 + (k1 * 0.044715) * (x * x))
    return x / (1.0 + jnp.exp2(z2))


def _peer_kernel(h2t_ref, u_ref, vt_ref, kc_ref, e1_ref, l2_ref, e2_ref, o_ref, a_scr, w_scr):
    n = pl.program_id(1)
    tm = h2t_ref.shape[1]
    tn = u_ref.shape[0]
    rows = 16
    nrow = tn // PEER_NKEYS

    @pl.when(n == 0)
    def _():
        o_ref[...] = jnp.zeros_like(o_ref)

    a_scr[...] = _dot(u_ref[...], h2t_ref[...])
    for c in range(nrow):
        i1 = n * nrow + c
        krow = [jnp.broadcast_to(kc_ref[h, pl.ds(i1, 1), :], (rows, tm)).astype(BF16) for h in range(PEER_HEADS)]
        e1row = [jnp.broadcast_to(e1_ref[h, pl.ds(i1, 1), :], (rows, tm)).astype(BF16) for h in range(PEER_HEADS)]
        for r0 in range(0, PEER_NKEYS, rows):
            gate = None
            for h in range(PEER_HEADS):
                gv = jnp.where(l2_ref[h, r0:r0 + rows, :] < krow[h], e2_ref[h, r0:r0 + rows, :] * e1row[h],
                               jnp.zeros((), BF16))
                gate = gv if gate is None else gate + gv
            rr = c * PEER_NKEYS + r0
            w_scr[rr:rr + rows, :] = _gelu_tanh(a_scr[rr:rr + rows, :].astype(BF16)) * gate
    o_ref[...] += _dot(vt_ref[...], w_scr[...])


def peer_experts(h2t, u, vt, kc, e1, l2, e2, *, tm=512, tn=512):
    d, s = h2t.shape
    ne = u.shape[0]
    once = pl.Buffered(1)
    table = pl.BlockSpec((PEER_HEADS, PEER_NKEYS, tm), lambda i, n: (0, 0, i), pipeline_mode=once)
    return pl.pallas_call(
        _peer_kernel,
        grid=(s // tm, ne // tn),
        in_specs=[pl.BlockSpec((d, tm), lambda i, n: (0, i), pipeline_mode=once),
                  pl.BlockSpec((tn, d), lambda i, n: (n, 0)),
                  pl.BlockSpec((d, tn), lambda i, n: (0, n)),
                  table, table, table, table],
        out_specs=pl.BlockSpec((d, tm), lambda i, n: (0, i)),
        out_shape=jax.ShapeDtypeStruct((d, s), F32),
        scratch_shapes=[pltpu.VMEM((tn, tm), F32), pltpu.VMEM((tn, tm), BF16)],
        compiler_params=_params(("arbitrary", "arbitrary")),
        name="peer",
    )(h2t, u, vt, kc, e1, l2, e2)


def _fin_kernel(x_ref, pt_ref, g2_ref, fg_ref, o_ref):
    x2 = x_ref[...] + g2_ref[...] * pt_ref[...].T
    ms = jnp.mean(x2 * x2, axis=-1, keepdims=True)
    o_ref[...] = x2 * lax.rsqrt(ms + EPS) * fg_ref[...]


def final_norm(x1, peer_t, gate2, final_g, *, tm=256):
    s, d = x1.shape
    vec = pl.BlockSpec((1, d), lambda i: (0, 0))
    return pl.pallas_call(
        _fin_kernel,
        grid=(s // tm,),
        in_specs=[pl.BlockSpec((tm, d), lambda i: (i, 0)),
                  pl.BlockSpec((d, tm), lambda i: (0, i)), vec, vec],
        out_specs=pl.BlockSpec((tm, d), lambda i: (i, 0)),
        out_shape=jax.ShapeDtypeStruct((s, d), F32),
        compiler_params=_params(("arbitrary",)),
        name="fin",
    )(x1, peer_t, gate2, final_g)


def _layer(x, mod, positions, norm1_g, w_in, b_forget, ret_gn_g, fox_norm_g, w_out, norm2_g,
           w_peer_q, peer_sub_keys, peer_u, peer_v, *, bq=2048, bk=512):
    s, d = x.shape
    shift1, scale1, gate1, shift2, scale2, gate2 = [mod[:, k * d:(k + 1) * d] for k in range(6)]
    row = lambda v: v.reshape(1, -1)

    w_t = w_in.T
    w_main = cast_bf16(w_t, rows=IN_MAIN, transpose=True, br=1024, bc=512, name="cast_in")
    n_ff = w_in.shape[1] - IN_MAIN
    w_ff = tail_weight(w_t, IN_MAIN, n_ff)
    pf32, kb, vt3, ff = in_proj(x, row(norm1_g), scale1, shift1, w_main, w_ff, tm=bk)

    b128 = jnp.pad(b_forget, (0, LANES - n_ff)).reshape(1, LANES)
    caug = cum_gate(ff, b128)
    y_ret = retention(pf32, positions.reshape(s, 1), row(ret_gn_g))
    y_fox = fox_attention(pf32, kb, caug, vt3, bq=bq)
    x1 = out_proj(y_ret, y_fox, row(fox_norm_g), cast_bf16(w_out, name="cast_out"), x, gate1)

    h2t, kc, e1, l2, e2 = peer_query(x1, row(norm2_g), scale2, shift2,
                                     cast_bf16(w_peer_q, name="cast_q"), peer_sub_keys.astype(BF16))
    peer_t = peer_experts(h2t, cast_bf16(peer_u, name="cast_u"),
                          cast_bf16(peer_v, transpose=True, br=512, bc=1024, name="cast_v"), kc, e1, l2, e2)
    return x1, peer_t, gate2


def kernel(x, c, positions, w_ada, b_ada, norm1_g, w_in, b_forget, ret_gn_g, fox_norm_g, w_out,
           norm2_g, w_peer_q, peer_sub_keys, peer_u, peer_v, final_g):
    b, s, d = x.shape
    assert b == 1 and w_ada.shape[0] == 1, "one sequence, one layer"
    c8 = jnp.broadcast_to(c, (8, d))
    mod = ada_mod(c8, w_ada[0], b_ada[0].reshape(1, -1))[0:1]
    x1, peer_t, gate2 = _layer(x[0], mod, positions[0], norm1_g[0], w_in[0], b_forget[0], ret_gn_g[0],
                               fox_norm_g[0], w_out[0], norm2_g[0], w_peer_q[0], peer_sub_keys[0],
                               peer_u[0], peer_v[0])
    return final_norm(x1, peer_t, gate2, final_g.reshape(1, -1))[None]
```

```python
import functools

import numpy as np
import jax
import jax.numpy as jnp
from jax import lax
from jax.experimental import pallas as pl
from jax.experimental.pallas import tpu as pltpu

F32 = jnp.float32
BF16 = jnp.bfloat16

LANES = 128
RET_HEADS = 8
RET_DK = 128
RET_DV = 256
RET_QK = RET_HEADS * RET_DK
RET_WIDTH = RET_HEADS * RET_DV
RET_CHUNK = 128
FOX_HEADS = 16
FOX_DH = 128
FOX_WIDTH = FOX_HEADS * FOX_DH
IN_MAIN = 2 * RET_QK + 2 * RET_WIDTH + 3 * FOX_WIDTH
N_F32_COLS = 2 * RET_QK + 2 * RET_WIDTH + FOX_WIDTH
ROPE_BASE = 10000.0
PEER_HEADS = 8
PEER_NKEYS = 128
PEER_NEXPERTS = PEER_NKEYS * PEER_NKEYS
PEER_DQ = 256
PEER_TOPK = 16
PEER_NCAND = 80
EPS = 1e-6
NEG_BIG = -1e30
LOG2E = 1.4426950408889634
VMEM_LIMIT = 56 * 1024 * 1024

_NT = (((1,), (1,)), ((), ()))
_TN = (((0,), (0,)), ((), ()))


def _params(sem):
    return pltpu.CompilerParams(dimension_semantics=sem, vmem_limit_bytes=VMEM_LIMIT)


def _dot(a, b):
    return jnp.dot(a, b, preferred_element_type=F32)


def _dot_nt(a, b):
    return lax.dot_general(a, b, _NT, preferred_element_type=F32)


def _ada_kernel(c_ref, w_ref, b_ref, o_ref):
    c = c_ref[...]
    ca = (c / (1.0 + jnp.exp(-c))).astype(BF16)
    o_ref[...] = _dot(ca, w_ref[...].astype(BF16)) + b_ref[...]


def ada_mod(c8, w, b, *, bn=512):
    d, n = w.shape
    return pl.pallas_call(
        _ada_kernel,
        grid=(n // bn,),
        in_specs=[pl.BlockSpec((8, d), lambda j: (0, 0)),
                  pl.BlockSpec((d, bn), lambda j: (0, j)),
                  pl.BlockSpec((1, bn), lambda j: (0, j))],
        out_specs=pl.BlockSpec((8, bn), lambda j: (0, j)),
        out_shape=jax.ShapeDtypeStruct((8, n), F32),
        compiler_params=_params(("arbitrary",)),
        name="ada",
    )(c8, w, b)


def _cast_kernel(x_ref, o_ref):
    o_ref[...] = x_ref[...].astype(BF16)


def _cast_t_kernel(x_ref, o_ref):
    o_ref[...] = x_ref[...].T.astype(BF16)


def cast_bf16(x, *, rows=None, cols=None, br=512, bc=2048, transpose=False, name="cast"):
    r, c = x.shape
    r = r if rows is None else rows
    c = c if cols is None else cols
    br, bc = min(br, r), min(bc, c)
    if transpose:
        return pl.pallas_call(
            _cast_t_kernel, grid=(r // br, c // bc),
            in_specs=[pl.BlockSpec((br, bc), lambda i, j: (i, j))],
            out_specs=pl.BlockSpec((bc, br), lambda i, j: (j, i)),
            out_shape=jax.ShapeDtypeStruct((c, r), BF16),
            compiler_params=_params(("arbitrary", "arbitrary")), name=name)(x)
    return pl.pallas_call(
        _cast_kernel, grid=(r // br, c // bc),
        in_specs=[pl.BlockSpec((br, bc), lambda i, j: (i, j))],
        out_specs=pl.BlockSpec((br, bc), lambda i, j: (i, j)),
        out_shape=jax.ShapeDtypeStruct((r, c), BF16),
        compiler_params=_params(("arbitrary", "arbitrary")), name=name)(x)


def _tail_kernel(x_ref, o_ref):
    o_ref[...] = jnp.zeros_like(o_ref)
    o_ref[:, 0:x_ref.shape[0]] = x_ref[...].T.astype(BF16)


def tail_weight(w_t, row0, nrows):
    d = w_t.shape[1]
    assert row0 % nrows == 0 and nrows % 8 == 0
    return pl.pallas_call(
        _tail_kernel, grid=(1,),
        in_specs=[pl.BlockSpec((nrows, d), lambda i: (row0 // nrows, 0))],
        out_specs=pl.BlockSpec((d, LANES), lambda i: (0, 0)),
        out_shape=jax.ShapeDtypeStruct((d, LANES), BF16),
        compiler_params=_params(("arbitrary",)), name="cast_tail")(w_t)


def _norm_mod(x, g, scale, shift):
    ms = jnp.mean(x * x, axis=-1, keepdims=True)
    y = x * lax.rsqrt(ms + EPS) * g
    return y * (1.0 + scale) + shift


def _inpj_kernel(nf, nk, x_ref, g_ref, sc_ref, sh_ref, w_ref, wf_ref, o_ref, ok_ref, ovt_ref, of_ref, h_scr):
    j = pl.program_id(1)

    @pl.when(j == 0)
    def _():
        hb = _norm_mod(x_ref[...], g_ref[...], sc_ref[...], sh_ref[...]).astype(BF16)
        h_scr[...] = hb
        of_ref[...] = _dot(hb, wf_ref[...])

    tn = w_ref.shape[1]
    piece = tn // 2

    def pieces(store):
        for c0 in range(0, tn, piece):
            store(c0, _dot(h_scr[...], w_ref[:, c0:c0 + piece]))

    @pl.when(j < nf)
    def _():
        def store(c0, r):
            o_ref[:, c0:c0 + piece] = r
        pieces(store)

    @pl.when(jnp.logical_and(j >= nf, j < nf + nk))
    def _():
        def store(c0, r):
            ok_ref[:, c0:c0 + piece] = r.astype(BF16)
        pieces(store)

    @pl.when(j >= nf + nk)
    def _():
        def store(c0, r):
            ovt_ref[0, c0:c0 + piece, :] = r.T.astype(BF16)
        pieces(store)


def in_proj(x, g, scale, shift, w_main, w_ff, *, tm=512, tn=1024):
    s, d = x.shape
    n = w_main.shape[1]
    nf, nk = N_F32_COLS // tn, FOX_WIDTH // tn
    vec = pl.BlockSpec((1, d), lambda i, j: (0, 0))
    return pl.pallas_call(
        functools.partial(_inpj_kernel, nf, nk),
        grid=(s // tm, n // tn),
        in_specs=[pl.BlockSpec((tm, d), lambda i, j: (i, 0)), vec, vec, vec,
                  pl.BlockSpec((d, tn), lambda i, j: (0, j)),
                  pl.BlockSpec((d, LANES), lambda i, j: (0, 0))],
        out_specs=[pl.BlockSpec((tm, tn), lambda i, j: (i, jnp.minimum(j, nf - 1))),
                   pl.BlockSpec((tm, tn), lambda i, j: (i, jnp.clip(j - nf, 0, nk - 1))),
                   pl.BlockSpec((1, tn, tm), lambda i, j: (i, jnp.clip(j - nf - nk, 0, nk - 1), 0)),
                   pl.BlockSpec((tm, LANES), lambda i, j: (i, 0))],
        out_shape=[jax.ShapeDtypeStruct((s, N_F32_COLS), F32),
                   jax.ShapeDtypeStruct((s, FOX_WIDTH), BF16),
                   jax.ShapeDtypeStruct((s // tm, FOX_WIDTH, tm), BF16),
                   jax.ShapeDtypeStruct((s, LANES), F32)],
        scratch_shapes=[pltpu.VMEM((tm, d), BF16)],
        compiler_params=_params(("arbitrary", "arbitrary")),
        name="inpj",
    )(x, g, scale, shift, w_main, w_ff)


def _split3(v):
    hi = v.astype(BF16)
    r1 = v - hi.astype(F32)
    mid = r1.astype(BF16)
    lo = (r1 - mid.astype(F32)).astype(BF16)
    return hi, mid, lo


def _cum_place():
    m = np.zeros((3 * LANES, FOX_HEADS * LANES), np.float32)
    for p in range(3):
        for h in range(FOX_HEADS):
            m[p * LANES + h, h * LANES + p] = 1.0
    return m


def _cum_kernel(ff_ref, b_ref, place_ref, o_ref, carry):
    @pl.when(pl.program_id(0) == 0)
    def _():
        carry[...] = jnp.zeros_like(carry)

    z = ff_ref[...] + b_ref[...]
    logf = jnp.minimum(z, 0.0) - jnp.log(1.0 + jnp.exp(-jnp.abs(z)))
    tc = z.shape[0]
    row = lax.broadcasted_iota(jnp.int32, (tc, tc), 0)
    col = lax.broadcasted_iota(jnp.int32, (tc, tc), 1)
    tri = jnp.where(row >= col, 1.0, 0.0).astype(BF16)
    hi, mid, lo = _split3(logf)
    cum = _dot(tri, hi) + _dot(tri, mid) + _dot(tri, lo) + carry[...]
    carry[...] = cum[tc - 1:tc, :]
    pieces = jnp.concatenate(_split3(cum * LOG2E), axis=1)
    o_ref[...] = _dot(pieces, place_ref[...]).astype(BF16)


def cum_gate(ff, b128, *, tc=256):
    s = ff.shape[0]
    place = jnp.asarray(_cum_place(), BF16)
    return pl.pallas_call(
        _cum_kernel,
        grid=(s // tc,),
        in_specs=[pl.BlockSpec((tc, LANES), lambda i: (i, 0)),
                  pl.BlockSpec((1, LANES), lambda i: (0, 0)),
                  pl.BlockSpec(place.shape, lambda i: (0, 0))],
        out_specs=pl.BlockSpec((tc, FOX_HEADS * LANES), lambda i: (i, 0)),
        out_shape=jax.ShapeDtypeStruct((s, FOX_HEADS * LANES), BF16),
        scratch_shapes=[pltpu.VMEM((1, LANES), F32)],
        compiler_params=_params(("arbitrary",)),
        name="cum",
    )(ff, b128, place)


def _ret_consts():
    h = np.arange(RET_HEADS, dtype=np.float32)
    log_g = np.log1p(-np.exp2(-5.0 - h)).astype(np.float32)
    i = np.arange(RET_CHUNK, dtype=np.float32)
    diff = i[:, None] - i[None, :]
    dmat = np.where(diff >= 0, np.exp(np.maximum(diff, 0.0)[None] * log_g[:, None, None]), 0.0)
    kdec = np.exp((RET_CHUNK - 1.0 - i)[None, :] * log_g[:, None])
    qdec = np.exp((i + 1.0)[None, :] * log_g[:, None])
    cdec = np.exp(RET_CHUNK * log_g)
    kdec = np.broadcast_to(kdec[:, :, None], (RET_HEADS, RET_CHUNK, RET_DK))
    qdec = np.broadcast_to(qdec[:, :, None], (RET_HEADS, RET_CHUNK, RET_DK))
    half = RET_DK // 2
    inv = (ROPE_BASE ** (-np.arange(half, dtype=np.float32) / half)).astype(np.float32)
    inv_full = np.concatenate([inv, inv])[None, :]
    sign = np.concatenate([-np.ones(half, np.float32), np.ones(half, np.float32)])[None, :]
    return (dmat.astype(np.float32), np.ascontiguousarray(kdec, np.float32),
            np.ascontiguousarray(qdec, np.float32), [float(v) for v in cdec],
            inv_full.astype(np.float32), sign)


def _ret_kernel(cdec, rq_ref, rk_ref, rv_ref, rg_ref, pos_ref, inv_ref, sign_ref,
                dmat_ref, kdec_ref, qdec_ref, gn_ref, o_ref, state):
    @pl.when(pl.program_id(0) == 0)
    def _():
        state[...] = jnp.zeros_like(state)

    ang = pos_ref[...].astype(F32) * inv_ref[...]
    cosf = jnp.cos(ang)
    sinf = jnp.sin(ang) * sign_ref[...]
    kscale = RET_DK ** -0.5
    for h in range(RET_HEADS):
        q = rq_ref[:, h * RET_DK:(h + 1) * RET_DK]
        k = rk_ref[:, h * RET_DK:(h + 1) * RET_DK]
        qr = q * cosf + pltpu.roll(q, RET_DK // 2, 1) * sinf
        kr = (k * cosf + pltpu.roll(k, RET_DK // 2, 1) * sinf) * kscale
        vb = rv_ref[:, h * RET_DV:(h + 1) * RET_DV].astype(BF16)
        scores = _dot_nt(qr.astype(BF16), kr.astype(BF16)) * dmat_ref[h]
        intra = _dot(scores.astype(BF16), vb)
        st = state[h]
        cross = _dot((qr * qdec_ref[h]).astype(BF16), st.astype(BF16))
        kd = (kr * kdec_ref[h]).astype(BF16)
        kv = lax.dot_general(kd, vb, _TN, preferred_element_type=F32)
        state[h] = st * cdec[h] + kv
        y = intra + cross
        mu = jnp.mean(y, axis=-1, keepdims=True)
        dlt = y - mu
        var = jnp.mean(dlt * dlt, axis=-1, keepdims=True)
        g = rg_ref[:, h * RET_DV:(h + 1) * RET_DV]
        yn = dlt * lax.rsqrt(var + EPS) * gn_ref[:, h * RET_DV:(h + 1) * RET_DV] * (g / (1.0 + jnp.exp(-g)))
        o_ref[:, h * RET_DV:(h + 1) * RET_DV] = yn.astype(BF16)


def retention(proj, pos_col, gn_g):
    s = proj.shape[0]
    c = RET_CHUNK
    dmat, kdec, qdec, cdec, inv_full, sign = _ret_consts()
    full3 = lambda n: (0, 0, 0)
    return pl.pallas_call(
        functools.partial(_ret_kernel, cdec),
        grid=(s // c,),
        in_specs=[pl.BlockSpec((c, RET_QK), lambda n: (n, 0)),
                  pl.BlockSpec((c, RET_QK), lambda n: (n, 1)),
                  pl.BlockSpec((c, RET_WIDTH), lambda n: (n, 1)),
                  pl.BlockSpec((c, RET_WIDTH), lambda n: (n, 2)),
                  pl.BlockSpec((c, 1), lambda n: (n, 0)),
                  pl.BlockSpec((1, RET_DK), lambda n: (0, 0)),
                  pl.BlockSpec((1, RET_DK), lambda n: (0, 0)),
                  pl.BlockSpec((RET_HEADS, c, c), full3),
                  pl.BlockSpec((RET_HEADS, c, RET_DK), full3),
                  pl.BlockSpec((RET_HEADS, c, RET_DK), full3),
                  pl.BlockSpec((1, RET_WIDTH), lambda n: (0, 0))],
        out_specs=pl.BlockSpec((c, RET_WIDTH), lambda n: (n, 0)),
        out_shape=jax.ShapeDtypeStruct((s, RET_WIDTH), BF16),
        scratch_shapes=[pltpu.VMEM((RET_HEADS, RET_DK, RET_DV), F32)],
        compiler_params=_params(("arbitrary",)),
        name="ret",
    )(proj, proj, proj, proj, pos_col, jnp.asarray(inv_full), jnp.asarray(sign),
      jnp.asarray(dmat), jnp.asarray(kdec), jnp.asarray(qdec), gn_g)


def _fox_kernel(q_ref, k_ref, c_ref, vt_ref, o_ref, m_scr, l_scr, acc_scr, *bufs):
    i = pl.program_id(1)
    bq = q_ref.shape[0]
    bk = bufs[0].shape[0]
    nd = len(bufs)
    assert nd * bk == bq
    sub = 8
    lane = lax.broadcasted_iota(jnp.int32, (bq, FOX_DH), 1)
    qa = jnp.where(lane < 3, -1.0, 0.0).astype(BF16)
    q2 = jnp.concatenate([(q_ref[...] * (FOX_DH ** -0.5 * LOG2E)).astype(BF16), qa], axis=1)
    m_scr[...] = jnp.full_like(m_scr, NEG_BIG)
    l_scr[...] = jnp.zeros_like(l_scr)
    acc_scr[...] = jnp.zeros_like(acc_scr)
    ones_rows = jnp.where(lax.broadcasted_iota(jnp.int32, (2 * sub, bk), 0) == 0, 1.0, 0.0).astype(BF16)

    def logits(j, s_scr, c0=0):
        off = pl.multiple_of(j * bk, bk)
        k2 = jnp.concatenate([k_ref[pl.ds(off, bk), :], c_ref[pl.ds(off, bk), :]], axis=1)
        s_scr[:, c0:bq] = _dot_nt(k2, q2[c0:bq, :])

    def update(j, s_scr, c0=0, diagonal=False):
        w = bq - c0
        st = s_scr[:, c0:bq].reshape(bk // sub, sub, w)
        if diagonal:
            kid = lax.broadcasted_iota(jnp.int32, (bk, bk), 0).reshape(bk // sub, sub, bk)
            qid = lax.broadcasted_iota(jnp.int32, (bk, bk), 1).reshape(bk // sub, sub, bk)
            first = jnp.where(kid <= qid, st[:, :, 0:bk], NEG_BIG)
            st = first if w == bk else jnp.concatenate([first, st[:, :, bk:w]], axis=2)
        m_old = m_scr[:, c0:bq]
        m_new = jnp.maximum(m_old, jnp.max(jnp.max(st, axis=0), axis=0, keepdims=True))
        alpha = jnp.exp2(m_old - m_new)
        p = jnp.exp2(st - jnp.broadcast_to(m_new, (sub, w))[None])
        pv = _dot(jnp.concatenate([vt_ref[j], ones_rows], axis=0), p.reshape(bk, w).astype(BF16))
        l_scr[:, c0:bq] = alpha * l_scr[:, c0:bq] + pv[FOX_DH:FOX_DH + 1, :]
        a8 = jnp.broadcast_to(alpha, (sub, w))[None]
        acc_scr[:, c0:bq] = (a8 * acc_scr[:, c0:bq].reshape(FOX_DH // sub, sub, w)).reshape(FOX_DH, w) + pv[0:FOX_DH, :]
        m_scr[:, c0:bq] = m_new

    ahead = 2
    for k in range(ahead):
        logits(k, bufs[k])

    def body(jj, carry):
        for k in range(nd):
            logits(nd * jj + k + ahead, bufs[(k + ahead) % nd])
            update(nd * jj + k, bufs[k])
        return carry

    lax.fori_loop(0, i, body, 0)
    for k in range(nd):
        if k + ahead < nd:
            logits(nd * i + k + ahead, bufs[k + ahead], (k + ahead) * bk)
        update(nd * i + k, bufs[k], k * bk, diagonal=True)
    o_ref[...] = (acc_scr[...] / l_scr[...]).T


def fox_attention(pf32, kb, caug, vt3, *, bq=2048):
    s = pf32.shape[0]
    bk = vt3.shape[2]
    assert vt3.shape == (s // bk, FOX_WIDTH, bk) and bq % (2 * bk) == 0
    qcol0 = (N_F32_COLS - FOX_WIDTH) // FOX_DH
    return pl.pallas_call(
        _fox_kernel,
        grid=(FOX_HEADS, s // bq),
        in_specs=[pl.BlockSpec((bq, FOX_DH), lambda h, i: (i, qcol0 + h)),
                  pl.BlockSpec((s, FOX_DH), lambda h, i: (0, h)),
                  pl.BlockSpec((s, LANES), lambda h, i: (0, h)),
                  pl.BlockSpec((s // bk, FOX_DH, bk), lambda h, i: (0, h, 0))],
        out_specs=pl.BlockSpec((bq, FOX_DH), lambda h, i: (i, h)),
        out_shape=jax.ShapeDtypeStruct((s, FOX_WIDTH), F32),
        scratch_shapes=[pltpu.VMEM((1, bq), F32), pltpu.VMEM((1, bq), F32), pltpu.VMEM((FOX_DH, bq), F32)]
        + [pltpu.VMEM((bk, bq), F32)] * (bq // bk),
        compiler_params=_params(("arbitrary", "arbitrary")),
        name="fox",
    )(pf32, kb, caug, vt3)


def _outp_kernel(yr_ref, yf_ref, fg_ref, w_ref, x_ref, g1_ref, o_ref, y_scr):
    @pl.when(pl.program_id(1) == 0)
    def _():
        yf = yf_ref[...]
        ms = jnp.mean(yf * yf, axis=-1, keepdims=True)
        yn = yf * lax.rsqrt(ms + EPS) * fg_ref[...]
        y_scr[:, 0:RET_WIDTH] = yr_ref[...]
        y_scr[:, RET_WIDTH:RET_WIDTH + FOX_WIDTH] = yn.astype(BF16)

    tn = w_ref.shape[1]
    piece = tn // 2
    for c0 in range(0, tn, piece):
        cs = slice(c0, c0 + piece)
        o_ref[:, cs] = x_ref[:, cs] + g1_ref[:, cs] * _dot(y_scr[...], w_ref[:, cs])


def out_proj(y_ret, y_fox, fox_g, w_out, x, gate1, *, tm=512, tn=1024):
    s, d = x.shape
    kdim = RET_WIDTH + FOX_WIDTH
    return pl.pallas_call(
        _outp_kernel,
        grid=(s // tm, d // tn),
        in_specs=[pl.BlockSpec((tm, RET_WIDTH), lambda i, j: (i, 0)),
                  pl.BlockSpec((tm, FOX_WIDTH), lambda i, j: (i, 0)),
                  pl.BlockSpec((1, FOX_WIDTH), lambda i, j: (0, 0)),
                  pl.BlockSpec((kdim, tn), lambda i, j: (0, j)),
                  pl.BlockSpec((tm, tn), lambda i, j: (i, j)),
                  pl.BlockSpec((1, tn), lambda i, j: (0, j))],
        out_specs=pl.BlockSpec((tm, tn), lambda i, j: (i, j)),
        out_shape=jax.ShapeDtypeStruct((s, d), F32),
        scratch_shapes=[pltpu.VMEM((tm, kdim), BF16)],
        compiler_params=_params(("arbitrary", "arbitrary")),
        name="outp",
    )(y_ret, y_fox, fox_g, w_out, x, gate1)


def _drop_max(cur):
    mx = jnp.max(jnp.max(cur, axis=0), axis=0, keepdims=True)
    hit = cur == jnp.broadcast_to(mx, cur.shape[1:])[None]
    return mx, hit, jnp.where(hit, -jnp.inf, cur)


def _top_rows(s, t_scr):
    cur = s.reshape(s.shape[0] // 8, 8, s.shape[1])
    level = jnp.full(cur.shape, float(PEER_TOPK), F32)
    for r in range(PEER_TOPK):
        mx, hit, cur = _drop_max(cur)
        t_scr[r:r + 1, :] = mx
        level = jnp.where(hit, float(r), level)
    return level.reshape(s.shape)


def _peerq_kernel(x_ref, g_ref, sc_ref, sh_ref, wq_ref, keys_ref,
                  h2t_ref, kc_ref, e1_ref, l2_ref, e2_ref, h_scr, t1_scr, t2_scr, cand_scr):
    @pl.when(pl.program_id(1) == 0)
    def _():
        h = _norm_mod(x_ref[...], g_ref[...], sc_ref[...], sh_ref[...])
        h_scr[...] = h.astype(BF16)
        h2t_ref[...] = h.T.astype(BF16)

    half = PEER_DQ // 2
    qb = _dot(h_scr[...], wq_ref[...]).astype(BF16)
    s1 = _dot_nt(keys_ref[0, 0], qb[:, 0:half])
    s2 = _dot_nt(keys_ref[0, 1], qb[:, half:PEER_DQ])
    tm = s1.shape[1]
    _top_rows(s1, t1_scr)
    l2_ref[0] = _top_rows(s2, t2_scr).astype(BF16)
    t1_lo = t1_scr[0:8, :]
    for b in range(8):
        cand_scr[b * 8:(b + 1) * 8, :] = t1_lo + t2_scr[b:b + 1, :]
    cand_scr[64:72, :] = t1_scr[0:1, :] + t2_scr[8:16, :]
    cand_scr[72:80, :] = t1_scr[8:16, :] + t2_scr[0:1, :]
    cand = cand_scr[...].reshape(PEER_NCAND // 8, 8, tm)
    cur = cand
    for r in range(PEER_TOPK):
        tau, _, cur = _drop_max(cur)
    m1 = t1_scr[0:1, :]
    m2 = t2_scr[0:1, :]
    top8 = jnp.broadcast_to(m1 + m2, (8, tm))[None]
    tau8 = jnp.broadcast_to(tau, (8, tm))[None]
    z = jnp.sum(jnp.sum(jnp.where(cand >= tau8, jnp.exp(cand - top8), 0.0), axis=0), axis=0, keepdims=True)
    s1g = s1.reshape(PEER_NKEYS // 8, 8, tm)
    kc = jnp.zeros(s1g.shape, F32)
    for b in range(PEER_TOPK):
        t2b = jnp.broadcast_to(t2_scr[b:b + 1, :], (8, tm))[None]
        kc = jnp.where(s1g + t2b >= tau8, float(b + 1), kc)
    kc_ref[0] = kc.reshape(PEER_NKEYS, tm)
    e1_ref[0] = jnp.exp(s1 - (m1 + jnp.log(z)))
    e2_ref[0] = jnp.exp(s2 - m2).astype(BF16)


def peer_query(x1, g, scale, shift, wq, keys, *, tm=512):
    s, d = x1.shape
    vec = pl.BlockSpec((1, d), lambda i, h: (0, 0))
    stat = pl.BlockSpec((1, PEER_NKEYS, tm), lambda i, h: (h, 0, i))
    table = lambda dt: jax.ShapeDtypeStruct((PEER_HEADS, PEER_NKEYS, s), dt)
    return pl.pallas_call(
        _peerq_kernel,
        grid=(s // tm, PEER_HEADS),
        in_specs=[pl.BlockSpec((tm, d), lambda i, h: (i, 0)), vec, vec, vec,
                  pl.BlockSpec((d, PEER_DQ), lambda i, h: (0, h)),
                  pl.BlockSpec((1, 2, PEER_NKEYS, PEER_DQ // 2), lambda i, h: (h, 0, 0, 0))],
        out_specs=[pl.BlockSpec((d, tm), lambda i, h: (0, i)), stat, stat, stat, stat],
        out_shape=[jax.ShapeDtypeStruct((d, s), BF16), table(F32), table(F32), table(BF16), table(BF16)],
        scratch_shapes=[pltpu.VMEM((tm, d), BF16),
                        pltpu.VMEM((PEER_TOPK, tm), F32), pltpu.VMEM((PEER_TOPK, tm), F32),
                        pltpu.VMEM((PEER_NCAND, tm), F32)],
        compiler_params=_params(("arbitrary", "arbitrary")),
        name="peerq",
    )(x1, g, scale, shift, wq, keys)


_GELU_C = 0.7978845608028654


def _gelu_tanh(x):
    k1 = -2.0 * _GELU_C * LOG2E
    z2 = x * (k1 + (k1 * 0.044715) * (x * x))
    return x / (1.0 + jnp.exp2(z2))


def _peer_kernel(h2t_ref, u_ref, vt_ref, kc_ref, e1_ref, l2_ref, e2_ref, o_ref, a_scr, w_scr):
    n = pl.program_id(1)
    tm = h2t_ref.shape[1]
    tn = u_ref.shape[0]
    rows = 16
    nrow = tn // PEER_NKEYS

    @pl.when(n == 0)
    def _():
        o_ref[...] = jnp.zeros_like(o_ref)

    a_scr[...] = _dot(u_ref[...], h2t_ref[...])
    for c in range(nrow):
        i1 = n * nrow + c
        krow = [jnp.broadcast_to(kc_ref[h, pl.ds(i1, 1), :], (rows, tm)).astype(BF16) for h in range(PEER_HEADS)]
        e1row = [jnp.broadcast_to(e1_ref[h, pl.ds(i1, 1), :], (rows, tm)).astype(BF16) for h in range(PEER_HEADS)]
        for r0 in range(0, PEER_NKEYS, rows):
            gate = None
            for h in range(PEER_HEADS):
                gv = jnp.where(l2_ref[h, r0:r0 + rows, :] < krow[h], e2_ref[h, r0:r0 + rows, :] * e1row[h],
                               jnp.zeros((), BF16))
                gate = gv if gate is None else gate + gv
            rr = c * PEER_NKEYS + r0
            w_scr[rr:rr + rows, :] = _gelu_tanh(a_scr[rr:rr + rows, :].astype(BF16)) * gate
    o_ref[...] += _dot(vt_ref[...], w_scr[...])


def peer_experts(h2t, u, vt, kc, e1, l2, e2, *, tm=512, tn=512):
    d, s = h2t.shape
    ne = u.shape[0]
    once = pl.Buffered(1)
    table = pl.BlockSpec((PEER_HEADS, PEER_NKEYS, tm), lambda i, n: (0, 0, i), pipeline_mode=once)
    return pl.pallas_call(
        _peer_kernel,
        grid=(s // tm, ne // tn),
        in_specs=[pl.BlockSpec((d, tm), lambda i, n: (0, i), pipeline_mode=once),
                  pl.BlockSpec((tn, d), lambda i, n: (n, 0)),
                  pl.BlockSpec((d, tn), lambda i, n: (0, n)),
                  table, table, table, table],
        out_specs=pl.BlockSpec((d, tm), lambda i, n: (0, i)),
        out_shape=jax.ShapeDtypeStruct((d, s), F32),
        scratch_shapes=[pltpu.VMEM((tn, tm), F32), pltpu.VMEM((tn, tm), BF16)],
        compiler_params=_params(("arbitrary", "arbitrary")),
        name="peer",
    )(h2t, u, vt, kc, e1, l2, e2)


def _fin_kernel(x_ref, pt_ref, g2_ref, fg_ref, o_ref):
    x2 = x_ref[...] + g2_ref[...] * pt_ref[...].T
    ms = jnp.mean(x2 * x2, axis=-1, keepdims=True)
    o_ref[...] = x2 * lax.rsqrt(ms + EPS) * fg_ref[...]


def final_norm(x1, peer_t, gate2, final_g, *, tm=256):
    s, d = x1.shape
    vec = pl.BlockSpec((1, d), lambda i: (0, 0))
    return pl.pallas_call(
        _fin_kernel,
        grid=(s // tm,),
        in_specs=[pl.BlockSpec((tm, d), lambda i: (i, 0)),
                  pl.BlockSpec((d, tm), lambda i: (0, i)), vec, vec],
        out_specs=pl.BlockSpec((tm, d), lambda i: (i, 0)),
        out_shape=jax.ShapeDtypeStruct((s, d), F32),
        compiler_params=_params(("arbitrary",)),
        name="fin",
    )(x1, peer_t, gate2, final_g)


def _layer(x, mod, positions, norm1_g, w_in, b_forget, ret_gn_g, fox_norm_g, w_out, norm2_g,
           w_peer_q, peer_sub_keys, peer_u, peer_v, *, bq=2048, bk=512):
    s, d = x.shape
    shift1, scale1, gate1, shift2, scale2, gate2 = [mod[:, k * d:(k + 1) * d] for k in range(6)]
    row = lambda v: v.reshape(1, -1)

    w_t = w_in.T
    w_main = cast_bf16(w_t, rows=IN_MAIN, transpose=True, br=1024, bc=512, name="cast_in")
    n_ff = w_in.shape[1] - IN_MAIN
    w_ff = tail_weight(w_t, IN_MAIN, n_ff)
    pf32, kb, vt3, ff = in_proj(x, row(norm1_g), scale1, shift1, w_main, w_ff, tm=bk)

    b128 = jnp.pad(b_forget, (0, LANES - n_ff)).reshape(1, LANES)
    caug = cum_gate(ff, b128)
    y_ret = retention(pf32, positions.reshape(s, 1), row(ret_gn_g))
    y_fox = fox_attention(pf32, kb, caug, vt3, bq=bq)
    x1 = out_proj(y_ret, y_fox, row(fox_norm_g), cast_bf16(w_out, name="cast_out"), x, gate1)

    h2t, kc, e1, l2, e2 = peer_query(x1, row(norm2_g), scale2, shift2,
                                     cast_bf16(w_peer_q, name="cast_q"), peer_sub_keys.astype(BF16))
    peer_t = peer_experts(h2t, cast_bf16(peer_u, name="cast_u"),
                          cast_bf16(peer_v, transpose=True, br=512, bc=1024, name="cast_v"), kc, e1, l2, e2)
    return x1, peer_t, gate2


def kernel(x, c, positions, w_ada, b_ada, norm1_g, w_in, b_forget, ret_gn_g, fox_norm_g, w_out,
           norm2_g, w_peer_q, peer_sub_keys, peer_u, peer_v, final_g):
    b, s, d = x.shape
    assert b == 1 and w_ada.shape[0] == 1, "one sequence, one layer"
    c8 = jnp.broadcast_to(c, (8, d))
    mod = ada_mod(c8, w_ada[0], b_ada[0].reshape(1, -1))[0:1]
    x1, peer_t, gate2 = _layer(x[0], mod, positions[0], norm1_g[0], w_in[0], b_forget[0], ret_gn_g[0],
                               fox_norm_g[0], w_out[0], norm2_g[0], w_peer_q[0], peer_sub_keys[0],
                               peer_u[0], peer_v[0])
    return final_norm(x1, peer_t, gate2, final_g.reshape(1, -1))[None]
```

```python
import functools

import numpy as np
import jax
import jax.numpy as jnp
from jax import lax
from jax.experimental import pallas as pl
from jax.experimental.pallas import tpu as pltpu

F32 = jnp.float32
BF16 = jnp.bfloat16

LANES = 128
RET_HEADS = 8
RET_DK = 128
RET_DV = 256
RET_QK = RET_HEADS * RET_DK
RET_WIDTH = RET_HEADS * RET_DV
RET_CHUNK = 128
FOX_HEADS = 16
FOX_DH = 128
FOX_WIDTH = FOX_HEADS * FOX_DH
IN_MAIN = 2 * RET_QK + 2 * RET_WIDTH + 3 * FOX_WIDTH
N_F32_COLS = 2 * RET_QK + 2 * RET_WIDTH + FOX_WIDTH
ROPE_BASE = 10000.0
PEER_HEADS = 8
PEER_NKEYS = 128
PEER_NEXPERTS = PEER_NKEYS * PEER_NKEYS
PEER_DQ = 256
PEER_TOPK = 16
PEER_NCAND = 80
EPS = 1e-6
NEG_BIG = -1e30
LOG2E = 1.4426950408889634
VMEM_LIMIT = 56 * 1024 * 1024

_NT = (((1,), (1,)), ((), ()))
_TN = (((0,), (0,)), ((), ()))


def _params(sem):
    return pltpu.CompilerParams(dimension_semantics=sem, vmem_limit_bytes=VMEM_LIMIT)


def _dot(a, b):
    return jnp.dot(a, b, preferred_element_type=F32)


def _dot_nt(a, b):
    return lax.dot_general(a, b, _NT, preferred_element_type=F32)


def _ada_kernel(c_ref, w_ref, b_ref, o_ref):
    c = c_ref[...]
    ca = (c / (1.0 + jnp.exp(-c))).astype(BF16)
    o_ref[...] = _dot(ca, w_ref[...].astype(BF16)) + b_ref[...]


def ada_mod(c8, w, b, *, bn=512):
    d, n = w.shape
    return pl.pallas_call(
        _ada_kernel,
        grid=(n // bn,),
        in_specs=[pl.BlockSpec((8, d), lambda j: (0, 0)),
                  pl.BlockSpec((d, bn), lambda j: (0, j)),
                  pl.BlockSpec((1, bn), lambda j: (0, j))],
        out_specs=pl.BlockSpec((8, bn), lambda j: (0, j)),
        out_shape=jax.ShapeDtypeStruct((8, n), F32),
        compiler_params=_params(("arbitrary",)),
        name="ada",
    )(c8, w, b)


def _cast_kernel(x_ref, o_ref):
    o_ref[...] = x_ref[...].astype(BF16)


def _cast_t_kernel(x_ref, o_ref):
    o_ref[...] = x_ref[...].T.astype(BF16)


def cast_bf16(x, *, rows=None, cols=None, br=512, bc=2048, transpose=False, name="cast"):
    r, c = x.shape
    r = r if rows is None else rows
    c = c if cols is None else cols
    br, bc = min(br, r), min(bc, c)
    if transpose:
        return pl.pallas_call(
            _cast_t_kernel, grid=(r // br, c // bc),
            in_specs=[pl.BlockSpec((br, bc), lambda i, j: (i, j))],
            out_specs=pl.BlockSpec((bc, br), lambda i, j: (j, i)),
            out_shape=jax.ShapeDtypeStruct((c, r), BF16),
            compiler_params=_params(("arbitrary", "arbitrary")), name=name)(x)
    return pl.pallas_call(
        _cast_kernel, grid=(r // br, c // bc),
        in_specs=[pl.BlockSpec((br, bc), lambda i, j: (i, j))],
        out_specs=pl.BlockSpec((br, bc), lambda i, j: (i, j)),
        out_shape=jax.ShapeDtypeStruct((r, c), BF16),
        compiler_params=_params(("arbitrary", "arbitrary")), name=name)(x)


def _tail_kernel(x_ref, o_ref):
    o_ref[...] = jnp.zeros_like(o_ref)
    o_ref[:, 0:x_ref.shape[0]] = x_ref[...].T.astype(BF16)


def tail_weight(w_t, row0, nrows):
    d = w_t.shape[1]
    assert row0 % nrows == 0 and nrows % 8 == 0
    return pl.pallas_call(
        _tail_kernel, grid=(1,),
        in_specs=[pl.BlockSpec((nrows, d), lambda i: (row0 // nrows, 0))],
        out_specs=pl.BlockSpec((d, LANES), lambda i: (0, 0)),
        out_shape=jax.ShapeDtypeStruct((d, LANES), BF16),
        compiler_params=_params(("arbitrary",)), name="cast_tail")(w_t)


def _norm_mod(x, g, scale, shift):
    ms = jnp.mean(x * x, axis=-1, keepdims=True)
    y = x * lax.rsqrt(ms + EPS) * g
    return y * (1.0 + scale) + shift


def _inpj_kernel(nf, nk, x_ref, g_ref, sc_ref, sh_ref, w_ref, wf_ref, o_ref, ok_ref, ovt_ref, of_ref, h_scr):
    j = pl.program_id(1)

    @pl.when(j == 0)
    def _():
        hb = _norm_mod(x_ref[...], g_ref[...], sc_ref[...], sh_ref[...]).astype(BF16)
        h_scr[...] = hb
        of_ref[...] = _dot(hb, wf_ref[...])

    tn = w_ref.shape[1]
    piece = tn // 2

    def pieces(store):
        for c0 in range(0, tn, piece):
            store(c0, _dot(h_scr[...], w_ref[:, c0:c0 + piece]))

    @pl.when(j < nf)
    def _():
        def store(c0, r):
            o_ref[:, c0:c0 + piece] = r
        pieces(store)

    @pl.when(jnp.logical_and(j >= nf, j < nf + nk))
    def _():
        def store(c0, r):
            ok_ref[:, c0:c0 + piece] = r.astype(BF16)
        pieces(store)

    @pl.when(j >= nf + nk)
    def _():
        def store(c0, r):
            ovt_ref[0, c0:c0 + piece, :] = r.T.astype(BF16)
        pieces(store)


def in_proj(x, g, scale, shift, w_main, w_ff, *, tm=512, tn=1024):
    s, d = x.shape
    n = w_main.shape[1]
    nf, nk = N_F32_COLS // tn, FOX_WIDTH // tn
    vec = pl.BlockSpec((1, d), lambda i, j: (0, 0))
    return pl.pallas_call(
        functools.partial(_inpj_kernel, nf, nk),
        grid=(s // tm, n // tn),
        in_specs=[pl.BlockSpec((tm, d), lambda i, j: (i, 0)), vec, vec, vec,
                  pl.BlockSpec((d, tn), lambda i, j: (0, j)),
                  pl.BlockSpec((d, LANES), lambda i, j: (0, 0))],
        out_specs=[pl.BlockSpec((tm, tn), lambda i, j: (i, jnp.minimum(j, nf - 1))),
                   pl.BlockSpec((tm, tn), lambda i, j: (i, jnp.clip(j - nf, 0, nk - 1))),
                   pl.BlockSpec((1, tn, tm), lambda i, j: (i, jnp.clip(j - nf - nk, 0, nk - 1), 0)),
                   pl.BlockSpec((tm, LANES), lambda i, j: (i, 0))],
        out_shape=[jax.ShapeDtypeStruct((s, N_F32_COLS), F32),
                   jax.ShapeDtypeStruct((s, FOX_WIDTH), BF16),
                   jax.ShapeDtypeStruct((s // tm, FOX_WIDTH, tm), BF16),
                   jax.ShapeDtypeStruct((s, LANES), F32)],
        scratch_shapes=[pltpu.VMEM((tm, d), BF16)],
        compiler_params=_params(("arbitrary", "arbitrary")),
        name="inpj",
    )(x, g, scale, shift, w_main, w_ff)


def _split3(v):
    hi = v.astype(BF16)
    r1 = v - hi.astype(F32)
    mid = r1.astype(BF16)
    lo = (r1 - mid.astype(F32)).astype(BF16)
    return hi, mid, lo


def _cum_place():
    m = np.zeros((3 * LANES, FOX_HEADS * LANES), np.float32)
    for p in range(3):
        for h in range(FOX_HEADS):
            m[p * LANES + h, h * LANES + p] = 1.0
    return m


def _cum_kernel(ff_ref, b_ref, place_ref, o_ref, carry):
    @pl.when(pl.program_id(0) == 0)
    def _():
        carry[...] = jnp.zeros_like(carry)

    z = ff_ref[...] + b_ref[...]
    logf = jnp.minimum(z, 0.0) - jnp.log(1.0 + jnp.exp(-jnp.abs(z)))
    tc = z.shape[0]
    row = lax.broadcasted_iota(jnp.int32, (tc, tc), 0)
    col = lax.broadcasted_iota(jnp.int32, (tc, tc), 1)
    tri = jnp.where(row >= col, 1.0, 0.0).astype(BF16)
    hi, mid, lo = _split3(logf)
    cum = _dot(tri, hi) + _dot(tri, mid) + _dot(tri, lo) + carry[...]
    carry[...] = cum[tc - 1:tc, :]
    pieces = jnp.concatenate(_split3(cum * LOG2E), axis=1)
    o_ref[...] = _dot(pieces, place_ref[...]).astype(BF16)


def cum_gate(ff, b128, *, tc=256):
    s = ff.shape[0]
    place = jnp.asarray(_cum_place(), BF16)
    return pl.pallas_call(
        _cum_kernel,
        grid=(s // tc,),
        in_specs=[pl.BlockSpec((tc, LANES), lambda i: (i, 0)),
                  pl.BlockSpec((1, LANES), lambda i: (0, 0)),
                  pl.BlockSpec(place.shape, lambda i: (0, 0))],
        out_specs=pl.BlockSpec((tc, FOX_HEADS * LANES), lambda i: (i, 0)),
        out_shape=jax.ShapeDtypeStruct((s, FOX_HEADS * LANES), BF16),
        scratch_shapes=[pltpu.VMEM((1, LANES), F32)],
        compiler_params=_params(("arbitrary",)),
        name="cum",
    )(ff, b128, place)


def _ret_consts():
    h = np.arange(RET_HEADS, dtype=np.float32)
    log_g = np.log1p(-np.exp2(-5.0 - h)).astype(np.float32)
    i = np.arange(RET_CHUNK, dtype=np.float32)
    diff = i[:, None] - i[None, :]
    dmat = np.where(diff >= 0, np.exp(np.maximum(diff, 0.0)[None] * log_g[:, None, None]), 0.0)
    kdec = np.exp((RET_CHUNK - 1.0 - i)[None, :] * log_g[:, None])
    qdec = np.exp((i + 1.0)[None, :] * log_g[:, None])
    cdec = np.exp(RET_CHUNK * log_g)
    kdec = np.broadcast_to(kdec[:, :, None], (RET_HEADS, RET_CHUNK, RET_DK))
    qdec = np.broadcast_to(qdec[:, :, None], (RET_HEADS, RET_CHUNK, RET_DK))
    half = RET_DK // 2
    inv = (ROPE_BASE ** (-np.arange(half, dtype=np.float32) / half)).astype(np.float32)
    inv_full = np.concatenate([inv, inv])[None, :]
    sign = np.concatenate([-np.ones(half, np.float32), np.ones(half, np.float32)])[None, :]
    return (dmat.astype(np.float32), np.ascontiguousarray(kdec, np.float32),
            np.ascontiguousarray(qdec, np.float32), [float(v) for v in cdec],
            inv_full.astype(np.float32), sign)


def _ret_kernel(cdec, rq_ref, rk_ref, rv_ref, rg_ref, pos_ref, inv_ref, sign_ref,
                dmat_ref, kdec_ref, qdec_ref, gn_ref, o_ref, state):
    @pl.when(pl.program_id(0) == 0)
    def _():
        state[...] = jnp.zeros_like(state)

    ang = pos_ref[...].astype(F32) * inv_ref[...]
    cosf = jnp.cos(ang)
    sinf = jnp.sin(ang) * sign_ref[...]
    kscale = RET_DK ** -0.5
    for h in range(RET_HEADS):
        q = rq_ref[:, h * RET_DK:(h + 1) * RET_DK]
        k = rk_ref[:, h * RET_DK:(h + 1) * RET_DK]
        qr = q * cosf + pltpu.roll(q, RET_DK // 2, 1) * sinf
        kr = (k * cosf + pltpu.roll(k, RET_DK // 2, 1) * sinf) * kscale
        vb = rv_ref[:, h * RET_DV:(h + 1) * RET_DV].astype(BF16)
        scores = _dot_nt(qr.astype(BF16), kr.astype(BF16)) * dmat_ref[h]
        intra = _dot(scores.astype(BF16), vb)
        st = state[h]
        cross = _dot((qr * qdec_ref[h]).astype(BF16), st.astype(BF16))
        kd = (kr * kdec_ref[h]).astype(BF16)
        kv = lax.dot_general(kd, vb, _TN, preferred_element_type=F32)
        state[h] = st * cdec[h] + kv
        y = intra + cross
        mu = jnp.mean(y, axis=-1, keepdims=True)
        dlt = y - mu
        var = jnp.mean(dlt * dlt, axis=-1, keepdims=True)
        g = rg_ref[:, h * RET_DV:(h + 1) * RET_DV]
        yn = dlt * lax.rsqrt(var + EPS) * gn_ref[:, h * RET_DV:(h + 1) * RET_DV] * (g / (1.0 + jnp.exp(-g)))
        o_ref[:, h * RET_DV:(h + 1) * RET_DV] = yn.astype(BF16)


def retention(proj, pos_col, gn_g):
    s = proj.shape[0]
    c = RET_CHUNK
    dmat, kdec, qdec, cdec, inv_full, sign = _ret_consts()
    full3 = lambda n: (0, 0, 0)
    return pl.pallas_call(
        functools.partial(_ret_kernel, cdec),
        grid=(s // c,),
        in_specs=[pl.BlockSpec((c, RET_QK), lambda n: (n, 0)),
                  pl.BlockSpec((c, RET_QK), lambda n: (n, 1)),
                  pl.BlockSpec((c, RET_WIDTH), lambda n: (n, 1)),
                  pl.BlockSpec((c, RET_WIDTH), lambda n: (n, 2)),
                  pl.BlockSpec((c, 1), lambda n: (n, 0)),
                  pl.BlockSpec((1, RET_DK), lambda n: (0, 0)),
                  pl.BlockSpec((1, RET_DK), lambda n: (0, 0)),
                  pl.BlockSpec((RET_HEADS, c, c), full3),
                  pl.BlockSpec((RET_HEADS, c, RET_DK), full3),
                  pl.BlockSpec((RET_HEADS, c, RET_DK), full3),
                  pl.BlockSpec((1, RET_WIDTH), lambda n: (0, 0))],
        out_specs=pl.BlockSpec((c, RET_WIDTH), lambda n: (n, 0)),
        out_shape=jax.ShapeDtypeStruct((s, RET_WIDTH), BF16),
        scratch_shapes=[pltpu.VMEM((RET_HEADS, RET_DK, RET_DV), F32)],
        compiler_params=_params(("arbitrary",)),
        name="ret",
    )(proj, proj, proj, proj, pos_col, jnp.asarray(inv_full), jnp.asarray(sign),
      jnp.asarray(dmat), jnp.asarray(kdec), jnp.asarray(qdec), gn_g)


def _fox_kernel(q_ref, k_ref, c_ref, vt_ref, o_ref, m_scr, l_scr, acc_scr, *bufs):
    i = pl.program_id(1)
    bq = q_ref.shape[0]
    bk = bufs[0].shape[0]
    nd = len(bufs)
    assert nd * bk == bq
    sub = 8
    lane = lax.broadcasted_iota(jnp.int32, (bq, FOX_DH), 1)
    qa = jnp.where(lane < 3, -1.0, 0.0).astype(BF16)
    q2 = jnp.concatenate([(q_ref[...] * (FOX_DH ** -0.5 * LOG2E)).astype(BF16), qa], axis=1)
    m_scr[...] = jnp.full_like(m_scr, NEG_BIG)
    l_scr[...] = jnp.zeros_like(l_scr)
    acc_scr[...] = jnp.zeros_like(acc_scr)
    ones_rows = jnp.where(lax.broadcasted_iota(jnp.int32, (2 * sub, bk), 0) == 0, 1.0, 0.0).astype(BF16)

    def logits(j, s_scr, c0=0):
        off = pl.multiple_of(j * bk, bk)
        k2 = jnp.concatenate([k_ref[pl.ds(off, bk), :], c_ref[pl.ds(off, bk), :]], axis=1)
        s_scr[:, c0:bq] = _dot_nt(k2, q2[c0:bq, :])

    def update_cols(j, s_scr, c0, c1, diagonal):
        w = c1 - c0
        st = s_scr[:, c0:c1].reshape(bk // sub, sub, w)
        if diagonal:
            kid = lax.broadcasted_iota(jnp.int32, (bk, bk), 0).reshape(bk // sub, sub, bk)
            qid = lax.broadcasted_iota(jnp.int32, (bk, bk), 1).reshape(bk // sub, sub, bk)
            st = jnp.where(kid <= qid, st, NEG_BIG)
        m_old = m_scr[:, c0:c1]
        m_new = jnp.maximum(m_old, jnp.max(jnp.max(st, axis=0), axis=0, keepdims=True))
        alpha = jnp.exp2(m_old - m_new)
        p = jnp.exp2(st - jnp.broadcast_to(m_new, (sub, w))[None])
        pv = _dot(jnp.concatenate([vt_ref[j], ones_rows], axis=0), p.reshape(bk, w).astype(BF16))
        l_scr[:, c0:c1] = alpha * l_scr[:, c0:c1] + pv[FOX_DH:FOX_DH + 1, :]
        a8 = jnp.broadcast_to(alpha, (sub, w))[None]
        acc_scr[:, c0:c1] = (a8 * acc_scr[:, c0:c1].reshape(FOX_DH // sub, sub, w)).reshape(FOX_DH, w) + pv[0:FOX_DH, :]
        m_scr[:, c0:c1] = m_new

    def update(j, s_scr, c0=0, diagonal=False):
        if diagonal:
            update_cols(j, s_scr, c0, c0 + bk, True)
            c0 += bk
        if c0 < bq:
            update_cols(j, s_scr, c0, bq, False)

    ahead = 2
    for k in range(ahead):
        logits(k, bufs[k])

    def body(jj, carry):
        for k in range(nd):
            logits(nd * jj + k + ahead, bufs[(k + ahead) % nd])
            update(nd * jj + k, bufs[k])
        return carry

    lax.fori_loop(0, i, body, 0)
    for k in range(nd):
        if k + ahead < nd:
            logits(nd * i + k + ahead, bufs[k + ahead], (k + ahead) * bk)
        update(nd * i + k, bufs[k], k * bk, diagonal=True)
    o_ref[...] = (acc_scr[...] / l_scr[...]).T


def fox_attention(pf32, kb, caug, vt3, *, bq=2048):
    s = pf32.shape[0]
    bk = vt3.shape[2]
    assert vt3.shape == (s // bk, FOX_WIDTH, bk) and bq % (2 * bk) == 0
    qcol0 = (N_F32_COLS - FOX_WIDTH) // FOX_DH
    return pl.pallas_call(
        _fox_kernel,
        grid=(FOX_HEADS, s // bq),
        in_specs=[pl.BlockSpec((bq, FOX_DH), lambda h, i: (i, qcol0 + h)),
                  pl.BlockSpec((s, FOX_DH), lambda h, i: (0, h)),
                  pl.BlockSpec((s, LANES), lambda h, i: (0, h)),
                  pl.BlockSpec((s // bk, FOX_DH, bk), lambda h, i: (0, h, 0))],
        out_specs=pl.BlockSpec((bq, FOX_DH), lambda h, i: (i, h)),
        out_shape=jax.ShapeDtypeStruct((s, FOX_WIDTH), F32),
        scratch_shapes=[pltpu.VMEM((1, bq), F32), pltpu.VMEM((1, bq), F32), pltpu.VMEM((FOX_DH, bq), F32)]
        + [pltpu.VMEM((bk, bq), F32)] * (bq // bk),
        compiler_params=_params(("arbitrary", "arbitrary")),
        name="fox",
    )(pf32, kb, caug, vt3)


def _outp_kernel(yr_ref, yf_ref, fg_ref, w_ref, x_ref, g1_ref, o_ref, y_scr):
    @pl.when(pl.program_id(1) == 0)
    def _():
        yf = yf_ref[...]
        ms = jnp.mean(yf * yf, axis=-1, keepdims=True)
        yn = yf * lax.rsqrt(ms + EPS) * fg_ref[...]
        y_scr[:, 0:RET_WIDTH] = yr_ref[...]
        y_scr[:, RET_WIDTH:RET_WIDTH + FOX_WIDTH] = yn.astype(BF16)

    tn = w_ref.shape[1]
    piece = tn // 2
    for c0 in range(0, tn, piece):
        cs = slice(c0, c0 + piece)
        o_ref[:, cs] = x_ref[:, cs] + g1_ref[:, cs] * _dot(y_scr[...], w_ref[:, cs])


def out_proj(y_ret, y_fox, fox_g, w_out, x, gate1, *, tm=512, tn=1024):
    s, d = x.shape
    kdim = RET_WIDTH + FOX_WIDTH
    return pl.pallas_call(
        _outp_kernel,
        grid=(s // tm, d // tn),
        in_specs=[pl.BlockSpec((tm, RET_WIDTH), lambda i, j: (i, 0)),
                  pl.BlockSpec((tm, FOX_WIDTH), lambda i, j: (i, 0)),
                  pl.BlockSpec((1, FOX_WIDTH), lambda i, j: (0, 0)),
                  pl.BlockSpec((kdim, tn), lambda i, j: (0, j)),
                  pl.BlockSpec((tm, tn), lambda i, j: (i, j)),
                  pl.BlockSpec((1, tn), lambda i, j: (0, j))],
        out_specs=pl.BlockSpec((tm, tn), lambda i, j: (i, j)),
        out_shape=jax.ShapeDtypeStruct((s, d), F32),
        scratch_shapes=[pltpu.VMEM((tm, kdim), BF16)],
        compiler_params=_params(("arbitrary", "arbitrary")),
        name="outp",
    )(y_ret, y_fox, fox_g, w_out, x, gate1)


def _drop_max(cur):
    mx = jnp.max(jnp.max(cur, axis=0), axis=0, keepdims=True)
    hit = cur == jnp.broadcast_to(mx, cur.shape[1:])[None]
    return mx, hit, jnp.where(hit, -jnp.inf, cur)


def _top_rows(s, t_scr, ls):
    cur = s.reshape(s.shape[0] // 8, 8, s.shape[1])
    level = jnp.full(cur.shape, float(PEER_TOPK), F32)
    for r in range(PEER_TOPK):
        mx, hit, cur = _drop_max(cur)
        t_scr[r:r + 1, ls] = mx
        level = jnp.where(hit, float(r), level)
    return level.reshape(s.shape)


def _peerq_kernel(x_ref, g_ref, sc_ref, sh_ref, wq_ref, keys_ref,
                  h2t_ref, kc_ref, e1_ref, l2_ref, e2_ref, h_scr, s1_scr, s2_scr, t1_scr, t2_scr, cand_scr):
    @pl.when(pl.program_id(1) == 0)
    def _():
        h = _norm_mod(x_ref[...], g_ref[...], sc_ref[...], sh_ref[...])
        h_scr[...] = h.astype(BF16)
        h2t_ref[...] = h.T.astype(BF16)

    half = PEER_DQ // 2
    qb = _dot(h_scr[...], wq_ref[...]).astype(BF16)
    s1 = _dot_nt(keys_ref[0, 0], qb[:, 0:half])
    s2 = _dot_nt(keys_ref[0, 1], qb[:, half:PEER_DQ])
    s1_scr[...] = s1
    s2_scr[...] = s2
    for g in range(s1.shape[1] // LANES):
        ls = slice(g * LANES, (g + 1) * LANES)
        s1l = s1_scr[:, ls]
        s2l = s2_scr[:, ls]
        _top_rows(s1l, t1_scr, ls)
        l2_ref[0, :, ls] = _top_rows(s2l, t2_scr, ls).astype(BF16)
        t1_lo = t1_scr[0:8, ls]
        for b in range(8):
            cand_scr[b * 8:(b + 1) * 8, ls] = t1_lo + t2_scr[b:b + 1, ls]
        cand_scr[64:72, ls] = t1_scr[0:1, ls] + t2_scr[8:16, ls]
        cand_scr[72:80, ls] = t1_scr[8:16, ls] + t2_scr[0:1, ls]
        cand = cand_scr[:, ls].reshape(PEER_NCAND // 8, 8, LANES)
        cur = cand
        for r in range(PEER_TOPK):
            tau, _, cur = _drop_max(cur)
        m1 = t1_scr[0:1, ls]
        m2 = t2_scr[0:1, ls]
        top8 = jnp.broadcast_to(m1 + m2, (8, LANES))[None]
        tau8 = jnp.broadcast_to(tau, (8, LANES))[None]
        z = jnp.sum(jnp.sum(jnp.where(cand >= tau8, jnp.exp(cand - top8), 0.0), axis=0), axis=0, keepdims=True)
        s1g = s1l.reshape(PEER_NKEYS // 8, 8, LANES)
        kc = jnp.zeros(s1g.shape, F32)
        for b in range(PEER_TOPK):
            t2b = jnp.broadcast_to(t2_scr[b:b + 1, ls], (8, LANES))[None]
            kc = jnp.where(s1g + t2b >= tau8, float(b + 1), kc)
        kc_ref[0, :, ls] = kc.reshape(PEER_NKEYS, LANES)
        e1_ref[0, :, ls] = jnp.exp(s1l - (m1 + jnp.log(z)))
        e2_ref[0, :, ls] = jnp.exp(s2l - m2).astype(BF16)


def peer_query(x1, g, scale, shift, wq, keys, *, tm=512):
    s, d = x1.shape
    vec = pl.BlockSpec((1, d), lambda i, h: (0, 0))
    stat = pl.BlockSpec((1, PEER_NKEYS, tm), lambda i, h: (h, 0, i))
    table = lambda dt: jax.ShapeDtypeStruct((PEER_HEADS, PEER_NKEYS, s), dt)
    return pl.pallas_call(
        _peerq_kernel,
        grid=(s // tm, PEER_HEADS),
        in_specs=[pl.BlockSpec((tm, d), lambda i, h: (i, 0)), vec, vec, vec,
                  pl.BlockSpec((d, PEER_DQ), lambda i, h: (0, h)),
                  pl.BlockSpec((1, 2, PEER_NKEYS, PEER_DQ // 2), lambda i, h: (h, 0, 0, 0))],
        out_specs=[pl.BlockSpec((d, tm), lambda i, h: (0, i)), stat, stat, stat, stat],
        out_shape=[jax.ShapeDtypeStruct((d, s), BF16), table(F32), table(F32), table(BF16), table(BF16)],
        scratch_shapes=[pltpu.VMEM((tm, d), BF16),
                        pltpu.VMEM((PEER_NKEYS, tm), F32), pltpu.VMEM((PEER_NKEYS, tm), F32),
                        pltpu.VMEM((PEER_TOPK, tm), F32), pltpu.VMEM((PEER_TOPK, tm), F32),
                        pltpu.VMEM((PEER_NCAND, tm), F32)],
        compiler_params=_params(("arbitrary", "arbitrary")),
        name="peerq",
    )(x1, g, scale, shift, wq, keys)


_GELU_C = 0.7978845608028654


def _gelu_tanh(x):
    k1 = -2.0 * _GELU_C * LOG2E
    z2 = x * (k1 + (k1 * 0.044715) * (x * x))
    return x / (1.0 + jnp.exp2(z2))


def _peer_kernel(h2t_ref, u_ref, vt_ref, kc_ref, e1_ref, l2_ref, e2_ref, o_ref, a_scr, w_scr):
    n = pl.program_id(1)
    tm = h2t_ref.shape[1]
    tn = u_ref.shape[0]
    rows = 16
    nrow = tn // PEER_NKEYS

    @pl.when(n == 0)
    def _():
        o_ref[...] = jnp.zeros_like(o_ref)

    a_scr[...] = _dot(u_ref[...], h2t_ref[...])
    for c in range(nrow):
        i1 = n * nrow + c
        krow = [jnp.broadcast_to(kc_ref[h, pl.ds(i1, 1), :], (rows, tm)).astype(BF16) for h in range(PEER_HEADS)]
        e1row = [jnp.broadcast_to(e1_ref[h, pl.ds(i1, 1), :], (rows, tm)).astype(BF16) for h in range(PEER_HEADS)]
        for r0 in range(0, PEER_NKEYS, rows):
            gate = None
            for h in range(PEER_HEADS):
                gv = jnp.where(l2_ref[h, r0:r0 + rows, :] < krow[h], e2_ref[h, r0:r0 + rows, :] * e1row[h],
                               jnp.zeros((), BF16))
                gate = gv if gate is None else gate + gv
            rr = c * PEER_NKEYS + r0
            w_scr[rr:rr + rows, :] = _gelu_tanh(a_scr[rr:rr + rows, :].astype(BF16)) * gate
    o_ref[...] += _dot(vt_ref[...], w_scr[...])


def peer_experts(h2t, u, vt, kc, e1, l2, e2, *, tm=512, tn=512):
    d, s = h2t.shape
    ne = u.shape[0]
    once = pl.Buffered(1)
    table = pl.BlockSpec((PEER_HEADS, PEER_NKEYS, tm), lambda i, n: (0, 0, i), pipeline_mode=once)
    return pl.pallas_call(
        _peer_kernel,
        grid=(s // tm, ne // tn),
        in_specs=[pl.BlockSpec((d, tm), lambda i, n: (0, i), pipeline_mode=once),
                  pl.BlockSpec((tn, d), lambda i, n: (n, 0)),
                  pl.BlockSpec((d, tn), lambda i, n: (0, n)),
                  table, table, table, table],
        out_specs=pl.BlockSpec((d, tm), lambda i, n: (0, i)),
        out_shape=jax.ShapeDtypeStruct((d, s), F32),
        scratch_shapes=[pltpu.VMEM((tn, tm), F32), pltpu.VMEM((tn, tm), BF16)],
        compiler_params=_params(("arbitrary", "arbitrary")),
        name="peer",
    )(h2t, u, vt, kc, e1, l2, e2)


def _fin_kernel(x_ref, pt_ref, g2_ref, fg_ref, o_ref):
    x2 = x_ref[...] + g2_ref[...] * pt_ref[...].T
    ms = jnp.mean(x2 * x2, axis=-1, keepdims=True)
    o_ref[...] = x2 * lax.rsqrt(ms + EPS) * fg_ref[...]


def final_norm(x1, peer_t, gate2, final_g, *, tm=256):
    s, d = x1.shape
    vec = pl.BlockSpec((1, d), lambda i: (0, 0))
    return pl.pallas_call(
        _fin_kernel,
        grid=(s // tm,),
        in_specs=[pl.BlockSpec((tm, d), lambda i: (i, 0)),
                  pl.BlockSpec((d, tm), lambda i: (0, i)), vec, vec],
        out_specs=pl.BlockSpec((tm, d), lambda i: (i, 0)),
        out_shape=jax.ShapeDtypeStruct((s, d), F32),
        compiler_params=_params(("arbitrary",)),
        name="fin",
    )(x1, peer_t, gate2, final_g)


def _layer(x, mod, positions, norm1_g, w_in, b_forget, ret_gn_g, fox_norm_g, w_out, norm2_g,
           w_peer_q, peer_sub_keys, peer_u, peer_v, *, bq=2048, bk=512):
    s, d = x.shape
    shift1, scale1, gate1, shift2, scale2, gate2 = [mod[:, k * d:(k + 1) * d] for k in range(6)]
    row = lambda v: v.reshape(1, -1)

    w_t = w_in.T
    w_main = cast_bf16(w_t, rows=IN_MAIN, transpose=True, br=2048, bc=512, name="cast_in")
    n_ff = w_in.shape[1] - IN_MAIN
    w_ff = tail_weight(w_t, IN_MAIN, n_ff)
    pf32, kb, vt3, ff = in_proj(x, row(norm1_g), scale1, shift1, w_main, w_ff, tm=bk)

    b128 = jnp.pad(b_forget, (0, LANES - n_ff)).reshape(1, LANES)
    caug = cum_gate(ff, b128)
    y_ret = retention(pf32, positions.reshape(s, 1), row(ret_gn_g))
    y_fox = fox_attention(pf32, kb, caug, vt3, bq=bq)
    x1 = out_proj(y_ret, y_fox, row(fox_norm_g), cast_bf16(w_out, name="cast_out"), x, gate1)

    h2t, kc, e1, l2, e2 = peer_query(x1, row(norm2_g), scale2, shift2,
                                     cast_bf16(w_peer_q, name="cast_q"), peer_sub_keys.astype(BF16))
    peer_t = peer_experts(h2t, cast_bf16(peer_u, name="cast_u"),
                          cast_bf16(peer_v, transpose=True, br=1024, bc=1024, name="cast_v"), kc, e1, l2, e2)
    return x1, peer_t, gate2


def kernel(x, c, positions, w_ada, b_ada, norm1_g, w_in, b_forget, ret_gn_g, fox_norm_g, w_out,
           norm2_g, w_peer_q, peer_sub_keys, peer_u, peer_v, final_g):
    b, s, d = x.shape
    assert b == 1 and w_ada.shape[0] == 1, "one sequence, one layer"
    c8 = jnp.broadcast_to(c, (8, d))
    mod = ada_mod(c8, w_ada[0], b_ada[0].reshape(1, -1))[0:1]
    x1, peer_t, gate2 = _layer(x[0], mod, positions[0], norm1_g[0], w_in[0], b_forget[0], ret_gn_g[0],
                               fox_norm_g[0], w_out[0], norm2_g[0], w_peer_q[0], peer_sub_keys[0],
                               peer_u[0], peer_v[0])
    return final_norm(x1, peer_t, gate2, final_g.reshape(1, -1))[None]
```

```python
import functools

import numpy as np
import jax
import jax.numpy as jnp
from jax import lax
from jax.experimental import pallas as pl
from jax.experimental.pallas import tpu as pltpu

F32 = jnp.float32
BF16 = jnp.bfloat16

LANES = 128
RET_HEADS = 8
RET_DK = 128
RET_DV = 256
RET_QK = RET_HEADS * RET_DK
RET_WIDTH = RET_HEADS * RET_DV
RET_CHUNK = 128
FOX_HEADS = 16
FOX_DH = 128
FOX_WIDTH = FOX_HEADS * FOX_DH
IN_MAIN = 2 * RET_QK + 2 * RET_WIDTH + 3 * FOX_WIDTH
N_F32_COLS = 2 * RET_QK + 2 * RET_WIDTH + FOX_WIDTH
ROPE_BASE = 10000.0
PEER_HEADS = 8
PEER_NKEYS = 128
PEER_NEXPERTS = PEER_NKEYS * PEER_NKEYS
PEER_DQ = 256
PEER_TOPK = 16
PEER_NCAND = 80
EPS = 1e-6
NEG_BIG = -1e30
LOG2E = 1.4426950408889634
VMEM_LIMIT = 56 * 1024 * 1024

_NT = (((1,), (1,)), ((), ()))
_TN = (((0,), (0,)), ((), ()))


def _params(sem):
    return pltpu.CompilerParams(dimension_semantics=sem, vmem_limit_bytes=VMEM_LIMIT)


def _dot(a, b):
    return jnp.dot(a, b, preferred_element_type=F32)


def _dot_nt(a, b):
    return lax.dot_general(a, b, _NT, preferred_element_type=F32)


def _ada_kernel(c_ref, w_ref, b_ref, o_ref):
    c = c_ref[...]
    ca = (c / (1.0 + jnp.exp(-c))).astype(BF16)
    o_ref[...] = _dot(ca, w_ref[...].astype(BF16)) + b_ref[...]


def ada_mod(c8, w, b, *, bn=512):
    d, n = w.shape
    return pl.pallas_call(
        _ada_kernel,
        grid=(n // bn,),
        in_specs=[pl.BlockSpec((8, d), lambda j: (0, 0)),
                  pl.BlockSpec((d, bn), lambda j: (0, j)),
                  pl.BlockSpec((1, bn), lambda j: (0, j))],
        out_specs=pl.BlockSpec((8, bn), lambda j: (0, j)),
        out_shape=jax.ShapeDtypeStruct((8, n), F32),
        compiler_params=_params(("arbitrary",)),
        name="ada",
    )(c8, w, b)


def _cast_kernel(x_ref, o_ref):
    o_ref[...] = x_ref[...].astype(BF16)


def _cast_t_kernel(x_ref, o_ref):
    o_ref[...] = x_ref[...].T.astype(BF16)


def cast_bf16(x, *, rows=None, cols=None, br=512, bc=2048, transpose=False, name="cast"):
    r, c = x.shape
    r = r if rows is None else rows
    c = c if cols is None else cols
    br, bc = min(br, r), min(bc, c)
    if transpose:
        return pl.pallas_call(
            _cast_t_kernel, grid=(r // br, c // bc),
            in_specs=[pl.BlockSpec((br, bc), lambda i, j: (i, j))],
            out_specs=pl.BlockSpec((bc, br), lambda i, j: (j, i)),
            out_shape=jax.ShapeDtypeStruct((c, r), BF16),
            compiler_params=_params(("arbitrary", "arbitrary")), name=name)(x)
    return pl.pallas_call(
        _cast_kernel, grid=(r // br, c // bc),
        in_specs=[pl.BlockSpec((br, bc), lambda i, j: (i, j))],
        out_specs=pl.BlockSpec((br, bc), lambda i, j: (i, j)),
        out_shape=jax.ShapeDtypeStruct((r, c), BF16),
        compiler_params=_params(("arbitrary", "arbitrary")), name=name)(x)


def _tail_kernel(x_ref, o_ref):
    o_ref[...] = jnp.zeros_like(o_ref)
    o_ref[:, 0:x_ref.shape[0]] = x_ref[...].T.astype(BF16)


def tail_weight(w_t, row0, nrows):
    d = w_t.shape[1]
    assert row0 % nrows == 0 and nrows % 8 == 0
    return pl.pallas_call(
        _tail_kernel, grid=(1,),
        in_specs=[pl.BlockSpec((nrows, d), lambda i: (row0 // nrows, 0))],
        out_specs=pl.BlockSpec((d, LANES), lambda i: (0, 0)),
        out_shape=jax.ShapeDtypeStruct((d, LANES), BF16),
        compiler_params=_params(("arbitrary",)), name="cast_tail")(w_t)


def _norm_mod(x, g, scale, shift):
    ms = jnp.mean(x * x, axis=-1, keepdims=True)
    y = x * lax.rsqrt(ms + EPS) * g
    return y * (1.0 + scale) + shift


def _inpj_kernel(nf, nk, x_ref, g_ref, sc_ref, sh_ref, w_ref, wf_ref, o_ref, ok_ref, ovt_ref, of_ref, h_scr):
    j = pl.program_id(1)

    @pl.when(j == 0)
    def _():
        hb = _norm_mod(x_ref[...], g_ref[...], sc_ref[...], sh_ref[...]).astype(BF16)
        h_scr[...] = hb
        of_ref[...] = _dot(hb, wf_ref[...])

    tn = w_ref.shape[1]
    piece = tn // 2

    def pieces(store):
        for c0 in range(0, tn, piece):
            store(c0, _dot(h_scr[...], w_ref[:, c0:c0 + piece]))

    @pl.when(j < nf)
    def _():
        def store(c0, r):
            o_ref[:, c0:c0 + piece] = r
        pieces(store)

    @pl.when(jnp.logical_and(j >= nf, j < nf + nk))
    def _():
        def store(c0, r):
            ok_ref[:, c0:c0 + piece] = r.astype(BF16)
        pieces(store)

    @pl.when(j >= nf + nk)
    def _():
        def store(c0, r):
            ovt_ref[0, c0:c0 + piece, :] = r.T.astype(BF16)
        pieces(store)


def in_proj(x, g, scale, shift, w_main, w_ff, *, tm=512, tn=1024):
    s, d = x.shape
    n = w_main.shape[1]
    nf, nk = N_F32_COLS // tn, FOX_WIDTH // tn
    vec = pl.BlockSpec((1, d), lambda i, j: (0, 0))
    return pl.pallas_call(
        functools.partial(_inpj_kernel, nf, nk),
        grid=(s // tm, n // tn),
        in_specs=[pl.BlockSpec((tm, d), lambda i, j: (i, 0)), vec, vec, vec,
                  pl.BlockSpec((d, tn), lambda i, j: (0, j)),
                  pl.BlockSpec((d, LANES), lambda i, j: (0, 0))],
        out_specs=[pl.BlockSpec((tm, tn), lambda i, j: (i, jnp.minimum(j, nf - 1))),
                   pl.BlockSpec((tm, tn), lambda i, j: (i, jnp.clip(j - nf, 0, nk - 1))),
                   pl.BlockSpec((1, tn, tm), lambda i, j: (i, jnp.clip(j - nf - nk, 0, nk - 1), 0)),
                   pl.BlockSpec((tm, LANES), lambda i, j: (i, 0))],
        out_shape=[jax.ShapeDtypeStruct((s, N_F32_COLS), F32),
                   jax.ShapeDtypeStruct((s, FOX_WIDTH), BF16),
                   jax.ShapeDtypeStruct((s // tm, FOX_WIDTH, tm), BF16),
                   jax.ShapeDtypeStruct((s, LANES), F32)],
        scratch_shapes=[pltpu.VMEM((tm, d), BF16)],
        compiler_params=_params(("arbitrary", "arbitrary")),
        name="inpj",
    )(x, g, scale, shift, w_main, w_ff)


def _split3(v):
    hi = v.astype(BF16)
    r1 = v - hi.astype(F32)
    mid = r1.astype(BF16)
    lo = (r1 - mid.astype(F32)).astype(BF16)
    return hi, mid, lo


def _cum_place():
    m = np.zeros((3 * LANES, FOX_HEADS * LANES), np.float32)
    for p in range(3):
        for h in range(FOX_HEADS):
            m[p * LANES + h, h * LANES + p] = 1.0
    return m


def _cum_kernel(ff_ref, b_ref, place_ref, o_ref, carry):
    @pl.when(pl.program_id(0) == 0)
    def _():
        carry[...] = jnp.zeros_like(carry)

    z = ff_ref[...] + b_ref[...]
    logf = jnp.minimum(z, 0.0) - jnp.log(1.0 + jnp.exp(-jnp.abs(z)))
    tc = z.shape[0]
    row = lax.broadcasted_iota(jnp.int32, (tc, tc), 0)
    col = lax.broadcasted_iota(jnp.int32, (tc, tc), 1)
    tri = jnp.where(row >= col, 1.0, 0.0).astype(BF16)
    hi, mid, lo = _split3(logf)
    cum = _dot(tri, hi) + _dot(tri, mid) + _dot(tri, lo) + carry[...]
    carry[...] = cum[tc - 1:tc, :]
    pieces = jnp.concatenate(_split3(cum * LOG2E), axis=1)
    o_ref[...] = _dot(pieces, place_ref[...]).astype(BF16)


def cum_gate(ff, b128, *, tc=256):
    s = ff.shape[0]
    place = jnp.asarray(_cum_place(), BF16)
    return pl.pallas_call(
        _cum_kernel,
        grid=(s // tc,),
        in_specs=[pl.BlockSpec((tc, LANES), lambda i: (i, 0)),
                  pl.BlockSpec((1, LANES), lambda i: (0, 0)),
                  pl.BlockSpec(place.shape, lambda i: (0, 0))],
        out_specs=pl.BlockSpec((tc, FOX_HEADS * LANES), lambda i: (i, 0)),
        out_shape=jax.ShapeDtypeStruct((s, FOX_HEADS * LANES), BF16),
        scratch_shapes=[pltpu.VMEM((1, LANES), F32)],
        compiler_params=_params(("arbitrary",)),
        name="cum",
    )(ff, b128, place)


def _ret_consts():
    h = np.arange(RET_HEADS, dtype=np.float32)
    log_g = np.log1p(-np.exp2(-5.0 - h)).astype(np.float32)
    i = np.arange(RET_CHUNK, dtype=np.float32)
    diff = i[:, None] - i[None, :]
    dmat = np.where(diff >= 0, np.exp(np.maximum(diff, 0.0)[None] * log_g[:, None, None]), 0.0)
    kdec = np.exp((RET_CHUNK - 1.0 - i)[None, :] * log_g[:, None])
    qdec = np.exp((i + 1.0)[None, :] * log_g[:, None])
    cdec = np.exp(RET_CHUNK * log_g)
    kdec = np.broadcast_to(kdec[:, :, None], (RET_HEADS, RET_CHUNK, RET_DK))
    qdec = np.broadcast_to(qdec[:, :, None], (RET_HEADS, RET_CHUNK, RET_DK))
    half = RET_DK // 2
    inv = (ROPE_BASE ** (-np.arange(half, dtype=np.float32) / half)).astype(np.float32)
    inv_full = np.concatenate([inv, inv])[None, :]
    sign = np.concatenate([-np.ones(half, np.float32), np.ones(half, np.float32)])[None, :]
    return (dmat.astype(np.float32), np.ascontiguousarray(kdec, np.float32),
            np.ascontiguousarray(qdec, np.float32), [float(v) for v in cdec],
            inv_full.astype(np.float32), sign)


def _ret_kernel(cdec, rq_ref, rk_ref, rv_ref, rg_ref, pos_ref, inv_ref, sign_ref,
                dmat_ref, kdec_ref, qdec_ref, gn_ref, o_ref, state):
    @pl.when(pl.program_id(0) == 0)
    def _():
        state[...] = jnp.zeros_like(state)

    ang = pos_ref[...].astype(F32) * inv_ref[...]
    cosf = jnp.cos(ang)
    sinf = jnp.sin(ang) * sign_ref[...]
    kscale = RET_DK ** -0.5
    for h in range(RET_HEADS):
        q = rq_ref[:, h * RET_DK:(h + 1) * RET_DK]
        k = rk_ref[:, h * RET_DK:(h + 1) * RET_DK]
        qr = q * cosf + pltpu.roll(q, RET_DK // 2, 1) * sinf
        kr = (k * cosf + pltpu.roll(k, RET_DK // 2, 1) * sinf) * kscale
        vb = rv_ref[:, h * RET_DV:(h + 1) * RET_DV].astype(BF16)
        scores = _dot_nt(qr.astype(BF16), kr.astype(BF16)) * dmat_ref[h]
        intra = _dot(scores.astype(BF16), vb)
        st = state[h]
        cross = _dot((qr * qdec_ref[h]).astype(BF16), st.astype(BF16))
        kd = (kr * kdec_ref[h]).astype(BF16)
        kv = lax.dot_general(kd, vb, _TN, preferred_element_type=F32)
        state[h] = st * cdec[h] + kv
        y = intra + cross
        mu = jnp.mean(y, axis=-1, keepdims=True)
        dlt = y - mu
        var = jnp.mean(dlt * dlt, axis=-1, keepdims=True)
        g = rg_ref[:, h * RET_DV:(h + 1) * RET_DV]
        yn = dlt * lax.rsqrt(var + EPS) * gn_ref[:, h * RET_DV:(h + 1) * RET_DV] * (g / (1.0 + jnp.exp(-g)))
        o_ref[:, h * RET_DV:(h + 1) * RET_DV] = yn.astype(BF16)


def retention(proj, pos_col, gn_g):
    s = proj.shape[0]
    c = RET_CHUNK
    dmat, kdec, qdec, cdec, inv_full, sign = _ret_consts()
    full3 = lambda n: (0, 0, 0)
    return pl.pallas_call(
        functools.partial(_ret_kernel, cdec),
        grid=(s // c,),
        in_specs=[pl.BlockSpec((c, RET_QK), lambda n: (n, 0)),
                  pl.BlockSpec((c, RET_QK), lambda n: (n, 1)),
                  pl.BlockSpec((c, RET_WIDTH), lambda n: (n, 1)),
                  pl.BlockSpec((c, RET_WIDTH), lambda n: (n, 2)),
                  pl.BlockSpec((c, 1), lambda n: (n, 0)),
                  pl.BlockSpec((1, RET_DK), lambda n: (0, 0)),
                  pl.BlockSpec((1, RET_DK), lambda n: (0, 0)),
                  pl.BlockSpec((RET_HEADS, c, c), full3),
                  pl.BlockSpec((RET_HEADS, c, RET_DK), full3),
                  pl.BlockSpec((RET_HEADS, c, RET_DK), full3),
                  pl.BlockSpec((1, RET_WIDTH), lambda n: (0, 0))],
        out_specs=pl.BlockSpec((c, RET_WIDTH), lambda n: (n, 0)),
        out_shape=jax.ShapeDtypeStruct((s, RET_WIDTH), BF16),
        scratch_shapes=[pltpu.VMEM((RET_HEADS, RET_DK, RET_DV), F32)],
        compiler_params=_params(("arbitrary",)),
        name="ret",
    )(proj, proj, proj, proj, pos_col, jnp.asarray(inv_full), jnp.asarray(sign),
      jnp.asarray(dmat), jnp.asarray(kdec), jnp.asarray(qdec), gn_g)


def _fox_kernel(q_ref, k_ref, c_ref, vt_ref, o_ref, m_scr, l_scr, acc_scr, *bufs):
    i = pl.program_id(1)
    bq = q_ref.shape[0]
    bk = bufs[0].shape[0]
    nd = len(bufs)
    assert nd * bk == bq
    sub = 8
    lane = lax.broadcasted_iota(jnp.int32, (bq, FOX_DH), 1)
    qa = jnp.where(lane < 3, -1.0, 0.0).astype(BF16)
    q2 = jnp.concatenate([(q_ref[...] * (FOX_DH ** -0.5 * LOG2E)).astype(BF16), qa], axis=1)
    m_scr[...] = jnp.full_like(m_scr, NEG_BIG)
    l_scr[...] = jnp.zeros_like(l_scr)
    acc_scr[...] = jnp.zeros_like(acc_scr)
    ones_rows = jnp.where(lax.broadcasted_iota(jnp.int32, (2 * sub, bk), 0) == 0, 1.0, 0.0).astype(BF16)

    def logits(j, s_scr, c0=0):
        off = pl.multiple_of(j * bk, bk)
        k2 = jnp.concatenate([k_ref[pl.ds(off, bk), :], c_ref[pl.ds(off, bk), :]], axis=1)
        s_scr[:, c0:bq] = _dot_nt(k2, q2[c0:bq, :])

    def update_cols(j, s_scr, c0, c1, diagonal):
        w = c1 - c0
        st = s_scr[:, c0:c1].reshape(bk // sub, sub, w)
        if diagonal:
            kid = lax.broadcasted_iota(jnp.int32, (bk, bk), 0).reshape(bk // sub, sub, bk)
            qid = lax.broadcasted_iota(jnp.int32, (bk, bk), 1).reshape(bk // sub, sub, bk)
            st = jnp.where(kid <= qid, st, NEG_BIG)
        m_old = m_scr[:, c0:c1]
        m_new = jnp.maximum(m_old, jnp.max(jnp.max(st, axis=0), axis=0, keepdims=True))
        alpha = jnp.exp2(m_old - m_new)
        p = jnp.exp2(st - jnp.broadcast_to(m_new, (sub, w))[None])
        pv = _dot(jnp.concatenate([vt_ref[j], ones_rows], axis=0), p.reshape(bk, w).astype(BF16))
        l_scr[:, c0:c1] = alpha * l_scr[:, c0:c1] + pv[FOX_DH:FOX_DH + 1, :]
        a8 = jnp.broadcast_to(alpha, (sub, w))[None]
        acc_scr[:, c0:c1] = (a8 * acc_scr[:, c0:c1].reshape(FOX_DH // sub, sub, w)).reshape(FOX_DH, w) + pv[0:FOX_DH, :]
        m_scr[:, c0:c1] = m_new

    def update(j, s_scr, c0=0, diagonal=False):
        if diagonal:
            update_cols(j, s_scr, c0, c0 + bk, True)
            c0 += bk
        if c0 < bq:
            update_cols(j, s_scr, c0, bq, False)

    ahead = 2
    for k in range(ahead):
        logits(k, bufs[k])

    def body(jj, carry):
        for k in range(nd):
            logits(nd * jj + k + ahead, bufs[(k + ahead) % nd])
            update(nd * jj + k, bufs[k])
        return carry

    lax.fori_loop(0, i, body, 0)
    for k in range(nd):
        if k + ahead < nd:
            logits(nd * i + k + ahead, bufs[k + ahead], (k + ahead) * bk)
        update(nd * i + k, bufs[k], k * bk, diagonal=True)
    o_ref[...] = (acc_scr[...] / l_scr[...]).T


def fox_attention(pf32, kb, caug, vt3, *, bq=2048):
    s = pf32.shape[0]
    bk = vt3.shape[2]
    assert vt3.shape == (s // bk, FOX_WIDTH, bk) and bq % (2 * bk) == 0
    qcol0 = (N_F32_COLS - FOX_WIDTH) // FOX_DH
    return pl.pallas_call(
        _fox_kernel,
        grid=(FOX_HEADS, s // bq),
        in_specs=[pl.BlockSpec((bq, FOX_DH), lambda h, i: (i, qcol0 + h)),
                  pl.BlockSpec((s, FOX_DH), lambda h, i: (0, h)),
                  pl.BlockSpec((s, LANES), lambda h, i: (0, h)),
                  pl.BlockSpec((s // bk, FOX_DH, bk), lambda h, i: (0, h, 0))],
        out_specs=pl.BlockSpec((bq, FOX_DH), lambda h, i: (i, h)),
        out_shape=jax.ShapeDtypeStruct((s, FOX_WIDTH), F32),
        scratch_shapes=[pltpu.VMEM((1, bq), F32), pltpu.VMEM((1, bq), F32), pltpu.VMEM((FOX_DH, bq), F32)]
        + [pltpu.VMEM((bk, bq), F32)] * (bq // bk),
        compiler_params=_params(("arbitrary", "arbitrary")),
        name="fox",
    )(pf32, kb, caug, vt3)


def _outp_kernel(yr_ref, yf_ref, fg_ref, w_ref, x_ref, g1_ref, o_ref, y_scr):
    @pl.when(pl.program_id(1) == 0)
    def _():
        yf = yf_ref[...]
        ms = jnp.mean(yf * yf, axis=-1, keepdims=True)
        yn = yf * lax.rsqrt(ms + EPS) * fg_ref[...]
        y_scr[:, 0:RET_WIDTH] = yr_ref[...]
        y_scr[:, RET_WIDTH:RET_WIDTH + FOX_WIDTH] = yn.astype(BF16)

    tn = w_ref.shape[1]
    piece = tn // 2
    for c0 in range(0, tn, piece):
        cs = slice(c0, c0 + piece)
        o_ref[:, cs] = x_ref[:, cs] + g1_ref[:, cs] * _dot(y_scr[...], w_ref[:, cs])


def out_proj(y_ret, y_fox, fox_g, w_out, x, gate1, *, tm=512, tn=1024):
    s, d = x.shape
    kdim = RET_WIDTH + FOX_WIDTH
    return pl.pallas_call(
        _outp_kernel,
        grid=(s // tm, d // tn),
        in_specs=[pl.BlockSpec((tm, RET_WIDTH), lambda i, j: (i, 0)),
                  pl.BlockSpec((tm, FOX_WIDTH), lambda i, j: (i, 0)),
                  pl.BlockSpec((1, FOX_WIDTH), lambda i, j: (0, 0)),
                  pl.BlockSpec((kdim, tn), lambda i, j: (0, j)),
                  pl.BlockSpec((tm, tn), lambda i, j: (i, j)),
                  pl.BlockSpec((1, tn), lambda i, j: (0, j))],
        out_specs=pl.BlockSpec((tm, tn), lambda i, j: (i, j)),
        out_shape=jax.ShapeDtypeStruct((s, d), F32),
        scratch_shapes=[pltpu.VMEM((tm, kdim), BF16)],
        compiler_params=_params(("arbitrary", "arbitrary")),
        name="outp",
    )(y_ret, y_fox, fox_g, w_out, x, gate1)


def _drop_max(cur):
    mx = jnp.max(jnp.max(cur, axis=0), axis=0, keepdims=True)
    hit = cur == jnp.broadcast_to(mx, cur.shape[1:])[None]
    return mx, hit, jnp.where(hit, -jnp.inf, cur)


def _top_rows(s, t_scr, ls):
    cur = s.reshape(s.shape[0] // 8, 8, s.shape[1])
    level = jnp.full(cur.shape, float(PEER_TOPK), F32)
    for r in range(PEER_TOPK):
        mx, hit, cur = _drop_max(cur)
        t_scr[r:r + 1, ls] = mx
        level = jnp.where(hit, float(r), level)
    return level.reshape(s.shape)


def _peerq_kernel(group, x_ref, g_ref, sc_ref, sh_ref, wq_ref, keys_ref, u_ref, v_ref,
                  h2t_ref, kc_ref, e1_ref, l2_ref, e2_ref, ub_ref, vtb_ref,
                  h_scr, s1_scr, s2_scr, t1_scr, t2_scr, cand_scr):
    step = pl.program_id(0) * pl.num_programs(1) + pl.program_id(1)

    @pl.when(step % group == 0)
    def _():
        ub_ref[...] = u_ref[...].astype(BF16)
        vtb_ref[...] = v_ref[...].T.astype(BF16)

    @pl.when(pl.program_id(1) == 0)
    def _():
        h = _norm_mod(x_ref[...], g_ref[...], sc_ref[...], sh_ref[...])
        h_scr[...] = h.astype(BF16)
        h2t_ref[...] = h.T.astype(BF16)

    half = PEER_DQ // 2
    qb = _dot(h_scr[...], wq_ref[...]).astype(BF16)
    s1 = _dot_nt(keys_ref[0, 0], qb[:, 0:half])
    s2 = _dot_nt(keys_ref[0, 1], qb[:, half:PEER_DQ])
    s1_scr[...] = s1
    s2_scr[...] = s2
    for g in range(s1.shape[1] // LANES):
        ls = slice(g * LANES, (g + 1) * LANES)
        s1l = s1_scr[:, ls]
        s2l = s2_scr[:, ls]
        _top_rows(s1l, t1_scr, ls)
        l2_ref[0, :, ls] = _top_rows(s2l, t2_scr, ls).astype(BF16)
        t1_lo = t1_scr[0:8, ls]
        for b in range(8):
            cand_scr[b * 8:(b + 1) * 8, ls] = t1_lo + t2_scr[b:b + 1, ls]
        cand_scr[64:72, ls] = t1_scr[0:1, ls] + t2_scr[8:16, ls]
        cand_scr[72:80, ls] = t1_scr[8:16, ls] + t2_scr[0:1, ls]
        cand = cand_scr[:, ls].reshape(PEER_NCAND // 8, 8, LANES)
        cur = cand
        for r in range(PEER_TOPK):
            tau, _, cur = _drop_max(cur)
        m1 = t1_scr[0:1, ls]
        m2 = t2_scr[0:1, ls]
        top8 = jnp.broadcast_to(m1 + m2, (8, LANES))[None]
        tau8 = jnp.broadcast_to(tau, (8, LANES))[None]
        z = jnp.sum(jnp.sum(jnp.where(cand >= tau8, jnp.exp(cand - top8), 0.0), axis=0), axis=0, keepdims=True)
        s1g = s1l.reshape(PEER_NKEYS // 8, 8, LANES)
        kc = jnp.zeros(s1g.shape, F32)
        for b in range(PEER_TOPK):
            t2b = jnp.broadcast_to(t2_scr[b:b + 1, ls], (8, LANES))[None]
            kc = jnp.where(s1g + t2b >= tau8, float(b + 1), kc)
        kc_ref[0, :, ls] = kc.reshape(PEER_NKEYS, LANES)
        e1_ref[0, :, ls] = jnp.exp(s1l - (m1 + jnp.log(z)))
        e2_ref[0, :, ls] = jnp.exp(s2l - m2).astype(BF16)


def peer_query(x1, g, scale, shift, wq, keys, u, v, *, tm=512):
    s, d = x1.shape
    ne = u.shape[0]
    nsteps = (s // tm) * PEER_HEADS
    per_step = ne // nsteps
    group = max(1, LANES // per_step)
    rows = per_step * group
    assert rows * (nsteps // group) == ne and rows % LANES == 0
    vec = pl.BlockSpec((1, d), lambda i, h: (0, 0))
    stat = pl.BlockSpec((1, PEER_NKEYS, tm), lambda i, h: (h, 0, i))
    table = lambda dt: jax.ShapeDtypeStruct((PEER_HEADS, PEER_NKEYS, s), dt)
    eblk = lambda i, h: (i * PEER_HEADS + h) // group
    return pl.pallas_call(
        functools.partial(_peerq_kernel, group),
        grid=(s // tm, PEER_HEADS),
        in_specs=[pl.BlockSpec((tm, d), lambda i, h: (i, 0)), vec, vec, vec,
                  pl.BlockSpec((d, PEER_DQ), lambda i, h: (0, h)),
                  pl.BlockSpec((1, 2, PEER_NKEYS, PEER_DQ // 2), lambda i, h: (h, 0, 0, 0)),
                  pl.BlockSpec((rows, d), lambda i, h: (eblk(i, h), 0)),
                  pl.BlockSpec((rows, d), lambda i, h: (eblk(i, h), 0))],
        out_specs=[pl.BlockSpec((d, tm), lambda i, h: (0, i)), stat, stat, stat, stat,
                   pl.BlockSpec((rows, d), lambda i, h: (eblk(i, h), 0)),
                   pl.BlockSpec((d, rows), lambda i, h: (0, eblk(i, h)))],
        out_shape=[jax.ShapeDtypeStruct((d, s), BF16), table(F32), table(F32), table(BF16), table(BF16),
                   jax.ShapeDtypeStruct((ne, d), BF16), jax.ShapeDtypeStruct((d, ne), BF16)],
        scratch_shapes=[pltpu.VMEM((tm, d), BF16),
                        pltpu.VMEM((PEER_NKEYS, tm), F32), pltpu.VMEM((PEER_NKEYS, tm), F32),
                        pltpu.VMEM((PEER_TOPK, tm), F32), pltpu.VMEM((PEER_TOPK, tm), F32),
                        pltpu.VMEM((PEER_NCAND, tm), F32)],
        compiler_params=_params(("arbitrary", "arbitrary")),
        name="peerq",
    )(x1, g, scale, shift, wq, keys, u, v)


_GELU_C = 0.7978845608028654


def _gelu_tanh(x):
    k1 = -2.0 * _GELU_C * LOG2E
    z2 = x * (k1 + (k1 * 0.044715) * (x * x))
    return x / (1.0 + jnp.exp2(z2))


def _peer_kernel(h2t_ref, u_ref, vt_ref, kc_ref, e1_ref, l2_ref, e2_ref, o_ref, a_scr, w_scr):
    n = pl.program_id(1)
    tm = h2t_ref.shape[1]
    tn = u_ref.shape[0]
    rows = 16
    nrow = tn // PEER_NKEYS

    @pl.when(n == 0)
    def _():
        o_ref[...] = jnp.zeros_like(o_ref)

    a_scr[...] = _dot(u_ref[...], h2t_ref[...])
    for c in range(nrow):
        i1 = n * nrow + c
        krow = [jnp.broadcast_to(kc_ref[h, pl.ds(i1, 1), :], (rows, tm)).astype(BF16) for h in range(PEER_HEADS)]
        e1row = [jnp.broadcast_to(e1_ref[h, pl.ds(i1, 1), :], (rows, tm)).astype(BF16) for h in range(PEER_HEADS)]
        for r0 in range(0, PEER_NKEYS, rows):
            gate = None
            for h in range(PEER_HEADS):
                gv = jnp.where(l2_ref[h, r0:r0 + rows, :] < krow[h], e2_ref[h, r0:r0 + rows, :] * e1row[h],
                               jnp.zeros((), BF16))
                gate = gv if gate is None else gate + gv
            rr = c * PEER_NKEYS + r0
            w_scr[rr:rr + rows, :] = _gelu_tanh(a_scr[rr:rr + rows, :].astype(BF16)) * gate
    o_ref[...] += _dot(vt_ref[...], w_scr[...])


def peer_experts(h2t, u, vt, kc, e1, l2, e2, *, tm=512, tn=512):
    d, s = h2t.shape
    ne = u.shape[0]
    once = pl.Buffered(1)
    table = pl.BlockSpec((PEER_HEADS, PEER_NKEYS, tm), lambda i, n: (0, 0, i), pipeline_mode=once)
    return pl.pallas_call(
        _peer_kernel,
        grid=(s // tm, ne // tn),
        in_specs=[pl.BlockSpec((d, tm), lambda i, n: (0, i), pipeline_mode=once),
                  pl.BlockSpec((tn, d), lambda i, n: (n, 0)),
                  pl.BlockSpec((d, tn), lambda i, n: (0, n)),
                  table, table, table, table],
        out_specs=pl.BlockSpec((d, tm), lambda i, n: (0, i)),
        out_shape=jax.ShapeDtypeStruct((d, s), F32),
        scratch_shapes=[pltpu.VMEM((tn, tm), F32), pltpu.VMEM((tn, tm), BF16)],
        compiler_params=_params(("arbitrary", "arbitrary")),
        name="peer",
    )(h2t, u, vt, kc, e1, l2, e2)


def _fin_kernel(x_ref, pt_ref, g2_ref, fg_ref, o_ref):
    x2 = x_ref[...] + g2_ref[...] * pt_ref[...].T
    ms = jnp.mean(x2 * x2, axis=-1, keepdims=True)
    o_ref[...] = x2 * lax.rsqrt(ms + EPS) * fg_ref[...]


def final_norm(x1, peer_t, gate2, final_g, *, tm=256):
    s, d = x1.shape
    vec = pl.BlockSpec((1, d), lambda i: (0, 0))
    return pl.pallas_call(
        _fin_kernel,
        grid=(s // tm,),
        in_specs=[pl.BlockSpec((tm, d), lambda i: (i, 0)),
                  pl.BlockSpec((d, tm), lambda i: (0, i)), vec, vec],
        out_specs=pl.BlockSpec((tm, d), lambda i: (i, 0)),
        out_shape=jax.ShapeDtypeStruct((s, d), F32),
        compiler_params=_params(("arbitrary",)),
        name="fin",
    )(x1, peer_t, gate2, final_g)


def _layer(x, mod, positions, norm1_g, w_in, b_forget, ret_gn_g, fox_norm_g, w_out, norm2_g,
           w_peer_q, peer_sub_keys, peer_u, peer_v, *, bq=2048, bk=512):
    s, d = x.shape
    shift1, scale1, gate1, shift2, scale2, gate2 = [mod[:, k * d:(k + 1) * d] for k in range(6)]
    row = lambda v: v.reshape(1, -1)

    w_t = w_in.T
    w_main = cast_bf16(w_t, rows=IN_MAIN, transpose=True, br=2048, bc=512, name="cast_in")
    n_ff = w_in.shape[1] - IN_MAIN
    w_ff = tail_weight(w_t, IN_MAIN, n_ff)
    pf32, kb, vt3, ff = in_proj(x, row(norm1_g), scale1, shift1, w_main, w_ff, tm=bk)

    b128 = jnp.pad(b_forget, (0, LANES - n_ff)).reshape(1, LANES)
    caug = cum_gate(ff, b128)
    y_ret = retention(pf32, positions.reshape(s, 1), row(ret_gn_g))
    y_fox = fox_attention(pf32, kb, caug, vt3, bq=bq)
    x1 = out_proj(y_ret, y_fox, row(fox_norm_g), cast_bf16(w_out, name="cast_out"), x, gate1)

    h2t, kc, e1, l2, e2, ub, vtb = peer_query(x1, row(norm2_g), scale2, shift2, cast_bf16(w_peer_q, name="cast_q"),
                                              peer_sub_keys.astype(BF16), peer_u, peer_v)
    peer_t = peer_experts(h2t, ub, vtb, kc, e1, l2, e2)
    return x1, peer_t, gate2


def kernel(x, c, positions, w_ada, b_ada, norm1_g, w_in, b_forget, ret_gn_g, fox_norm_g, w_out,
           norm2_g, w_peer_q, peer_sub_keys, peer_u, peer_v, final_g):
    b, s, d = x.shape
    assert b == 1 and w_ada.shape[0] == 1, "one sequence, one layer"
    c8 = jnp.broadcast_to(c, (8, d))
    mod = ada_mod(c8, w_ada[0], b_ada[0].reshape(1, -1))[0:1]
    x1, peer_t, gate2 = _layer(x[0], mod, positions[0], norm1_g[0], w_in[0], b_forget[0], ret_gn_g[0],
                               fox_norm_g[0], w_out[0], norm2_g[0], w_peer_q[0], peer_sub_keys[0],
                               peer_u[0], peer_v[0])
    return final_norm(x1, peer_t, gate2, final_g.reshape(1, -1))[None]
```

```python
import functools

import numpy as np
import jax
import jax.numpy as jnp
from jax import lax
from jax.experimental import pallas as pl
from jax.experimental.pallas import tpu as pltpu

F32 = jnp.float32
BF16 = jnp.bfloat16

LANES = 128
RET_HEADS = 8
RET_DK = 128
RET_DV = 256
RET_QK = RET_HEADS * RET_DK
RET_WIDTH = RET_HEADS * RET_DV
RET_CHUNK = 128
FOX_HEADS = 16
FOX_DH = 128
FOX_WIDTH = FOX_HEADS * FOX_DH
IN_MAIN = 2 * RET_QK + 2 * RET_WIDTH + 3 * FOX_WIDTH
N_F32_COLS = 2 * RET_QK + 2 * RET_WIDTH + FOX_WIDTH
ROPE_BASE = 10000.0
PEER_HEADS = 8
PEER_NKEYS = 128
PEER_NEXPERTS = PEER_NKEYS * PEER_NKEYS
PEER_DQ = 256
PEER_TOPK = 16
PEER_NCAND = 80
EPS = 1e-6
NEG_BIG = -1e30
LOG2E = 1.4426950408889634
VMEM_LIMIT = 56 * 1024 * 1024

_NT = (((1,), (1,)), ((), ()))
_TN = (((0,), (0,)), ((), ()))


def _params(sem):
    return pltpu.CompilerParams(dimension_semantics=sem, vmem_limit_bytes=VMEM_LIMIT)


def _dot(a, b):
    return jnp.dot(a, b, preferred_element_type=F32)


def _dot_nt(a, b):
    return lax.dot_general(a, b, _NT, preferred_element_type=F32)


def _ada_kernel(c_ref, w_ref, b_ref, o_ref):
    c = c_ref[...]
    ca = (c / (1.0 + jnp.exp(-c))).astype(BF16)
    o_ref[...] = _dot(ca, w_ref[...].astype(BF16)) + b_ref[...]


def ada_mod(c8, w, b, *, bn=512):
    d, n = w.shape
    return pl.pallas_call(
        _ada_kernel,
        grid=(n // bn,),
        in_specs=[pl.BlockSpec((8, d), lambda j: (0, 0)),
                  pl.BlockSpec((d, bn), lambda j: (0, j)),
                  pl.BlockSpec((1, bn), lambda j: (0, j))],
        out_specs=pl.BlockSpec((8, bn), lambda j: (0, j)),
        out_shape=jax.ShapeDtypeStruct((8, n), F32),
        compiler_params=_params(("arbitrary",)),
        name="ada",
    )(c8, w, b)


def _cast_kernel(x_ref, o_ref):
    o_ref[...] = x_ref[...].astype(BF16)


def _cast_t_kernel(x_ref, o_ref):
    o_ref[...] = x_ref[...].T.astype(BF16)


def cast_bf16(x, *, rows=None, cols=None, br=512, bc=2048, transpose=False, name="cast"):
    r, c = x.shape
    r = r if rows is None else rows
    c = c if cols is None else cols
    br, bc = min(br, r), min(bc, c)
    if transpose:
        return pl.pallas_call(
            _cast_t_kernel, grid=(r // br, c // bc),
            in_specs=[pl.BlockSpec((br, bc), lambda i, j: (i, j))],
            out_specs=pl.BlockSpec((bc, br), lambda i, j: (j, i)),
            out_shape=jax.ShapeDtypeStruct((c, r), BF16),
            compiler_params=_params(("arbitrary", "arbitrary")), name=name)(x)
    return pl.pallas_call(
        _cast_kernel, grid=(r // br, c // bc),
        in_specs=[pl.BlockSpec((br, bc), lambda i, j: (i, j))],
        out_specs=pl.BlockSpec((br, bc), lambda i, j: (i, j)),
        out_shape=jax.ShapeDtypeStruct((r, c), BF16),
        compiler_params=_params(("arbitrary", "arbitrary")), name=name)(x)


def _tail_kernel(x_ref, o_ref):
    o_ref[...] = jnp.zeros_like(o_ref)
    o_ref[:, 0:x_ref.shape[0]] = x_ref[...].T.astype(BF16)


def tail_weight(w_t, row0, nrows):
    d = w_t.shape[1]
    assert row0 % nrows == 0 and nrows % 8 == 0
    return pl.pallas_call(
        _tail_kernel, grid=(1,),
        in_specs=[pl.BlockSpec((nrows, d), lambda i: (row0 // nrows, 0))],
        out_specs=pl.BlockSpec((d, LANES), lambda i: (0, 0)),
        out_shape=jax.ShapeDtypeStruct((d, LANES), BF16),
        compiler_params=_params(("arbitrary",)), name="cast_tail")(w_t)


def _norm_mod(x, g, scale, shift):
    ms = jnp.mean(x * x, axis=-1, keepdims=True)
    y = x * lax.rsqrt(ms + EPS) * g
    return y * (1.0 + scale) + shift


def _inpj_kernel(nf, nk, x_ref, g_ref, sc_ref, sh_ref, w_ref, wf_ref, o_ref, ok_ref, ovt_ref, of_ref, h_scr):
    j = pl.program_id(1)

    @pl.when(j == 0)
    def _():
        hb = _norm_mod(x_ref[...], g_ref[...], sc_ref[...], sh_ref[...]).astype(BF16)
        h_scr[...] = hb
        of_ref[...] = _dot(hb, wf_ref[...])

    tn = w_ref.shape[1]
    piece = tn // 2

    def pieces(store):
        for c0 in range(0, tn, piece):
            store(c0, _dot(h_scr[...], w_ref[:, c0:c0 + piece]))

    @pl.when(j < nf)
    def _():
        def store(c0, r):
            o_ref[:, c0:c0 + piece] = r
        pieces(store)

    @pl.when(jnp.logical_and(j >= nf, j < nf + nk))
    def _():
        def store(c0, r):
            ok_ref[:, c0:c0 + piece] = r.astype(BF16)
        pieces(store)

    @pl.when(j >= nf + nk)
    def _():
        def store(c0, r):
            ovt_ref[0, c0:c0 + piece, :] = r.T.astype(BF16)
        pieces(store)


def in_proj(x, g, scale, shift, w_main, w_ff, *, tm=512, tn=1024):
    s, d = x.shape
    n = w_main.shape[1]
    nf, nk = N_F32_COLS // tn, FOX_WIDTH // tn
    vec = pl.BlockSpec((1, d), lambda i, j: (0, 0))
    return pl.pallas_call(
        functools.partial(_inpj_kernel, nf, nk),
        grid=(s // tm, n // tn),
        in_specs=[pl.BlockSpec((tm, d), lambda i, j: (i, 0)), vec, vec, vec,
                  pl.BlockSpec((d, tn), lambda i, j: (0, j)),
                  pl.BlockSpec((d, LANES), lambda i, j: (0, 0))],
        out_specs=[pl.BlockSpec((tm, tn), lambda i, j: (i, jnp.minimum(j, nf - 1))),
                   pl.BlockSpec((tm, tn), lambda i, j: (i, jnp.clip(j - nf, 0, nk - 1))),
                   pl.BlockSpec((1, tn, tm), lambda i, j: (i, jnp.clip(j - nf - nk, 0, nk - 1), 0)),
                   pl.BlockSpec((tm, LANES), lambda i, j: (i, 0))],
        out_shape=[jax.ShapeDtypeStruct((s, N_F32_COLS), F32),
                   jax.ShapeDtypeStruct((s, FOX_WIDTH), BF16),
                   jax.ShapeDtypeStruct((s // tm, FOX_WIDTH, tm), BF16),
                   jax.ShapeDtypeStruct((s, LANES), F32)],
        scratch_shapes=[pltpu.VMEM((tm, d), BF16)],
        compiler_params=_params(("arbitrary", "arbitrary")),
        name="inpj",
    )(x, g, scale, shift, w_main, w_ff)


def _split3(v):
    hi = v.astype(BF16)
    r1 = v - hi.astype(F32)
    mid = r1.astype(BF16)
    lo = (r1 - mid.astype(F32)).astype(BF16)
    return hi, mid, lo


def _cum_place():
    m = np.zeros((3 * LANES, FOX_HEADS * LANES), np.float32)
    for p in range(3):
        for h in range(FOX_HEADS):
            m[p * LANES + h, h * LANES + p] = 1.0
    return m


def _cum_kernel(ff_ref, b_ref, place_ref, o_ref, carry):
    @pl.when(pl.program_id(0) == 0)
    def _():
        carry[...] = jnp.zeros_like(carry)

    z = ff_ref[...] + b_ref[...]
    logf = jnp.minimum(z, 0.0) - jnp.log(1.0 + jnp.exp(-jnp.abs(z)))
    tc = z.shape[0]
    row = lax.broadcasted_iota(jnp.int32, (tc, tc), 0)
    col = lax.broadcasted_iota(jnp.int32, (tc, tc), 1)
    tri = jnp.where(row >= col, 1.0, 0.0).astype(BF16)
    hi, mid, lo = _split3(logf)
    cum = _dot(tri, hi) + _dot(tri, mid) + _dot(tri, lo) + carry[...]
    carry[...] = cum[tc - 1:tc, :]
    pieces = jnp.concatenate(_split3(cum * LOG2E), axis=1)
    o_ref[...] = _dot(pieces, place_ref[...]).astype(BF16)


def cum_gate(ff, b128, *, tc=256):
    s = ff.shape[0]
    place = jnp.asarray(_cum_place(), BF16)
    return pl.pallas_call(
        _cum_kernel,
        grid=(s // tc,),
        in_specs=[pl.BlockSpec((tc, LANES), lambda i: (i, 0)),
                  pl.BlockSpec((1, LANES), lambda i: (0, 0)),
                  pl.BlockSpec(place.shape, lambda i: (0, 0))],
        out_specs=pl.BlockSpec((tc, FOX_HEADS * LANES), lambda i: (i, 0)),
        out_shape=jax.ShapeDtypeStruct((s, FOX_HEADS * LANES), BF16),
        scratch_shapes=[pltpu.VMEM((1, LANES), F32)],
        compiler_params=_params(("arbitrary",)),
        name="cum",
    )(ff, b128, place)


def _ret_consts():
    h = np.arange(RET_HEADS, dtype=np.float32)
    log_g = np.log1p(-np.exp2(-5.0 - h)).astype(np.float32)
    i = np.arange(RET_CHUNK, dtype=np.float32)
    diff = i[:, None] - i[None, :]
    dmat = np.where(diff >= 0, np.exp(np.maximum(diff, 0.0)[None] * log_g[:, None, None]), 0.0)
    kdec = np.exp((RET_CHUNK - 1.0 - i)[None, :] * log_g[:, None])
    qdec = np.exp((i + 1.0)[None, :] * log_g[:, None])
    cdec = np.exp(RET_CHUNK * log_g)
    kdec = np.broadcast_to(kdec[:, :, None], (RET_HEADS, RET_CHUNK, RET_DK))
    qdec = np.broadcast_to(qdec[:, :, None], (RET_HEADS, RET_CHUNK, RET_DK))
    half = RET_DK // 2
    inv = (ROPE_BASE ** (-np.arange(half, dtype=np.float32) / half)).astype(np.float32)
    inv_full = np.concatenate([inv, inv])[None, :]
    sign = np.concatenate([-np.ones(half, np.float32), np.ones(half, np.float32)])[None, :]
    return (dmat.astype(np.float32), np.ascontiguousarray(kdec, np.float32),
            np.ascontiguousarray(qdec, np.float32), [float(v) for v in cdec],
            inv_full.astype(np.float32), sign)


def _ret_kernel(cdec, rq_ref, rk_ref, rv_ref, rg_ref, pos_ref, inv_ref, sign_ref,
                dmat_ref, kdec_ref, qdec_ref, gn_ref, o_ref, state):
    @pl.when(pl.program_id(0) == 0)
    def _():
        state[...] = jnp.zeros_like(state)

    ang = pos_ref[...].astype(F32) * inv_ref[...]
    cosf = jnp.cos(ang)
    sinf = jnp.sin(ang) * sign_ref[...]
    kscale = RET_DK ** -0.5
    for h in range(RET_HEADS):
        q = rq_ref[:, h * RET_DK:(h + 1) * RET_DK]
        k = rk_ref[:, h * RET_DK:(h + 1) * RET_DK]
        qr = q * cosf + pltpu.roll(q, RET_DK // 2, 1) * sinf
        kr = (k * cosf + pltpu.roll(k, RET_DK // 2, 1) * sinf) * kscale
        vb = rv_ref[:, h * RET_DV:(h + 1) * RET_DV].astype(BF16)
        scores = _dot_nt(qr.astype(BF16), kr.astype(BF16)) * dmat_ref[h]
        intra = _dot(scores.astype(BF16), vb)
        st = state[h]
        cross = _dot((qr * qdec_ref[h]).astype(BF16), st.astype(BF16))
        kd = (kr * kdec_ref[h]).astype(BF16)
        kv = lax.dot_general(kd, vb, _TN, preferred_element_type=F32)
        state[h] = st * cdec[h] + kv
        y = intra + cross
        mu = jnp.mean(y, axis=-1, keepdims=True)
        dlt = y - mu
        var = jnp.mean(dlt * dlt, axis=-1, keepdims=True)
        g = rg_ref[:, h * RET_DV:(h + 1) * RET_DV]
        yn = dlt * lax.rsqrt(var + EPS) * gn_ref[:, h * RET_DV:(h + 1) * RET_DV] * (g / (1.0 + jnp.exp(-g)))
        o_ref[:, h * RET_DV:(h + 1) * RET_DV] = yn.astype(BF16)


def retention(proj, pos_col, gn_g):
    s = proj.shape[0]
    c = RET_CHUNK
    dmat, kdec, qdec, cdec, inv_full, sign = _ret_consts()
    full3 = lambda n: (0, 0, 0)
    return pl.pallas_call(
        functools.partial(_ret_kernel, cdec),
        grid=(s // c,),
        in_specs=[pl.BlockSpec((c, RET_QK), lambda n: (n, 0)),
                  pl.BlockSpec((c, RET_QK), lambda n: (n, 1)),
                  pl.BlockSpec((c, RET_WIDTH), lambda n: (n, 1)),
                  pl.BlockSpec((c, RET_WIDTH), lambda n: (n, 2)),
                  pl.BlockSpec((c, 1), lambda n: (n, 0)),
                  pl.BlockSpec((1, RET_DK), lambda n: (0, 0)),
                  pl.BlockSpec((1, RET_DK), lambda n: (0, 0)),
                  pl.BlockSpec((RET_HEADS, c, c), full3),
                  pl.BlockSpec((RET_HEADS, c, RET_DK), full3),
                  pl.BlockSpec((RET_HEADS, c, RET_DK), full3),
                  pl.BlockSpec((1, RET_WIDTH), lambda n: (0, 0))],
        out_specs=pl.BlockSpec((c, RET_WIDTH), lambda n: (n, 0)),
        out_shape=jax.ShapeDtypeStruct((s, RET_WIDTH), BF16),
        scratch_shapes=[pltpu.VMEM((RET_HEADS, RET_DK, RET_DV), F32)],
        compiler_params=_params(("arbitrary",)),
        name="ret",
    )(proj, proj, proj, proj, pos_col, jnp.asarray(inv_full), jnp.asarray(sign),
      jnp.asarray(dmat), jnp.asarray(kdec), jnp.asarray(qdec), gn_g)


def _fox_kernel(q_ref, k_ref, c_ref, vt_ref, o_ref, m_scr, l_scr, acc_scr, *bufs):
    i = pl.program_id(1)
    bq = q_ref.shape[0]
    bk = bufs[0].shape[0]
    nd = len(bufs)
    assert nd * bk == bq
    sub = 8
    lane = lax.broadcasted_iota(jnp.int32, (bq, FOX_DH), 1)
    qa = jnp.where(lane < 3, -1.0, 0.0).astype(BF16)
    q2 = jnp.concatenate([(q_ref[...] * (FOX_DH ** -0.5 * LOG2E)).astype(BF16), qa], axis=1)
    m_scr[...] = jnp.full_like(m_scr, NEG_BIG)
    l_scr[...] = jnp.zeros_like(l_scr)
    acc_scr[...] = jnp.zeros_like(acc_scr)
    ones_rows = jnp.where(lax.broadcasted_iota(jnp.int32, (2 * sub, bk), 0) == 0, 1.0, 0.0).astype(BF16)

    def logits(j, s_scr, c0=0):
        off = pl.multiple_of(j * bk, bk)
        k2 = jnp.concatenate([k_ref[pl.ds(off, bk), :], c_ref[pl.ds(off, bk), :]], axis=1)
        s_scr[:, c0:bq] = _dot_nt(k2, q2[c0:bq, :])

    def update_cols(j, s_scr, c0, c1, diagonal):
        w = c1 - c0
        st = s_scr[:, c0:c1].reshape(bk // sub, sub, w)
        if diagonal:
            kid = lax.broadcasted_iota(jnp.int32, (bk, bk), 0).reshape(bk // sub, sub, bk)
            qid = lax.broadcasted_iota(jnp.int32, (bk, bk), 1).reshape(bk // sub, sub, bk)
            st = jnp.where(kid <= qid, st, NEG_BIG)
        m_old = m_scr[:, c0:c1]
        m_new = jnp.maximum(m_old, jnp.max(jnp.max(st, axis=0), axis=0, keepdims=True))
        alpha = jnp.exp2(m_old - m_new)
        p = jnp.exp2(st - jnp.broadcast_to(m_new, (sub, w))[None])
        pv = _dot(jnp.concatenate([vt_ref[j], ones_rows], axis=0), p.reshape(bk, w).astype(BF16))
        l_scr[:, c0:c1] = alpha * l_scr[:, c0:c1] + pv[FOX_DH:FOX_DH + 1, :]
        a8 = jnp.broadcast_to(alpha, (sub, w))[None]
        acc_scr[:, c0:c1] = (a8 * acc_scr[:, c0:c1].reshape(FOX_DH // sub, sub, w)).reshape(FOX_DH, w) + pv[0:FOX_DH, :]
        m_scr[:, c0:c1] = m_new

    def update(j, s_scr, c0=0, diagonal=False):
        if diagonal:
            update_cols(j, s_scr, c0, c0 + bk, True)
            c0 += bk
        if c0 < bq:
            update_cols(j, s_scr, c0, bq, False)

    ahead = 2
    for k in range(ahead):
        logits(k, bufs[k])

    def body(jj, carry):
        for k in range(nd):
            logits(nd * jj + k + ahead, bufs[(k + ahead) % nd])
            update(nd * jj + k, bufs[k])
        return carry

    lax.fori_loop(0, i, body, 0)
    for k in range(nd):
        if k + ahead < nd:
            logits(nd * i + k + ahead, bufs[k + ahead], (k + ahead) * bk)
        update(nd * i + k, bufs[k], k * bk, diagonal=True)
    o_ref[...] = (acc_scr[...] / l_scr[...]).T


def fox_attention(pf32, kb, caug, vt3, *, bq=2048):
    s = pf32.shape[0]
    bk = vt3.shape[2]
    assert vt3.shape == (s // bk, FOX_WIDTH, bk) and bq % (2 * bk) == 0
    qcol0 = (N_F32_COLS - FOX_WIDTH) // FOX_DH
    return pl.pallas_call(
        _fox_kernel,
        grid=(FOX_HEADS, s // bq),
        in_specs=[pl.BlockSpec((bq, FOX_DH), lambda h, i: (i, qcol0 + h)),
                  pl.BlockSpec((s, FOX_DH), lambda h, i: (0, h)),
                  pl.BlockSpec((s, LANES), lambda h, i: (0, h)),
                  pl.BlockSpec((s // bk, FOX_DH, bk), lambda h, i: (0, h, 0))],
        out_specs=pl.BlockSpec((bq, FOX_DH), lambda h, i: (i, h)),
        out_shape=jax.ShapeDtypeStruct((s, FOX_WIDTH), F32),
        scratch_shapes=[pltpu.VMEM((1, bq), F32), pltpu.VMEM((1, bq), F32), pltpu.VMEM((FOX_DH, bq), F32)]
        + [pltpu.VMEM((bk, bq), F32)] * (bq // bk),
        compiler_params=_params(("arbitrary", "arbitrary")),
        name="fox",
    )(pf32, kb, caug, vt3)


def _outp_kernel(yr_ref, yf_ref, fg_ref, w_ref, x_ref, g1_ref, o_ref, y_scr):
    @pl.when(pl.program_id(1) == 0)
    def _():
        yf = yf_ref[...]
        ms = jnp.mean(yf * yf, axis=-1, keepdims=True)
        yn = yf * lax.rsqrt(ms + EPS) * fg_ref[...]
        y_scr[:, 0:RET_WIDTH] = yr_ref[...]
        y_scr[:, RET_WIDTH:RET_WIDTH + FOX_WIDTH] = yn.astype(BF16)

    tn = w_ref.shape[1]
    piece = tn // 2
    for c0 in range(0, tn, piece):
        cs = slice(c0, c0 + piece)
        o_ref[:, cs] = x_ref[:, cs] + g1_ref[:, cs] * _dot(y_scr[...], w_ref[:, cs])


def out_proj(y_ret, y_fox, fox_g, w_out, x, gate1, *, tm=512, tn=1024):
    s, d = x.shape
    kdim = RET_WIDTH + FOX_WIDTH
    return pl.pallas_call(
        _outp_kernel,
        grid=(s // tm, d // tn),
        in_specs=[pl.BlockSpec((tm, RET_WIDTH), lambda i, j: (i, 0)),
                  pl.BlockSpec((tm, FOX_WIDTH), lambda i, j: (i, 0)),
                  pl.BlockSpec((1, FOX_WIDTH), lambda i, j: (0, 0)),
                  pl.BlockSpec((kdim, tn), lambda i, j: (0, j)),
                  pl.BlockSpec((tm, tn), lambda i, j: (i, j)),
                  pl.BlockSpec((1, tn), lambda i, j: (0, j))],
        out_specs=pl.BlockSpec((tm, tn), lambda i, j: (i, j)),
        out_shape=jax.ShapeDtypeStruct((s, d), F32),
        scratch_shapes=[pltpu.VMEM((tm, kdim), BF16)],
        compiler_params=_params(("arbitrary", "arbitrary")),
        name="outp",
    )(y_ret, y_fox, fox_g, w_out, x, gate1)


def _drop_max(cur):
    mx = jnp.max(jnp.max(cur, axis=0), axis=0, keepdims=True)
    hit = cur == jnp.broadcast_to(mx, cur.shape[1:])[None]
    return mx, hit, jnp.where(hit, -jnp.inf, cur)


def _top_rows(s, t_scr, ls):
    cur = s.reshape(s.shape[0] // 8, 8, s.shape[1])
    level = jnp.full(cur.shape, float(PEER_TOPK), F32)
    for r in range(PEER_TOPK):
        mx, hit, cur = _drop_max(cur)
        t_scr[r:r + 1, ls] = mx
        level = jnp.where(hit, float(r), level)
    return level.reshape(s.shape)


def _peerq_kernel(group, x_ref, g_ref, sc_ref, sh_ref, wq_ref, keys_ref, u_ref, v_ref,
                  h2t_ref, kc_ref, e1_ref, l2_ref, e2_ref, ub_ref, vtb_ref,
                  h_scr, s1_scr, s2_scr, t1_scr, t2_scr, cand_scr):
    step = pl.program_id(0) * pl.num_programs(1) + pl.program_id(1)

    @pl.when(step % group == 0)
    def _():
        ub_ref[...] = u_ref[...].astype(BF16)
        vtb_ref[...] = v_ref[...].T.astype(BF16)

    @pl.when(pl.program_id(1) == 0)
    def _():
        h = _norm_mod(x_ref[...], g_ref[...], sc_ref[...], sh_ref[...])
        h_scr[...] = h.astype(BF16)
        h2t_ref[...] = h.T.astype(BF16)

    half = PEER_DQ // 2
    qb = _dot(h_scr[...], wq_ref[...]).astype(BF16)
    s1 = _dot_nt(keys_ref[0, 0], qb[:, 0:half])
    s2 = _dot_nt(keys_ref[0, 1], qb[:, half:PEER_DQ])
    s1_scr[...] = s1
    s2_scr[...] = s2
    for g in range(s1.shape[1] // LANES):
        ls = slice(g * LANES, (g + 1) * LANES)
        s1l = s1_scr[:, ls]
        s2l = s2_scr[:, ls]
        _top_rows(s1l, t1_scr, ls)
        l2_ref[0, :, ls] = _top_rows(s2l, t2_scr, ls).astype(BF16)
        t1_lo = t1_scr[0:8, ls]
        for b in range(8):
            cand_scr[b * 8:(b + 1) * 8, ls] = t1_lo + t2_scr[b:b + 1, ls]
        cand_scr[64:72, ls] = t1_scr[0:1, ls] + t2_scr[8:16, ls]
        cand_scr[72:80, ls] = t1_scr[8:16, ls] + t2_scr[0:1, ls]
        cand = cand_scr[:, ls].reshape(PEER_NCAND // 8, 8, LANES)
        cur = cand
        for r in range(PEER_TOPK):
            tau, _, cur = _drop_max(cur)
        m1 = t1_scr[0:1, ls]
        m2 = t2_scr[0:1, ls]
        top8 = jnp.broadcast_to(m1 + m2, (8, LANES))[None]
        tau8 = jnp.broadcast_to(tau, (8, LANES))[None]
        z = jnp.sum(jnp.sum(jnp.where(cand >= tau8, jnp.exp(cand - top8), 0.0), axis=0), axis=0, keepdims=True)
        s1g = s1l.reshape(PEER_NKEYS // 8, 8, LANES)
        kc = jnp.zeros(s1g.shape, F32)
        for b in range(PEER_TOPK):
            t2b = jnp.broadcast_to(t2_scr[b:b + 1, ls], (8, LANES))[None]
            kc = jnp.where(s1g + t2b >= tau8, float(b + 1), kc)
        kc_ref[0, :, ls] = kc.reshape(PEER_NKEYS, LANES)
        e1_ref[0, :, ls] = jnp.exp(s1l - (m1 + jnp.log(z)))
        e2_ref[0, :, ls] = jnp.exp(s2l - m2).astype(BF16)


def peer_query(x1, g, scale, shift, wq, keys, u, v, *, tm=512):
    s, d = x1.shape
    ne = u.shape[0]
    nsteps = (s // tm) * PEER_HEADS
    per_step = ne // nsteps
    group = max(1, LANES // per_step)
    rows = per_step * group
    assert rows * (nsteps // group) == ne and rows % LANES == 0
    vec = pl.BlockSpec((1, d), lambda i, h: (0, 0))
    stat = pl.BlockSpec((1, PEER_NKEYS, tm), lambda i, h: (h, 0, i))
    table = lambda dt: jax.ShapeDtypeStruct((PEER_HEADS, PEER_NKEYS, s), dt)
    eblk = lambda i, h: (i * PEER_HEADS + h) // group
    return pl.pallas_call(
        functools.partial(_peerq_kernel, group),
        grid=(s // tm, PEER_HEADS),
        in_specs=[pl.BlockSpec((tm, d), lambda i, h: (i, 0)), vec, vec, vec,
                  pl.BlockSpec((d, PEER_DQ), lambda i, h: (0, h)),
                  pl.BlockSpec((1, 2, PEER_NKEYS, PEER_DQ // 2), lambda i, h: (h, 0, 0, 0)),
                  pl.BlockSpec((rows, d), lambda i, h: (eblk(i, h), 0)),
                  pl.BlockSpec((rows, d), lambda i, h: (eblk(i, h), 0))],
        out_specs=[pl.BlockSpec((d, tm), lambda i, h: (0, i)), stat, stat, stat, stat,
                   pl.BlockSpec((rows, d), lambda i, h: (eblk(i, h), 0)),
                   pl.BlockSpec((d, rows), lambda i, h: (0, eblk(i, h)))],
        out_shape=[jax.ShapeDtypeStruct((d, s), BF16), table(F32), table(F32), table(BF16), table(BF16),
                   jax.ShapeDtypeStruct((ne, d), BF16), jax.ShapeDtypeStruct((d, ne), BF16)],
        scratch_shapes=[pltpu.VMEM((tm, d), BF16),
                        pltpu.VMEM((PEER_NKEYS, tm), F32), pltpu.VMEM((PEER_NKEYS, tm), F32),
                        pltpu.VMEM((PEER_TOPK, tm), F32), pltpu.VMEM((PEER_TOPK, tm), F32),
                        pltpu.VMEM((PEER_NCAND, tm), F32)],
        compiler_params=_params(("arbitrary", "arbitrary")),
        name="peerq",
    )(x1, g, scale, shift, wq, keys, u, v)


_GELU_C = 0.7978845608028654


def _gelu_tanh(x):
    k1 = -2.0 * _GELU_C * LOG2E
    z2 = x * (k1 + (k1 * 0.044715) * (x * x))
    return x / (1.0 + jnp.exp2(z2))


def _peer_kernel(nblk, h2t_ref, u_ref, vt_ref, kc_ref, e1_ref, l2_ref, e2_ref, o_ref, a_scr, w_scr):
    t = pl.program_id(0)
    tm = h2t_ref.shape[1]
    tn = u_ref.shape[0]
    rows = 16
    nrow = tn // PEER_NKEYS
    cur = t % 2

    @pl.when(jnp.maximum(t - 1, 0) % nblk == 0)
    def _():
        o_ref[...] = jnp.zeros_like(o_ref)

    @pl.when(t == 0)
    def _():
        w_scr[1] = jnp.zeros(w_scr.shape[1:], BF16)

    a_scr[...] = _dot(u_ref[...], h2t_ref[...])
    o_ref[...] += _dot(vt_ref[...], w_scr[1 - cur])
    base = lax.shift_right_logical(t, 30) * rows
    for c in range(nrow):
        i1 = (t % nblk) * nrow + c
        krow = [jnp.broadcast_to(kc_ref[h, pl.ds(i1, 1), :], (rows, tm)).astype(BF16) for h in range(PEER_HEADS)]
        e1row = [jnp.broadcast_to(e1_ref[h, pl.ds(i1, 1), :], (rows, tm)).astype(BF16) for h in range(PEER_HEADS)]
        for r0 in range(0, PEER_NKEYS, rows):
            gate = None
            for h in range(PEER_HEADS):
                gv = jnp.where(l2_ref[h, r0:r0 + rows, :] < krow[h], e2_ref[h, r0:r0 + rows, :] * e1row[h],
                               jnp.zeros((), BF16))
                gate = gv if gate is None else gate + gv
            rr = c * PEER_NKEYS + r0
            a_rows = a_scr[pl.ds(pl.multiple_of(base + rr, rows), rows), :]
            w_scr[cur, rr:rr + rows, :] = _gelu_tanh(a_rows.astype(BF16)) * gate


def peer_experts(h2t, u, vt, kc, e1, l2, e2, *, tm=512, tn=512):
    d, s = h2t.shape
    ne = u.shape[0]
    nblk, ntile = ne // tn, s // tm
    once = pl.Buffered(1)
    tile_a = lambda t: jnp.minimum(t // nblk, ntile - 1)
    prev = lambda t: jnp.maximum(t - 1, 0)
    table = pl.BlockSpec((PEER_HEADS, PEER_NKEYS, tm), lambda t: (0, 0, tile_a(t)), pipeline_mode=once)
    return pl.pallas_call(
        functools.partial(_peer_kernel, nblk),
        grid=(ntile * nblk + 1,),
        in_specs=[pl.BlockSpec((d, tm), lambda t: (0, tile_a(t)), pipeline_mode=once),
                  pl.BlockSpec((tn, d), lambda t: (t % nblk, 0)),
                  pl.BlockSpec((d, tn), lambda t: (0, prev(t) % nblk)),
                  table, table, table, table],
        out_specs=pl.BlockSpec((d, tm), lambda t: (0, prev(t) // nblk)),
        out_shape=jax.ShapeDtypeStruct((d, s), F32),
        scratch_shapes=[pltpu.VMEM((tn, tm), F32), pltpu.VMEM((2, tn, tm), BF16)],
        compiler_params=_params(("arbitrary",)),
        name="peer",
    )(h2t, u, vt, kc, e1, l2, e2)


def _fin_kernel(x_ref, pt_ref, g2_ref, fg_ref, o_ref):
    x2 = x_ref[...] + g2_ref[...] * pt_ref[...].T
    ms = jnp.mean(x2 * x2, axis=-1, keepdims=True)
    o_ref[...] = x2 * lax.rsqrt(ms + EPS) * fg_ref[...]


def final_norm(x1, peer_t, gate2, final_g, *, tm=256):
    s, d = x1.shape
    vec = pl.BlockSpec((1, d), lambda i: (0, 0))
    return pl.pallas_call(
        _fin_kernel,
        grid=(s // tm,),
        in_specs=[pl.BlockSpec((tm, d), lambda i: (i, 0)),
                  pl.BlockSpec((d, tm), lambda i: (0, i)), vec, vec],
        out_specs=pl.BlockSpec((tm, d), lambda i: (i, 0)),
        out_shape=jax.ShapeDtypeStruct((s, d), F32),
        compiler_params=_params(("arbitrary",)),
        name="fin",
    )(x1, peer_t, gate2, final_g)


def _layer(x, mod, positions, norm1_g, w_in, b_forget, ret_gn_g, fox_norm_g, w_out, norm2_g,
           w_peer_q, peer_sub_keys, peer_u, peer_v, *, bq=2048, bk=512):
    s, d = x.shape
    shift1, scale1, gate1, shift2, scale2, gate2 = [mod[:, k * d:(k + 1) * d] for k in range(6)]
    row = lambda v: v.reshape(1, -1)

    w_t = w_in.T
    w_main = cast_bf16(w_t, rows=IN_MAIN, transpose=True, br=2048, bc=512, name="cast_in")
    n_ff = w_in.shape[1] - IN_MAIN
    w_ff = tail_weight(w_t, IN_MAIN, n_ff)
    pf32, kb, vt3, ff = in_proj(x, row(norm1_g), scale1, shift1, w_main, w_ff, tm=bk)

    b128 = jnp.pad(b_forget, (0, LANES - n_ff)).reshape(1, LANES)
    caug = cum_gate(ff, b128)
    y_ret = retention(pf32, positions.reshape(s, 1), row(ret_gn_g))
    y_fox = fox_attention(pf32, kb, caug, vt3, bq=bq)
    x1 = out_proj(y_ret, y_fox, row(fox_norm_g), cast_bf16(w_out, name="cast_out"), x, gate1)

    h2t, kc, e1, l2, e2, ub, vtb = peer_query(x1, row(norm2_g), scale2, shift2, cast_bf16(w_peer_q, name="cast_q"),
                                              peer_sub_keys.astype(BF16), peer_u, peer_v)
    peer_t = peer_experts(h2t, ub, vtb, kc, e1, l2, e2)
    return x1, peer_t, gate2


def kernel(x, c, positions, w_ada, b_ada, norm1_g, w_in, b_forget, ret_gn_g, fox_norm_g, w_out,
           norm2_g, w_peer_q, peer_sub_keys, peer_u, peer_v, final_g):
    b, s, d = x.shape
    assert b == 1 and w_ada.shape[0] == 1, "one sequence, one layer"
    c8 = jnp.broadcast_to(c, (8, d))
    mod = ada_mod(c8, w_ada[0], b_ada[0].reshape(1, -1))[0:1]
    x1, peer_t, gate2 = _layer(x[0], mod, positions[0], norm1_g[0], w_in[0], b_forget[0], ret_gn_g[0],
                               fox_norm_g[0], w_out[0], norm2_g[0], w_peer_q[0], peer_sub_keys[0],
                               peer_u[0], peer_v[0])
    return final_norm(x1, peer_t, gate2, final_g.reshape(1, -1))[None]
```

```python
import functools

import numpy as np
import jax
import jax.numpy as jnp
from jax import lax
from jax.experimental import pallas as pl
from jax.experimental.pallas import tpu as pltpu

F32 = jnp.float32
BF16 = jnp.bfloat16

LANES = 128
RET_HEADS = 8
RET_DK = 128
RET_DV = 256
RET_QK = RET_HEADS * RET_DK
RET_WIDTH = RET_HEADS * RET_DV
RET_CHUNK = 128
FOX_HEADS = 16
FOX_DH = 128
FOX_WIDTH = FOX_HEADS * FOX_DH
IN_MAIN = 2 * RET_QK + 2 * RET_WIDTH + 3 * FOX_WIDTH
N_F32_COLS = 2 * RET_QK + 2 * RET_WIDTH + FOX_WIDTH
ROPE_BASE = 10000.0
PEER_HEADS = 8
PEER_NKEYS = 128
PEER_NEXPERTS = PEER_NKEYS * PEER_NKEYS
PEER_DQ = 256
PEER_TOPK = 16
PEER_NCAND = 80
EPS = 1e-6
NEG_BIG = -1e30
LOG2E = 1.4426950408889634
VMEM_LIMIT = 56 * 1024 * 1024

_NT = (((1,), (1,)), ((), ()))
_TN = (((0,), (0,)), ((), ()))


def _params(sem):
    return pltpu.CompilerParams(dimension_semantics=sem, vmem_limit_bytes=VMEM_LIMIT)


def _dot(a, b):
    return jnp.dot(a, b, preferred_element_type=F32)


def _dot_nt(a, b):
    return lax.dot_general(a, b, _NT, preferred_element_type=F32)


def _ada_kernel(c_ref, w_ref, b_ref, o_ref):
    c = c_ref[...]
    ca = (c / (1.0 + jnp.exp(-c))).astype(BF16)
    o_ref[...] = _dot(ca, w_ref[...].astype(BF16)) + b_ref[...]


def ada_mod(c8, w, b, *, bn=512):
    d, n = w.shape
    return pl.pallas_call(
        _ada_kernel,
        grid=(n // bn,),
        in_specs=[pl.BlockSpec((8, d), lambda j: (0, 0)),
                  pl.BlockSpec((d, bn), lambda j: (0, j)),
                  pl.BlockSpec((1, bn), lambda j: (0, j))],
        out_specs=pl.BlockSpec((8, bn), lambda j: (0, j)),
        out_shape=jax.ShapeDtypeStruct((8, n), F32),
        compiler_params=_params(("arbitrary",)),
        name="ada",
    )(c8, w, b)


def _cast_kernel(x_ref, o_ref):
    o_ref[...] = x_ref[...].astype(BF16)


def _cast_t_kernel(x_ref, o_ref):
    o_ref[...] = x_ref[...].T.astype(BF16)


def cast_bf16(x, *, rows=None, cols=None, br=512, bc=2048, transpose=False, name="cast"):
    r, c = x.shape
    r = r if rows is None else rows
    c = c if cols is None else cols
    br, bc = min(br, r), min(bc, c)
    if transpose:
        return pl.pallas_call(
            _cast_t_kernel, grid=(r // br, c // bc),
            in_specs=[pl.BlockSpec((br, bc), lambda i, j: (i, j))],
            out_specs=pl.BlockSpec((bc, br), lambda i, j: (j, i)),
            out_shape=jax.ShapeDtypeStruct((c, r), BF16),
            compiler_params=_params(("arbitrary", "arbitrary")), name=name)(x)
    return pl.pallas_call(
        _cast_kernel, grid=(r // br, c // bc),
        in_specs=[pl.BlockSpec((br, bc), lambda i, j: (i, j))],
        out_specs=pl.BlockSpec((br, bc), lambda i, j: (i, j)),
        out_shape=jax.ShapeDtypeStruct((r, c), BF16),
        compiler_params=_params(("arbitrary", "arbitrary")), name=name)(x)


def _tail_kernel(x_ref, o_ref):
    o_ref[...] = jnp.zeros_like(o_ref)
    o_ref[:, 0:x_ref.shape[0]] = x_ref[...].T.astype(BF16)


def tail_weight(w_t, row0, nrows):
    d = w_t.shape[1]
    assert row0 % nrows == 0 and nrows % 8 == 0
    return pl.pallas_call(
        _tail_kernel, grid=(1,),
        in_specs=[pl.BlockSpec((nrows, d), lambda i: (row0 // nrows, 0))],
        out_specs=pl.BlockSpec((d, LANES), lambda i: (0, 0)),
        out_shape=jax.ShapeDtypeStruct((d, LANES), BF16),
        compiler_params=_params(("arbitrary",)), name="cast_tail")(w_t)


def _norm_mod(x, g, scale, shift):
    ms = jnp.mean(x * x, axis=-1, keepdims=True)
    y = x * lax.rsqrt(ms + EPS) * g
    return y * (1.0 + scale) + shift


def _inpj_kernel(nf, nk, x_ref, g_ref, sc_ref, sh_ref, w_ref, wf_ref, o_ref, ok_ref, ovt_ref, of_ref, h_scr):
    j = pl.program_id(1)

    @pl.when(j == 0)
    def _():
        hb = _norm_mod(x_ref[...], g_ref[...], sc_ref[...], sh_ref[...]).astype(BF16)
        h_scr[...] = hb
        of_ref[...] = _dot(hb, wf_ref[...])

    tn = w_ref.shape[1]
    piece = tn // 2

    def pieces(store):
        for c0 in range(0, tn, piece):
            store(c0, _dot(h_scr[...], w_ref[:, c0:c0 + piece]))

    @pl.when(j < nf)
    def _():
        def store(c0, r):
            o_ref[:, c0:c0 + piece] = r
        pieces(store)

    @pl.when(jnp.logical_and(j >= nf, j < nf + nk))
    def _():
        def store(c0, r):
            ok_ref[:, c0:c0 + piece] = r.astype(BF16)
        pieces(store)

    @pl.when(j >= nf + nk)
    def _():
        def store(c0, r):
            ovt_ref[0, c0:c0 + piece, :] = r.T.astype(BF16)
        pieces(store)


def in_proj(x, g, scale, shift, w_main, w_ff, *, tm=512, tn=1024):
    s, d = x.shape
    n = w_main.shape[1]
    nf, nk = N_F32_COLS // tn, FOX_WIDTH // tn
    vec = pl.BlockSpec((1, d), lambda i, j: (0, 0))
    return pl.pallas_call(
        functools.partial(_inpj_kernel, nf, nk),
        grid=(s // tm, n // tn),
        in_specs=[pl.BlockSpec((tm, d), lambda i, j: (i, 0)), vec, vec, vec,
                  pl.BlockSpec((d, tn), lambda i, j: (0, j)),
                  pl.BlockSpec((d, LANES), lambda i, j: (0, 0))],
        out_specs=[pl.BlockSpec((tm, tn), lambda i, j: (i, jnp.minimum(j, nf - 1))),
                   pl.BlockSpec((tm, tn), lambda i, j: (i, jnp.clip(j - nf, 0, nk - 1))),
                   pl.BlockSpec((1, tn, tm), lambda i, j: (i, jnp.clip(j - nf - nk, 0, nk - 1), 0)),
                   pl.BlockSpec((tm, LANES), lambda i, j: (i, 0))],
        out_shape=[jax.ShapeDtypeStruct((s, N_F32_COLS), F32),
                   jax.ShapeDtypeStruct((s, FOX_WIDTH), BF16),
                   jax.ShapeDtypeStruct((s // tm, FOX_WIDTH, tm), BF16),
                   jax.ShapeDtypeStruct((s, LANES), F32)],
        scratch_shapes=[pltpu.VMEM((tm, d), BF16)],
        compiler_params=_params(("arbitrary", "arbitrary")),
        name="inpj",
    )(x, g, scale, shift, w_main, w_ff)


def _split3(v):
    hi = v.astype(BF16)
    r1 = v - hi.astype(F32)
    mid = r1.astype(BF16)
    lo = (r1 - mid.astype(F32)).astype(BF16)
    return hi, mid, lo


def _cum_place():
    m = np.zeros((3 * LANES, FOX_HEADS * LANES), np.float32)
    for p in range(3):
        for h in range(FOX_HEADS):
            m[p * LANES + h, h * LANES + p] = 1.0
    return m


def _cum_kernel(ff_ref, b_ref, place_ref, o_ref, carry):
    @pl.when(pl.program_id(0) == 0)
    def _():
        carry[...] = jnp.zeros_like(carry)

    z = ff_ref[...] + b_ref[...]
    logf = jnp.minimum(z, 0.0) - jnp.log(1.0 + jnp.exp(-jnp.abs(z)))
    tc = z.shape[0]
    row = lax.broadcasted_iota(jnp.int32, (tc, tc), 0)
    col = lax.broadcasted_iota(jnp.int32, (tc, tc), 1)
    tri = jnp.where(row >= col, 1.0, 0.0).astype(BF16)
    hi, mid, lo = _split3(logf)
    cum = _dot(tri, hi) + _dot(tri, mid) + _dot(tri, lo) + carry[...]
    carry[...] = cum[tc - 1:tc, :]
    pieces = jnp.concatenate(_split3(cum * LOG2E), axis=1)
    o_ref[...] = _dot(pieces, place_ref[...]).astype(BF16)


def cum_gate(ff, b128, *, tc=256):
    s = ff.shape[0]
    place = jnp.asarray(_cum_place(), BF16)
    return pl.pallas_call(
        _cum_kernel,
        grid=(s // tc,),
        in_specs=[pl.BlockSpec((tc, LANES), lambda i: (i, 0)),
                  pl.BlockSpec((1, LANES), lambda i: (0, 0)),
                  pl.BlockSpec(place.shape, lambda i: (0, 0))],
        out_specs=pl.BlockSpec((tc, FOX_HEADS * LANES), lambda i: (i, 0)),
        out_shape=jax.ShapeDtypeStruct((s, FOX_HEADS * LANES), BF16),
        scratch_shapes=[pltpu.VMEM((1, LANES), F32)],
        compiler_params=_params(("arbitrary",)),
        name="cum",
    )(ff, b128, place)


def _ret_consts():
    h = np.arange(RET_HEADS, dtype=np.float32)
    log_g = np.log1p(-np.exp2(-5.0 - h)).astype(np.float32)
    i = np.arange(RET_CHUNK, dtype=np.float32)
    diff = i[:, None] - i[None, :]
    dmat = np.where(diff >= 0, np.exp(np.maximum(diff, 0.0)[None] * log_g[:, None, None]), 0.0)
    kdec = np.exp((RET_CHUNK - 1.0 - i)[None, :] * log_g[:, None])
    qdec = np.exp((i + 1.0)[None, :] * log_g[:, None])
    cdec = np.exp(RET_CHUNK * log_g)
    kdec = np.broadcast_to(kdec[:, :, None], (RET_HEADS, RET_CHUNK, RET_DK))
    qdec = np.broadcast_to(qdec[:, :, None], (RET_HEADS, RET_CHUNK, RET_DK))
    half = RET_DK // 2
    inv = (ROPE_BASE ** (-np.arange(half, dtype=np.float32) / half)).astype(np.float32)
    inv_full = np.concatenate([inv, inv])[None, :]
    sign = np.concatenate([-np.ones(half, np.float32), np.ones(half, np.float32)])[None, :]
    return (dmat.astype(np.float32), np.ascontiguousarray(kdec, np.float32),
            np.ascontiguousarray(qdec, np.float32), [float(v) for v in cdec],
            inv_full.astype(np.float32), sign)


def _ret_kernel(cdec, rq_ref, rk_ref, rv_ref, rg_ref, pos_ref, inv_ref, sign_ref,
                dmat_ref, kdec_ref, qdec_ref, gn_ref, o_ref, state):
    @pl.when(pl.program_id(0) == 0)
    def _():
        state[...] = jnp.zeros_like(state)

    ang = pos_ref[...].astype(F32) * inv_ref[...]
    cosf = jnp.cos(ang)
    sinf = jnp.sin(ang) * sign_ref[...]
    kscale = RET_DK ** -0.5
    for h in range(RET_HEADS):
        q = rq_ref[:, h * RET_DK:(h + 1) * RET_DK]
        k = rk_ref[:, h * RET_DK:(h + 1) * RET_DK]
        qr = q * cosf + pltpu.roll(q, RET_DK // 2, 1) * sinf
        kr = (k * cosf + pltpu.roll(k, RET_DK // 2, 1) * sinf) * kscale
        vb = rv_ref[:, h * RET_DV:(h + 1) * RET_DV].astype(BF16)
        scores = _dot_nt(qr.astype(BF16), kr.astype(BF16)) * dmat_ref[h]
        intra = _dot(scores.astype(BF16), vb)
        st = state[h]
        cross = _dot((qr * qdec_ref[h]).astype(BF16), st.astype(BF16))
        kd = (kr * kdec_ref[h]).astype(BF16)
        kv = lax.dot_general(kd, vb, _TN, preferred_element_type=F32)
        state[h] = st * cdec[h] + kv
        y = intra + cross
        mu = jnp.mean(y, axis=-1, keepdims=True)
        dlt = y - mu
        var = jnp.mean(dlt * dlt, axis=-1, keepdims=True)
        g = rg_ref[:, h * RET_DV:(h + 1) * RET_DV]
        yn = dlt * lax.rsqrt(var + EPS) * gn_ref[:, h * RET_DV:(h + 1) * RET_DV] * (g / (1.0 + jnp.exp(-g)))
        o_ref[:, h * RET_DV:(h + 1) * RET_DV] = yn.astype(BF16)


def retention(proj, pos_col, gn_g):
    s = proj.shape[0]
    c = RET_CHUNK
    dmat, kdec, qdec, cdec, inv_full, sign = _ret_consts()
    full3 = lambda n: (0, 0, 0)
    return pl.pallas_call(
        functools.partial(_ret_kernel, cdec),
        grid=(s // c,),
        in_specs=[pl.BlockSpec((c, RET_QK), lambda n: (n, 0)),
                  pl.BlockSpec((c, RET_QK), lambda n: (n, 1)),
                  pl.BlockSpec((c, RET_WIDTH), lambda n: (n, 1)),
                  pl.BlockSpec((c, RET_WIDTH), lambda n: (n, 2)),
                  pl.BlockSpec((c, 1), lambda n: (n, 0)),
                  pl.BlockSpec((1, RET_DK), lambda n: (0, 0)),
                  pl.BlockSpec((1, RET_DK), lambda n: (0, 0)),
                  pl.BlockSpec((RET_HEADS, c, c), full3),
                  pl.BlockSpec((RET_HEADS, c, RET_DK), full3),
                  pl.BlockSpec((RET_HEADS, c, RET_DK), full3),
                  pl.BlockSpec((1, RET_WIDTH), lambda n: (0, 0))],
        out_specs=pl.BlockSpec((c, RET_WIDTH), lambda n: (n, 0)),
        out_shape=jax.ShapeDtypeStruct((s, RET_WIDTH), BF16),
        scratch_shapes=[pltpu.VMEM((RET_HEADS, RET_DK, RET_DV), F32)],
        compiler_params=_params(("arbitrary",)),
        name="ret",
    )(proj, proj, proj, proj, pos_col, jnp.asarray(inv_full), jnp.asarray(sign),
      jnp.asarray(dmat), jnp.asarray(kdec), jnp.asarray(qdec), gn_g)


def _fox_kernel(q_ref, k_ref, c_ref, vt_ref, o_ref, m_scr, l_scr, acc_scr, *bufs):
    i = pl.program_id(1)
    bq = q_ref.shape[0]
    bk = bufs[0].shape[0]
    nd = len(bufs)
    assert nd * bk == bq
    sub = 8
    lane = lax.broadcasted_iota(jnp.int32, (bq, FOX_DH), 1)
    qa = jnp.where(lane < 3, -1.0, 0.0).astype(BF16)
    q2 = jnp.concatenate([(q_ref[...] * (FOX_DH ** -0.5 * LOG2E)).astype(BF16), qa], axis=1)
    m_scr[...] = jnp.full_like(m_scr, NEG_BIG)
    l_scr[...] = jnp.zeros_like(l_scr)
    acc_scr[...] = jnp.zeros_like(acc_scr)
    ones_rows = jnp.where(lax.broadcasted_iota(jnp.int32, (2 * sub, bk), 0) == 0, 1.0, 0.0).astype(BF16)

    def logits(j, s_scr, c0=0):
        off = pl.multiple_of(j * bk, bk)
        k2 = jnp.concatenate([k_ref[pl.ds(off, bk), :], c_ref[pl.ds(off, bk), :]], axis=1)
        s_scr[:, c0:bq] = _dot_nt(k2, q2[c0:bq, :])

    def update_cols(j, s_scr, c0, c1, diagonal):
        w = c1 - c0
        st = s_scr[:, c0:c1].reshape(bk // sub, sub, w)
        if diagonal:
            kid = lax.broadcasted_iota(jnp.int32, (bk, bk), 0).reshape(bk // sub, sub, bk)
            qid = lax.broadcasted_iota(jnp.int32, (bk, bk), 1).reshape(bk // sub, sub, bk)
            st = jnp.where(kid <= qid, st, NEG_BIG)
        m_old = m_scr[:, c0:c1]
        m_new = jnp.maximum(m_old, jnp.max(jnp.max(st, axis=0), axis=0, keepdims=True))
        alpha = jnp.exp2(m_old - m_new)
        p = jnp.exp2(st - jnp.broadcast_to(m_new, (sub, w))[None])
        pv = _dot(jnp.concatenate([vt_ref[j], ones_rows], axis=0), p.reshape(bk, w).astype(BF16))
        l_scr[:, c0:c1] = alpha * l_scr[:, c0:c1] + pv[FOX_DH:FOX_DH + 1, :]
        a8 = jnp.broadcast_to(alpha, (sub, w))[None]
        acc_scr[:, c0:c1] = (a8 * acc_scr[:, c0:c1].reshape(FOX_DH // sub, sub, w)).reshape(FOX_DH, w) + pv[0:FOX_DH, :]
        m_scr[:, c0:c1] = m_new

    def update(j, s_scr, c0=0, diagonal=False):
        if diagonal:
            update_cols(j, s_scr, c0, c0 + bk, True)
            c0 += bk
        if c0 < bq:
            update_cols(j, s_scr, c0, bq, False)

    ahead = 2
    for k in range(ahead):
        logits(k, bufs[k])

    def body(jj, carry):
        for k in range(nd):
            logits(nd * jj + k + ahead, bufs[(k + ahead) % nd])
            update(nd * jj + k, bufs[k])
        return carry

    lax.fori_loop(0, i, body, 0)
    for k in range(nd):
        if k + ahead < nd:
            logits(nd * i + k + ahead, bufs[k + ahead], (k + ahead) * bk)
        update(nd * i + k, bufs[k], k * bk, diagonal=True)
    o_ref[...] = (acc_scr[...] / l_scr[...]).T


def fox_attention(pf32, kb, caug, vt3, *, bq=2048):
    s = pf32.shape[0]
    bk = vt3.shape[2]
    assert vt3.shape == (s // bk, FOX_WIDTH, bk) and bq % (2 * bk) == 0
    qcol0 = (N_F32_COLS - FOX_WIDTH) // FOX_DH
    return pl.pallas_call(
        _fox_kernel,
        grid=(FOX_HEADS, s // bq),
        in_specs=[pl.BlockSpec((bq, FOX_DH), lambda h, i: (i, qcol0 + h)),
                  pl.BlockSpec((s, FOX_DH), lambda h, i: (0, h)),
                  pl.BlockSpec((s, LANES), lambda h, i: (0, h)),
                  pl.BlockSpec((s // bk, FOX_DH, bk), lambda h, i: (0, h, 0))],
        out_specs=pl.BlockSpec((bq, FOX_DH), lambda h, i: (i, h)),
        out_shape=jax.ShapeDtypeStruct((s, FOX_WIDTH), F32),
        scratch_shapes=[pltpu.VMEM((1, bq), F32), pltpu.VMEM((1, bq), F32), pltpu.VMEM((FOX_DH, bq), F32)]
        + [pltpu.VMEM((bk, bq), F32)] * (bq // bk),
        compiler_params=_params(("arbitrary", "arbitrary")),
        name="fox",
    )(pf32, kb, caug, vt3)


def _outp_kernel(yr_ref, yf_ref, fg_ref, w_ref, x_ref, g1_ref, o_ref, y_scr):
    @pl.when(pl.program_id(1) == 0)
    def _():
        yf = yf_ref[...]
        ms = jnp.mean(yf * yf, axis=-1, keepdims=True)
        yn = yf * lax.rsqrt(ms + EPS) * fg_ref[...]
        y_scr[:, 0:RET_WIDTH] = yr_ref[...]
        y_scr[:, RET_WIDTH:RET_WIDTH + FOX_WIDTH] = yn.astype(BF16)

    tn = w_ref.shape[1]
    piece = tn // 2
    for c0 in range(0, tn, piece):
        cs = slice(c0, c0 + piece)
        o_ref[:, cs] = x_ref[:, cs] + g1_ref[:, cs] * _dot(y_scr[...], w_ref[:, cs])


def out_proj(y_ret, y_fox, fox_g, w_out, x, gate1, *, tm=512, tn=1024):
    s, d = x.shape
    kdim = RET_WIDTH + FOX_WIDTH
    return pl.pallas_call(
        _outp_kernel,
        grid=(s // tm, d // tn),
        in_specs=[pl.BlockSpec((tm, RET_WIDTH), lambda i, j: (i, 0)),
                  pl.BlockSpec((tm, FOX_WIDTH), lambda i, j: (i, 0)),
                  pl.BlockSpec((1, FOX_WIDTH), lambda i, j: (0, 0)),
                  pl.BlockSpec((kdim, tn), lambda i, j: (0, j)),
                  pl.BlockSpec((tm, tn), lambda i, j: (i, j)),
                  pl.BlockSpec((1, tn), lambda i, j: (0, j))],
        out_specs=pl.BlockSpec((tm, tn), lambda i, j: (i, j)),
        out_shape=jax.ShapeDtypeStruct((s, d), F32),
        scratch_shapes=[pltpu.VMEM((tm, kdim), BF16)],
        compiler_params=_params(("arbitrary", "arbitrary")),
        name="outp",
    )(y_ret, y_fox, fox_g, w_out, x, gate1)


def _drop_max(cur):
    mx = jnp.max(jnp.max(cur, axis=0), axis=0, keepdims=True)
    hit = cur == jnp.broadcast_to(mx, cur.shape[1:])[None]
    return mx, hit, jnp.where(hit, -jnp.inf, cur)


def _top_rows(s, t_scr, ls):
    cur = s.reshape(s.shape[0] // 8, 8, s.shape[1])
    level = jnp.full(cur.shape, float(PEER_TOPK), F32)
    for r in range(PEER_TOPK):
        mx, hit, cur = _drop_max(cur)
        t_scr[r:r + 1, ls] = mx
        level = jnp.where(hit, float(r), level)
    return level.reshape(s.shape)


def _peerq_kernel(group, x_ref, g_ref, sc_ref, sh_ref, wq0_ref, wqn_ref, keys_ref, u_ref, v_ref,
                  h2t_ref, kc_ref, e1_ref, l2_ref, e2_ref, ub_ref, vtb_ref,
                  h_scr, qb_scr, s1_scr, s2_scr, t1_scr, t2_scr, cand_scr):
    head = pl.program_id(1)
    step = pl.program_id(0) * pl.num_programs(1) + head
    cur = head % 2

    @pl.when(step % group == 0)
    def _():
        ub_ref[...] = u_ref[...].astype(BF16)
        vtb_ref[...] = v_ref[...].T.astype(BF16)

    @pl.when(pl.program_id(1) == 0)
    def _():
        h = _norm_mod(x_ref[...], g_ref[...], sc_ref[...], sh_ref[...])
        h_scr[...] = h.astype(BF16)
        h2t_ref[...] = h.T.astype(BF16)
        qb_scr[0] = _dot(h_scr[...], wq0_ref[...]).astype(BF16)

    half = PEER_DQ // 2
    qb = qb_scr[cur]
    s1 = _dot_nt(keys_ref[0, 0], qb[:, 0:half])
    s2 = _dot_nt(keys_ref[0, 1], qb[:, half:PEER_DQ])
    s1_scr[...] = s1
    s2_scr[...] = s2
    qb_scr[1 - cur] = _dot(h_scr[...], wqn_ref[...]).astype(BF16)
    for g in range(s1.shape[1] // LANES):
        ls = slice(g * LANES, (g + 1) * LANES)
        s1l = s1_scr[:, ls]
        s2l = s2_scr[:, ls]
        _top_rows(s1l, t1_scr, ls)
        l2_ref[0, :, ls] = _top_rows(s2l, t2_scr, ls).astype(BF16)
        t1_lo = t1_scr[0:8, ls]
        for b in range(8):
            cand_scr[b * 8:(b + 1) * 8, ls] = t1_lo + t2_scr[b:b + 1, ls]
        cand_scr[64:72, ls] = t1_scr[0:1, ls] + t2_scr[8:16, ls]
        cand_scr[72:80, ls] = t1_scr[8:16, ls] + t2_scr[0:1, ls]
        cand = cand_scr[:, ls].reshape(PEER_NCAND // 8, 8, LANES)
        cur = cand
        for r in range(PEER_TOPK):
            tau, _, cur = _drop_max(cur)
        m1 = t1_scr[0:1, ls]
        m2 = t2_scr[0:1, ls]
        top8 = jnp.broadcast_to(m1 + m2, (8, LANES))[None]
        tau8 = jnp.broadcast_to(tau, (8, LANES))[None]
        z = jnp.sum(jnp.sum(jnp.where(cand >= tau8, jnp.exp(cand - top8), 0.0), axis=0), axis=0, keepdims=True)
        s1g = s1l.reshape(PEER_NKEYS // 8, 8, LANES)
        kc = jnp.zeros(s1g.shape, F32)
        for b in range(PEER_TOPK):
            t2b = jnp.broadcast_to(t2_scr[b:b + 1, ls], (8, LANES))[None]
            kc = jnp.where(s1g + t2b >= tau8, float(b + 1), kc)
        kc_ref[0, :, ls] = kc.reshape(PEER_NKEYS, LANES)
        e1_ref[0, :, ls] = jnp.exp(s1l - (m1 + jnp.log(z)))
        e2_ref[0, :, ls] = jnp.exp(s2l - m2).astype(BF16)


def peer_query(x1, g, scale, shift, wq, keys, u, v, *, tm=512):
    s, d = x1.shape
    ne = u.shape[0]
    nsteps = (s // tm) * PEER_HEADS
    per_step = ne // nsteps
    group = max(1, LANES // per_step)
    rows = per_step * group
    assert rows * (nsteps // group) == ne and rows % LANES == 0
    vec = pl.BlockSpec((1, d), lambda i, h: (0, 0))
    stat = pl.BlockSpec((1, PEER_NKEYS, tm), lambda i, h: (h, 0, i))
    table = lambda dt: jax.ShapeDtypeStruct((PEER_HEADS, PEER_NKEYS, s), dt)
    eblk = lambda i, h: (i * PEER_HEADS + h) // group
    return pl.pallas_call(
        functools.partial(_peerq_kernel, group),
        grid=(s // tm, PEER_HEADS),
        in_specs=[pl.BlockSpec((tm, d), lambda i, h: (i, 0)), vec, vec, vec,
                  pl.BlockSpec((d, PEER_DQ), lambda i, h: (0, 0)),
                  pl.BlockSpec((d, PEER_DQ), lambda i, h: (0, jnp.minimum(h + 1, PEER_HEADS - 1))),
                  pl.BlockSpec((1, 2, PEER_NKEYS, PEER_DQ // 2), lambda i, h: (h, 0, 0, 0)),
                  pl.BlockSpec((rows, d), lambda i, h: (eblk(i, h), 0)),
                  pl.BlockSpec((rows, d), lambda i, h: (eblk(i, h), 0))],
        out_specs=[pl.BlockSpec((d, tm), lambda i, h: (0, i)), stat, stat, stat, stat,
                   pl.BlockSpec((rows, d), lambda i, h: (eblk(i, h), 0)),
                   pl.BlockSpec((d, rows), lambda i, h: (0, eblk(i, h)))],
        out_shape=[jax.ShapeDtypeStruct((d, s), BF16), table(F32), table(F32), table(BF16), table(BF16),
                   jax.ShapeDtypeStruct((ne, d), BF16), jax.ShapeDtypeStruct((d, ne), BF16)],
        scratch_shapes=[pltpu.VMEM((tm, d), BF16), pltpu.VMEM((2, tm, PEER_DQ), BF16),
                        pltpu.VMEM((PEER_NKEYS, tm), F32), pltpu.VMEM((PEER_NKEYS, tm), F32),
                        pltpu.VMEM((PEER_TOPK, tm), F32), pltpu.VMEM((PEER_TOPK, tm), F32),
                        pltpu.VMEM((PEER_NCAND, tm), F32)],
        compiler_params=_params(("arbitrary", "arbitrary")),
        name="peerq",
    )(x1, g, scale, shift, wq, wq, keys, u, v)


_GELU_C = 0.7978845608028654


def _gelu_tanh(x):
    k1 = -2.0 * _GELU_C * LOG2E
    z2 = x * (k1 + (k1 * 0.044715) * (x * x))
    return x / (1.0 + jnp.exp2(z2))


def _peer_kernel(nblk, h2t_ref, u_ref, vt_ref, kc_ref, e1_ref, l2_ref, e2_ref, o_ref, a_scr, w_scr):
    t = pl.program_id(0)
    tm = h2t_ref.shape[1]
    tn = u_ref.shape[0]
    rows = 16
    nrow = tn // PEER_NKEYS
    cur = t % 2

    @pl.when(jnp.maximum(t - 1, 0) % nblk == 0)
    def _():
        o_ref[...] = jnp.zeros_like(o_ref)

    @pl.when(t == 0)
    def _():
        w_scr[1] = jnp.zeros(w_scr.shape[1:], BF16)

    a_scr[...] = _dot(u_ref[...], h2t_ref[...])
    o_ref[...] += _dot(vt_ref[...], w_scr[1 - cur])
    base = lax.shift_right_logical(t, 30) * rows
    for c in range(nrow):
        i1 = (t % nblk) * nrow + c
        krow = [jnp.broadcast_to(kc_ref[h, pl.ds(i1, 1), :], (rows, tm)).astype(BF16) for h in range(PEER_HEADS)]
        e1row = [jnp.broadcast_to(e1_ref[h, pl.ds(i1, 1), :], (rows, tm)).astype(BF16) for h in range(PEER_HEADS)]
        for r0 in range(0, PEER_NKEYS, rows):
            gate = None
            for h in range(PEER_HEADS):
                gv = jnp.where(l2_ref[h, r0:r0 + rows, :] < krow[h], e2_ref[h, r0:r0 + rows, :] * e1row[h],
                               jnp.zeros((), BF16))
                gate = gv if gate is None else gate + gv
            rr = c * PEER_NKEYS + r0
            a_rows = a_scr[pl.ds(pl.multiple_of(base + rr, rows), rows), :]
            w_scr[cur, rr:rr + rows, :] = _gelu_tanh(a_rows.astype(BF16)) * gate


def peer_experts(h2t, u, vt, kc, e1, l2, e2, *, tm=512, tn=512):
    d, s = h2t.shape
    ne = u.shape[0]
    nblk, ntile = ne // tn, s // tm
    once = pl.Buffered(1)
    tile_a = lambda t: jnp.minimum(t // nblk, ntile - 1)
    prev = lambda t: jnp.maximum(t - 1, 0)
    table = pl.BlockSpec((PEER_HEADS, PEER_NKEYS, tm), lambda t: (0, 0, tile_a(t)), pipeline_mode=once)
    return pl.pallas_call(
        functools.partial(_peer_kernel, nblk),
        grid=(ntile * nblk + 1,),
        in_specs=[pl.BlockSpec((d, tm), lambda t: (0, tile_a(t)), pipeline_mode=once),
                  pl.BlockSpec((tn, d), lambda t: (t % nblk, 0)),
                  pl.BlockSpec((d, tn), lambda t: (0, prev(t) % nblk)),
                  table, table, table, table],
        out_specs=pl.BlockSpec((d, tm), lambda t: (0, prev(t) // nblk)),
        out_shape=jax.ShapeDtypeStruct((d, s), F32),
        scratch_shapes=[pltpu.VMEM((tn, tm), F32), pltpu.VMEM((2, tn, tm), BF16)],
        compiler_params=_params(("arbitrary",)),
        name="peer",
    )(h2t, u, vt, kc, e1, l2, e2)


def _fin_kernel(x_ref, pt_ref, g2_ref, fg_ref, o_ref):
    x2 = x_ref[...] + g2_ref[...] * pt_ref[...].T
    ms = jnp.mean(x2 * x2, axis=-1, keepdims=True)
    o_ref[...] = x2 * lax.rsqrt(ms + EPS) * fg_ref[...]


def final_norm(x1, peer_t, gate2, final_g, *, tm=256):
    s, d = x1.shape
    vec = pl.BlockSpec((1, d), lambda i: (0, 0))
    return pl.pallas_call(
        _fin_kernel,
        grid=(s // tm,),
        in_specs=[pl.BlockSpec((tm, d), lambda i: (i, 0)),
                  pl.BlockSpec((d, tm), lambda i: (0, i)), vec, vec],
        out_specs=pl.BlockSpec((tm, d), lambda i: (i, 0)),
        out_shape=jax.ShapeDtypeStruct((s, d), F32),
        compiler_params=_params(("arbitrary",)),
        name="fin",
    )(x1, peer_t, gate2, final_g)


def _layer(x, mod, positions, norm1_g, w_in, b_forget, ret_gn_g, fox_norm_g, w_out, norm2_g,
           w_peer_q, peer_sub_keys, peer_u, peer_v, *, bq=2048, bk=512):
    s, d = x.shape
    shift1, scale1, gate1, shift2, scale2, gate2 = [mod[:, k * d:(k + 1) * d] for k in range(6)]
    row = lambda v: v.reshape(1, -1)

    w_t = w_in.T
    w_main = cast_bf16(w_t, rows=IN_MAIN, transpose=True, br=2048, bc=512, name="cast_in")
    n_ff = w_in.shape[1] - IN_MAIN
    w_ff = tail_weight(w_t, IN_MAIN, n_ff)
    pf32, kb, vt3, ff = in_proj(x, row(norm1_g), scale1, shift1, w_main, w_ff, tm=bk)

    b128 = jnp.pad(b_forget, (0, LANES - n_ff)).reshape(1, LANES)
    caug = cum_gate(ff, b128)
    y_ret = retention(pf32, positions.reshape(s, 1), row(ret_gn_g))
    y_fox = fox_attention(pf32, kb, caug, vt3, bq=bq)
    x1 = out_proj(y_ret, y_fox, row(fox_norm_g), cast_bf16(w_out, name="cast_out"), x, gate1)

    h2t, kc, e1, l2, e2, ub, vtb = peer_query(x1, row(norm2_g), scale2, shift2, cast_bf16(w_peer_q, name="cast_q"),
                                              peer_sub_keys.astype(BF16), peer_u, peer_v)
    peer_t = peer_experts(h2t, ub, vtb, kc, e1, l2, e2)
    return x1, peer_t, gate2


def kernel(x, c, positions, w_ada, b_ada, norm1_g, w_in, b_forget, ret_gn_g, fox_norm_g, w_out,
           norm2_g, w_peer_q, peer_sub_keys, peer_u, peer_v, final_g):
    b, s, d = x.shape
    assert b == 1 and w_ada.shape[0] == 1, "one sequence, one layer"
    c8 = jnp.broadcast_to(c, (8, d))
    mod = ada_mod(c8, w_ada[0], b_ada[0].reshape(1, -1))[0:1]
    x1, peer_t, gate2 = _layer(x[0], mod, positions[0], norm1_g[0], w_in[0], b_forget[0], ret_gn_g[0],
                               fox_norm_g[0], w_out[0], norm2_g[0], w_peer_q[0], peer_sub_keys[0],
                               peer_u[0], peer_v[0])
    return final_norm(x1, peer_t, gate2, final_g.reshape(1, -1))[None]
```

```python
import functools

import numpy as np
import jax
import jax.numpy as jnp
from jax import lax
from jax.experimental import pallas as pl
from jax.experimental.pallas import tpu as pltpu

F32 = jnp.float32
BF16 = jnp.bfloat16

LANES = 128
RET_HEADS = 8
RET_DK = 128
RET_DV = 256
RET_QK = RET_HEADS * RET_DK
RET_WIDTH = RET_HEADS * RET_DV
RET_CHUNK = 128
FOX_HEADS = 16
FOX_DH = 128
FOX_WIDTH = FOX_HEADS * FOX_DH
IN_MAIN = 2 * RET_QK + 2 * RET_WIDTH + 3 * FOX_WIDTH
N_F32_COLS = 2 * RET_QK + 2 * RET_WIDTH + FOX_WIDTH
ROPE_BASE = 10000.0
PEER_HEADS = 8
PEER_NKEYS = 128
PEER_DQ = 256
PEER_TOPK = 16
PEER_NCAND = 80
EPS = 1e-6
NEG_BIG = -1e30
LOG2E = 1.4426950408889634
VMEM_LIMIT = 56 * 1024 * 1024

_NT = (((1,), (1,)), ((), ()))
_TN = (((0,), (0,)), ((), ()))


def _params(sem):
    return pltpu.CompilerParams(dimension_semantics=sem, vmem_limit_bytes=VMEM_LIMIT)


def _dot(a, b):
    return jnp.dot(a, b, preferred_element_type=F32)


def _dot_nt(a, b):
    return lax.dot_general(a, b, _NT, preferred_element_type=F32)


def _ada_kernel(c_ref, w_ref, b_ref, o_ref):
    c = c_ref[...]
    ca = (c / (1.0 + jnp.exp(-c))).astype(BF16)
    o_ref[...] = _dot(ca, w_ref[...].astype(BF16)) + b_ref[...]


def ada_mod(c8, w, b, *, bn=512):
    d, n = w.shape
    return pl.pallas_call(
        _ada_kernel,
        grid=(n // bn,),
        in_specs=[pl.BlockSpec((8, d), lambda j: (0, 0)),
                  pl.BlockSpec((d, bn), lambda j: (0, j)),
                  pl.BlockSpec((1, bn), lambda j: (0, j))],
        out_specs=pl.BlockSpec((8, bn), lambda j: (0, j)),
        out_shape=jax.ShapeDtypeStruct((8, n), F32),
        compiler_params=_params(("arbitrary",)),
        name="ada",
    )(c8, w, b)


def _cast_kernel(x_ref, o_ref):
    o_ref[...] = x_ref[...].astype(BF16)


def _cast_t_kernel(x_ref, o_ref):
    o_ref[...] = x_ref[...].T.astype(BF16)


def cast_bf16(x, *, rows=None, cols=None, br=512, bc=2048, transpose=False, name="cast"):
    r, c = x.shape
    r = r if rows is None else rows
    c = c if cols is None else cols
    br, bc = min(br, r), min(bc, c)
    if transpose:
        return pl.pallas_call(
            _cast_t_kernel, grid=(r // br, c // bc),
            in_specs=[pl.BlockSpec((br, bc), lambda i, j: (i, j))],
            out_specs=pl.BlockSpec((bc, br), lambda i, j: (j, i)),
            out_shape=jax.ShapeDtypeStruct((c, r), BF16),
            compiler_params=_params(("arbitrary", "arbitrary")), name=name)(x)
    return pl.pallas_call(
        _cast_kernel, grid=(r // br, c // bc),
        in_specs=[pl.BlockSpec((br, bc), lambda i, j: (i, j))],
        out_specs=pl.BlockSpec((br, bc), lambda i, j: (i, j)),
        out_shape=jax.ShapeDtypeStruct((r, c), BF16),
        compiler_params=_params(("arbitrary", "arbitrary")), name=name)(x)


def _tail_kernel(x_ref, o_ref):
    o_ref[...] = jnp.zeros_like(o_ref)
    o_ref[:, 0:x_ref.shape[0]] = x_ref[...].T.astype(BF16)


def tail_weight(w_t, row0, nrows):
    d = w_t.shape[1]
    assert row0 % nrows == 0 and nrows % 8 == 0
    return pl.pallas_call(
        _tail_kernel, grid=(1,),
        in_specs=[pl.BlockSpec((nrows, d), lambda i: (row0 // nrows, 0))],
        out_specs=pl.BlockSpec((d, LANES), lambda i: (0, 0)),
        out_shape=jax.ShapeDtypeStruct((d, LANES), BF16),
        compiler_params=_params(("arbitrary",)), name="cast_tail")(w_t)


def _norm_mod(x, g, scale, shift):
    ms = jnp.mean(x * x, axis=-1, keepdims=True)
    y = x * lax.rsqrt(ms + EPS) * g
    return y * (1.0 + scale) + shift


def _inpj_kernel(nf, nk, x_ref, g_ref, sc_ref, sh_ref, w_ref, wf_ref, o_ref, ok_ref, ovt_ref, of_ref, h_scr):
    j = pl.program_id(1)

    @pl.when(j == 0)
    def _():
        hb = _norm_mod(x_ref[...], g_ref[...], sc_ref[...], sh_ref[...]).astype(BF16)
        h_scr[...] = hb
        of_ref[...] = _dot(hb, wf_ref[...])

    tn = w_ref.shape[1]
    piece = tn // 2

    def pieces(store):
        for c0 in range(0, tn, piece):
            store(c0, _dot(h_scr[...], w_ref[:, c0:c0 + piece]))

    @pl.when(j < nf)
    def _():
        def store(c0, r):
            o_ref[:, c0:c0 + piece] = r
        pieces(store)

    @pl.when(jnp.logical_and(j >= nf, j < nf + nk))
    def _():
        def store(c0, r):
            ok_ref[:, c0:c0 + piece] = r.astype(BF16)
        pieces(store)

    @pl.when(j >= nf + nk)
    def _():
        def store(c0, r):
            ovt_ref[0, c0:c0 + piece, :] = r.T.astype(BF16)
        pieces(store)


def in_proj(x, g, scale, shift, w_main, w_ff, *, tm=512, tn=1024):
    s, d = x.shape
    n = w_main.shape[1]
    nf, nk = N_F32_COLS // tn, FOX_WIDTH // tn
    vec = pl.BlockSpec((1, d), lambda i, j: (0, 0))
    return pl.pallas_call(
        functools.partial(_inpj_kernel, nf, nk),
        grid=(s // tm, n // tn),
        in_specs=[pl.BlockSpec((tm, d), lambda i, j: (i, 0)), vec, vec, vec,
                  pl.BlockSpec((d, tn), lambda i, j: (0, j)),
                  pl.BlockSpec((d, LANES), lambda i, j: (0, 0))],
        out_specs=[pl.BlockSpec((tm, tn), lambda i, j: (i, jnp.minimum(j, nf - 1))),
                   pl.BlockSpec((tm, tn), lambda i, j: (i, jnp.clip(j - nf, 0, nk - 1))),
                   pl.BlockSpec((1, tn, tm), lambda i, j: (i, jnp.clip(j - nf - nk, 0, nk - 1), 0)),
                   pl.BlockSpec((tm, LANES), lambda i, j: (i, 0))],
        out_shape=[jax.ShapeDtypeStruct((s, N_F32_COLS), F32),
                   jax.ShapeDtypeStruct((s, FOX_WIDTH), BF16),
                   jax.ShapeDtypeStruct((s // tm, FOX_WIDTH, tm), BF16),
                   jax.ShapeDtypeStruct((s, LANES), F32)],
        scratch_shapes=[pltpu.VMEM((tm, d), BF16)],
        compiler_params=_params(("arbitrary", "arbitrary")),
        name="inpj",
    )(x, g, scale, shift, w_main, w_ff)


def _split3(v):
    hi = v.astype(BF16)
    r1 = v - hi.astype(F32)
    mid = r1.astype(BF16)
    lo = (r1 - mid.astype(F32)).astype(BF16)
    return hi, mid, lo


def _cum_place():
    m = np.zeros((3 * LANES, FOX_HEADS * LANES), np.float32)
    for p in range(3):
        for h in range(FOX_HEADS):
            m[p * LANES + h, h * LANES + p] = 1.0
    return m


def _cum_kernel(ff_ref, b_ref, place_ref, o_ref, carry):
    @pl.when(pl.program_id(0) == 0)
    def _():
        carry[...] = jnp.zeros_like(carry)

    z = ff_ref[...] + b_ref[...]
    logf = jnp.minimum(z, 0.0) - jnp.log(1.0 + jnp.exp(-jnp.abs(z)))
    tc = z.shape[0]
    row = lax.broadcasted_iota(jnp.int32, (tc, tc), 0)
    col = lax.broadcasted_iota(jnp.int32, (tc, tc), 1)
    tri = jnp.where(row >= col, 1.0, 0.0).astype(BF16)
    hi, mid, lo = _split3(logf)
    cum = _dot(tri, hi) + _dot(tri, mid) + _dot(tri, lo) + carry[...]
    carry[...] = cum[tc - 1:tc, :]
    pieces = jnp.concatenate(_split3(cum * LOG2E), axis=1)
    o_ref[...] = _dot(pieces, place_ref[...]).astype(BF16)


def cum_gate(ff, b128, *, tc=256):
    s = ff.shape[0]
    place = jnp.asarray(_cum_place(), BF16)
    return pl.pallas_call(
        _cum_kernel,
        grid=(s // tc,),
        in_specs=[pl.BlockSpec((tc, LANES), lambda i: (i, 0)),
                  pl.BlockSpec((1, LANES), lambda i: (0, 0)),
                  pl.BlockSpec(place.shape, lambda i: (0, 0))],
        out_specs=pl.BlockSpec((tc, FOX_HEADS * LANES), lambda i: (i, 0)),
        out_shape=jax.ShapeDtypeStruct((s, FOX_HEADS * LANES), BF16),
        scratch_shapes=[pltpu.VMEM((1, LANES), F32)],
        compiler_params=_params(("arbitrary",)),
        name="cum",
    )(ff, b128, place)


def _ret_consts():
    h = np.arange(RET_HEADS, dtype=np.float32)
    log_g = np.log1p(-np.exp2(-5.0 - h)).astype(np.float32)
    i = np.arange(RET_CHUNK, dtype=np.float32)
    diff = i[:, None] - i[None, :]
    dmat = np.where(diff >= 0, np.exp(np.maximum(diff, 0.0)[None] * log_g[:, None, None]), 0.0)
    kdec = np.exp((RET_CHUNK - 1.0 - i)[None, :] * log_g[:, None])
    qdec = np.exp((i + 1.0)[None, :] * log_g[:, None])
    cdec = np.exp(RET_CHUNK * log_g)
    kdec = np.broadcast_to(kdec[:, :, None], (RET_HEADS, RET_CHUNK, RET_DK))
    qdec = np.broadcast_to(qdec[:, :, None], (RET_HEADS, RET_CHUNK, RET_DK))
    half = RET_DK // 2
    inv = (ROPE_BASE ** (-np.arange(half, dtype=np.float32) / half)).astype(np.float32)
    inv_full = np.concatenate([inv, inv])[None, :]
    sign = np.concatenate([-np.ones(half, np.float32), np.ones(half, np.float32)])[None, :]
    return (dmat.astype(np.float32), np.ascontiguousarray(kdec, np.float32),
            np.ascontiguousarray(qdec, np.float32), [float(v) for v in cdec],
            inv_full.astype(np.float32), sign)


def _ret_kernel(cdec, rq_ref, rk_ref, rv_ref, rg_ref, pos_ref, inv_ref, sign_ref,
                dmat_ref, kdec_ref, qdec_ref, gn_ref, o_ref, state):
    @pl.when(pl.program_id(0) == 0)
    def _():
        state[...] = jnp.zeros_like(state)

    ang = pos_ref[...].astype(F32) * inv_ref[...]
    cosf = jnp.cos(ang)
    sinf = jnp.sin(ang) * sign_ref[...]
    kscale = RET_DK ** -0.5
    for h in range(RET_HEADS):
        q = rq_ref[:, h * RET_DK:(h + 1) * RET_DK]
        k = rk_ref[:, h * RET_DK:(h + 1) * RET_DK]
        qr = q * cosf + pltpu.roll(q, RET_DK // 2, 1) * sinf
        kr = (k * cosf + pltpu.roll(k, RET_DK // 2, 1) * sinf) * kscale
        vb = rv_ref[:, h * RET_DV:(h + 1) * RET_DV].astype(BF16)
        scores = _dot_nt(qr.astype(BF16), kr.astype(BF16)) * dmat_ref[h]
        intra = _dot(scores.astype(BF16), vb)
        st = state[h]
        cross = _dot((qr * qdec_ref[h]).astype(BF16), st.astype(BF16))
        kd = (kr * kdec_ref[h]).astype(BF16)
        kv = lax.dot_general(kd, vb, _TN, preferred_element_type=F32)
        state[h] = st * cdec[h] + kv
        y = intra + cross
        mu = jnp.mean(y, axis=-1, keepdims=True)
        dlt = y - mu
        var = jnp.mean(dlt * dlt, axis=-1, keepdims=True)
        g = rg_ref[:, h * RET_DV:(h + 1) * RET_DV]
        yn = dlt * lax.rsqrt(var + EPS) * gn_ref[:, h * RET_DV:(h + 1) * RET_DV] * (g / (1.0 + jnp.exp(-g)))
        o_ref[:, h * RET_DV:(h + 1) * RET_DV] = yn.astype(BF16)


def retention(proj, pos_col, gn_g):
    s = proj.shape[0]
    c = RET_CHUNK
    dmat, kdec, qdec, cdec, inv_full, sign = _ret_consts()
    full3 = lambda n: (0, 0, 0)
    return pl.pallas_call(
        functools.partial(_ret_kernel, cdec),
        grid=(s // c,),
        in_specs=[pl.BlockSpec((c, RET_QK), lambda n: (n, 0)),
                  pl.BlockSpec((c, RET_QK), lambda n: (n, 1)),
                  pl.BlockSpec((c, RET_WIDTH), lambda n: (n, 1)),
                  pl.BlockSpec((c, RET_WIDTH), lambda n: (n, 2)),
                  pl.BlockSpec((c, 1), lambda n: (n, 0)),
                  pl.BlockSpec((1, RET_DK), lambda n: (0, 0)),
                  pl.BlockSpec((1, RET_DK), lambda n: (0, 0)),
                  pl.BlockSpec((RET_HEADS, c, c), full3),
                  pl.BlockSpec((RET_HEADS, c, RET_DK), full3),
                  pl.BlockSpec((RET_HEADS, c, RET_DK), full3),
                  pl.BlockSpec((1, RET_WIDTH), lambda n: (0, 0))],
        out_specs=pl.BlockSpec((c, RET_WIDTH), lambda n: (n, 0)),
        out_shape=jax.ShapeDtypeStruct((s, RET_WIDTH), BF16),
        scratch_shapes=[pltpu.VMEM((RET_HEADS, RET_DK, RET_DV), F32)],
        compiler_params=_params(("arbitrary",)),
        name="ret",
    )(proj, proj, proj, proj, pos_col, jnp.asarray(inv_full), jnp.asarray(sign),
      jnp.asarray(dmat), jnp.asarray(kdec), jnp.asarray(qdec), gn_g)


def _fox_kernel(q_ref, k_ref, c_ref, vt_ref, o_ref, m_scr, l_scr, acc_scr, *bufs):
    i = pl.program_id(1)
    bq = q_ref.shape[0]
    bk = bufs[0].shape[0]
    nd = len(bufs)
    assert nd * bk == bq
    sub = 8
    lane = lax.broadcasted_iota(jnp.int32, (bq, FOX_DH), 1)
    qa = jnp.where(lane < 3, -1.0, 0.0).astype(BF16)
    q2 = jnp.concatenate([(q_ref[...] * (FOX_DH ** -0.5 * LOG2E)).astype(BF16), qa], axis=1)
    m_scr[...] = jnp.full_like(m_scr, NEG_BIG)
    l_scr[...] = jnp.zeros_like(l_scr)
    acc_scr[...] = jnp.zeros_like(acc_scr)
    ones_rows = jnp.where(lax.broadcasted_iota(jnp.int32, (2 * sub, bk), 0) == 0, 1.0, 0.0).astype(BF16)

    def logits(j, s_scr, c0=0):
        off = pl.multiple_of(j * bk, bk)
        k2 = jnp.concatenate([k_ref[pl.ds(off, bk), :], c_ref[pl.ds(off, bk), :]], axis=1)
        s_scr[:, c0:bq] = _dot_nt(k2, q2[c0:bq, :])

    def update_cols(j, s_scr, c0, c1, diagonal):
        w = c1 - c0
        st = s_scr[:, c0:c1].reshape(bk // sub, sub, w)
        if diagonal:
            kid = lax.broadcasted_iota(jnp.int32, (bk, bk), 0).reshape(bk // sub, sub, bk)
            qid = lax.broadcasted_iota(jnp.int32, (bk, bk), 1).reshape(bk // sub, sub, bk)
            st = jnp.where(kid <= qid, st, NEG_BIG)
        m_old = m_scr[:, c0:c1]
        m_new = jnp.maximum(m_old, jnp.max(jnp.max(st, axis=0), axis=0, keepdims=True))
        alpha = jnp.exp2(m_old - m_new)
        p = jnp.exp2(st - jnp.broadcast_to(m_new, (sub, w))[None])
        pv = _dot(jnp.concatenate([vt_ref[j], ones_rows], axis=0), p.reshape(bk, w).astype(BF16))
        l_scr[:, c0:c1] = alpha * l_scr[:, c0:c1] + pv[FOX_DH:FOX_DH + 1, :]
        a8 = jnp.broadcast_to(alpha, (sub, w))[None]
        acc = acc_scr[:, c0:c1].reshape(FOX_DH // sub, sub, w)
        acc_scr[:, c0:c1] = (a8 * acc).reshape(FOX_DH, w) + pv[0:FOX_DH, :]
        m_scr[:, c0:c1] = m_new

    def update(j, s_scr, c0=0, diagonal=False):
        if diagonal:
            update_cols(j, s_scr, c0, c0 + bk, True)
            c0 += bk
        if c0 < bq:
            update_cols(j, s_scr, c0, bq, False)

    ahead = 2
    for k in range(ahead):
        logits(k, bufs[k])

    def body(jj, carry):
        for k in range(nd):
            logits(nd * jj + k + ahead, bufs[(k + ahead) % nd])
            update(nd * jj + k, bufs[k])
        return carry

    lax.fori_loop(0, i, body, 0)
    for k in range(nd):
        if k + ahead < nd:
            logits(nd * i + k + ahead, bufs[k + ahead], (k + ahead) * bk)
        update(nd * i + k, bufs[k], k * bk, diagonal=True)
    o_ref[...] = (acc_scr[...] / l_scr[...]).T


def fox_attention(pf32, kb, caug, vt3, *, bq=2048):
    s = pf32.shape[0]
    bk = vt3.shape[2]
    assert vt3.shape == (s // bk, FOX_WIDTH, bk) and bq % (2 * bk) == 0
    qcol0 = (N_F32_COLS - FOX_WIDTH) // FOX_DH
    return pl.pallas_call(
        _fox_kernel,
        grid=(FOX_HEADS, s // bq),
        in_specs=[pl.BlockSpec((bq, FOX_DH), lambda h, i: (i, qcol0 + h)),
                  pl.BlockSpec((s, FOX_DH), lambda h, i: (0, h)),
                  pl.BlockSpec((s, LANES), lambda h, i: (0, h)),
                  pl.BlockSpec((s // bk, FOX_DH, bk), lambda h, i: (0, h, 0))],
        out_specs=pl.BlockSpec((bq, FOX_DH), lambda h, i: (i, h)),
        out_shape=jax.ShapeDtypeStruct((s, FOX_WIDTH), F32),
        scratch_shapes=[pltpu.VMEM((1, bq), F32), pltpu.VMEM((1, bq), F32), pltpu.VMEM((FOX_DH, bq), F32)]
        + [pltpu.VMEM((bk, bq), F32)] * (bq // bk),
        compiler_params=_params(("arbitrary", "arbitrary")),
        name="fox",
    )(pf32, kb, caug, vt3)


def _outp_kernel(yr_ref, yf_ref, fg_ref, w_ref, x_ref, g1_ref, o_ref, y_scr):
    @pl.when(pl.program_id(1) == 0)
    def _():
        yf = yf_ref[...]
        ms = jnp.mean(yf * yf, axis=-1, keepdims=True)
        yn = yf * lax.rsqrt(ms + EPS) * fg_ref[...]
        y_scr[:, 0:RET_WIDTH] = yr_ref[...]
        y_scr[:, RET_WIDTH:RET_WIDTH + FOX_WIDTH] = yn.astype(BF16)

    tn = w_ref.shape[1]
    piece = tn // 2
    for c0 in range(0, tn, piece):
        cs = slice(c0, c0 + piece)
        o_ref[:, cs] = x_ref[:, cs] + g1_ref[:, cs] * _dot(y_scr[...], w_ref[:, cs])


def out_proj(y_ret, y_fox, fox_g, w_out, x, gate1, *, tm=512, tn=1024):
    s, d = x.shape
    kdim = RET_WIDTH + FOX_WIDTH
    return pl.pallas_call(
        _outp_kernel,
        grid=(s // tm, d // tn),
        in_specs=[pl.BlockSpec((tm, RET_WIDTH), lambda i, j: (i, 0)),
                  pl.BlockSpec((tm, FOX_WIDTH), lambda i, j: (i, 0)),
                  pl.BlockSpec((1, FOX_WIDTH), lambda i, j: (0, 0)),
                  pl.BlockSpec((kdim, tn), lambda i, j: (0, j)),
                  pl.BlockSpec((tm, tn), lambda i, j: (i, j)),
                  pl.BlockSpec((1, tn), lambda i, j: (0, j))],
        out_specs=pl.BlockSpec((tm, tn), lambda i, j: (i, j)),
        out_shape=jax.ShapeDtypeStruct((s, d), F32),
        scratch_shapes=[pltpu.VMEM((tm, kdim), BF16)],
        compiler_params=_params(("arbitrary", "arbitrary")),
        name="outp",
    )(y_ret, y_fox, fox_g, w_out, x, gate1)


def _drop_max(cur):
    mx = jnp.max(jnp.max(cur, axis=0), axis=0, keepdims=True)
    hit = cur == jnp.broadcast_to(mx, cur.shape[1:])[None]
    return mx, hit, jnp.where(hit, -jnp.inf, cur)


def _top_rows(s, t_scr, ls):
    cur = s.reshape(s.shape[0] // 8, 8, s.shape[1])
    level = jnp.full(cur.shape, float(PEER_TOPK), F32)
    for r in range(PEER_TOPK):
        mx, hit, cur = _drop_max(cur)
        t_scr[r:r + 1, ls] = mx
        level = jnp.where(hit, float(r), level)
    return level.reshape(s.shape)


def _peerq_kernel(group, x_ref, g_ref, sc_ref, sh_ref, wq0_ref, wqn_ref, keys_ref, u_ref, v_ref,
                  h2t_ref, kc_ref, e1_ref, l2_ref, e2_ref, ub_ref, vtb_ref,
                  h_scr, qb_scr, s1_scr, s2_scr, t1_scr, t2_scr, cand_scr):
    head = pl.program_id(1)
    step = pl.program_id(0) * pl.num_programs(1) + head
    cur = head % 2

    @pl.when(step % group == 0)
    def _():
        ub_ref[...] = u_ref[...].astype(BF16)
        vtb_ref[...] = v_ref[...].T.astype(BF16)

    @pl.when(pl.program_id(1) == 0)
    def _():
        h = _norm_mod(x_ref[...], g_ref[...], sc_ref[...], sh_ref[...])
        h_scr[...] = h.astype(BF16)
        h2t_ref[...] = h.T.astype(BF16)
        qb_scr[0] = _dot(h_scr[...], wq0_ref[...]).astype(BF16)

    half = PEER_DQ // 2
    qb = qb_scr[cur]
    s1 = _dot_nt(keys_ref[0, 0], qb[:, 0:half])
    s2 = _dot_nt(keys_ref[0, 1], qb[:, half:PEER_DQ])
    s1_scr[...] = s1
    s2_scr[...] = s2
    qb_scr[1 - cur] = _dot(h_scr[...], wqn_ref[...]).astype(BF16)
    for g in range(s1.shape[1] // LANES):
        ls = slice(g * LANES, (g + 1) * LANES)
        s1l = s1_scr[:, ls]
        s2l = s2_scr[:, ls]
        _top_rows(s1l, t1_scr, ls)
        l2_ref[0, :, ls] = _top_rows(s2l, t2_scr, ls).astype(BF16)
        t1_lo = t1_scr[0:8, ls]
        for b in range(8):
            cand_scr[b * 8:(b + 1) * 8, ls] = t1_lo + t2_scr[b:b + 1, ls]
        cand_scr[64:72, ls] = t1_scr[0:1, ls] + t2_scr[8:16, ls]
        cand_scr[72:80, ls] = t1_scr[8:16, ls] + t2_scr[0:1, ls]
        cand = cand_scr[:, ls].reshape(PEER_NCAND // 8, 8, LANES)
        cur = cand
        for r in range(PEER_TOPK):
            tau, _, cur = _drop_max(cur)
        m1 = t1_scr[0:1, ls]
        m2 = t2_scr[0:1, ls]
        top8 = jnp.broadcast_to(m1 + m2, (8, LANES))[None]
        tau8 = jnp.broadcast_to(tau, (8, LANES))[None]
        z = jnp.sum(jnp.sum(jnp.where(cand >= tau8, jnp.exp(cand - top8), 0.0), axis=0), axis=0, keepdims=True)
        s1g = s1l.reshape(PEER_NKEYS // 8, 8, LANES)
        kc = jnp.zeros(s1g.shape, F32)
        for b in range(PEER_TOPK):
            t2b = jnp.broadcast_to(t2_scr[b:b + 1, ls], (8, LANES))[None]
            kc = jnp.where(s1g + t2b >= tau8, float(b + 1), kc)
        kc_ref[0, :, ls] = kc.reshape(PEER_NKEYS, LANES)
        e1_ref[0, :, ls] = jnp.exp(s1l - (m1 + jnp.log(z)))
        e2_ref[0, :, ls] = jnp.exp(s2l - m2).astype(BF16)


def peer_query(x1, g, scale, shift, wq, keys, u, v, *, tm=512):
    s, d = x1.shape
    ne = u.shape[0]
    nsteps = (s // tm) * PEER_HEADS
    per_step = ne // nsteps
    group = max(1, LANES // per_step)
    rows = per_step * group
    assert rows * (nsteps // group) == ne and rows % LANES == 0
    vec = pl.BlockSpec((1, d), lambda i, h: (0, 0))
    stat = pl.BlockSpec((1, PEER_NKEYS, tm), lambda i, h: (h, 0, i))
    table = lambda dt: jax.ShapeDtypeStruct((PEER_HEADS, PEER_NKEYS, s), dt)
    eblk = lambda i, h: (i * PEER_HEADS + h) // group
    return pl.pallas_call(
        functools.partial(_peerq_kernel, group),
        grid=(s // tm, PEER_HEADS),
        in_specs=[pl.BlockSpec((tm, d), lambda i, h: (i, 0)), vec, vec, vec,
                  pl.BlockSpec((d, PEER_DQ), lambda i, h: (0, 0)),
                  pl.BlockSpec((d, PEER_DQ), lambda i, h: (0, jnp.minimum(h + 1, PEER_HEADS - 1))),
                  pl.BlockSpec((1, 2, PEER_NKEYS, PEER_DQ // 2), lambda i, h: (h, 0, 0, 0)),
                  pl.BlockSpec((rows, d), lambda i, h: (eblk(i, h), 0)),
                  pl.BlockSpec((rows, d), lambda i, h: (eblk(i, h), 0))],
        out_specs=[pl.BlockSpec((d, tm), lambda i, h: (0, i)), stat, stat, stat, stat,
                   pl.BlockSpec((rows, d), lambda i, h: (eblk(i, h), 0)),
                   pl.BlockSpec((d, rows), lambda i, h: (0, eblk(i, h)))],
        out_shape=[jax.ShapeDtypeStruct((d, s), BF16), table(F32), table(F32), table(BF16), table(BF16),
                   jax.ShapeDtypeStruct((ne, d), BF16), jax.ShapeDtypeStruct((d, ne), BF16)],
        scratch_shapes=[pltpu.VMEM((tm, d), BF16), pltpu.VMEM((2, tm, PEER_DQ), BF16),
                        pltpu.VMEM((PEER_NKEYS, tm), F32), pltpu.VMEM((PEER_NKEYS, tm), F32),
                        pltpu.VMEM((PEER_TOPK, tm), F32), pltpu.VMEM((PEER_TOPK, tm), F32),
                        pltpu.VMEM((PEER_NCAND, tm), F32)],
        compiler_params=_params(("arbitrary", "arbitrary")),
        name="peerq",
    )(x1, g, scale, shift, wq, wq, keys, u, v)


_GELU_C = 0.7978845608028654


def _gelu_tanh(x):
    k1 = -2.0 * _GELU_C * LOG2E
    z2 = x * (k1 + (k1 * 0.044715) * (x * x))
    return x / (1.0 + jnp.exp2(z2))


def _peer_kernel(nblk, h2t_ref, u_ref, vt_ref, kc_ref, e1_ref, l2_ref, e2_ref, o_ref, a_scr, w_scr):
    t = pl.program_id(0)
    tm = h2t_ref.shape[1]
    tn = u_ref.shape[0]
    rows = 16
    nrow = tn // PEER_NKEYS
    cur = t % 2

    @pl.when(jnp.maximum(t - 1, 0) % nblk == 0)
    def _():
        o_ref[...] = jnp.zeros_like(o_ref)

    @pl.when(t == 0)
    def _():
        w_scr[1] = jnp.zeros(w_scr.shape[1:], BF16)

    a_scr[...] = _dot(u_ref[...], h2t_ref[...])
    o_ref[...] += _dot(vt_ref[...], w_scr[1 - cur])
    base = lax.shift_right_logical(t, 30) * rows

    for c in range(nrow):
        i1 = (t % nblk) * nrow + c
        krow = [jnp.broadcast_to(kc_ref[h, pl.ds(i1, 1), :], (rows, tm)).astype(BF16) for h in range(PEER_HEADS)]
        e1row = [jnp.broadcast_to(e1_ref[h, pl.ds(i1, 1), :], (rows, tm)).astype(BF16) for h in range(PEER_HEADS)]
        for r0 in range(0, PEER_NKEYS, rows):
            gate = None
            for h in range(PEER_HEADS):
                gv = jnp.where(l2_ref[h, r0:r0 + rows, :] < krow[h], e2_ref[h, r0:r0 + rows, :] * e1row[h],
                               jnp.zeros((), BF16))
                gate = gv if gate is None else gate + gv
            rr = c * PEER_NKEYS + r0
            a_rows = a_scr[pl.ds(pl.multiple_of(base + rr, rows), rows), :]
            w_scr[cur, rr:rr + rows, :] = _gelu_tanh(a_rows.astype(BF16)) * gate


def peer_experts(h2t, u, vt, kc, e1, l2, e2, *, tm=512, tn=512):
    d, s = h2t.shape
    ne = u.shape[0]
    nblk, ntile = ne // tn, s // tm
    once = pl.Buffered(1)
    tile_a = lambda t: jnp.minimum(t // nblk, ntile - 1)
    prev = lambda t: jnp.maximum(t - 1, 0)
    table = pl.BlockSpec((PEER_HEADS, PEER_NKEYS, tm), lambda t: (0, 0, tile_a(t)), pipeline_mode=once)
    return pl.pallas_call(
        functools.partial(_peer_kernel, nblk),
        grid=(ntile * nblk + 1,),
        in_specs=[pl.BlockSpec((d, tm), lambda t: (0, tile_a(t)), pipeline_mode=once),
                  pl.BlockSpec((tn, d), lambda t: (t % nblk, 0)),
                  pl.BlockSpec((d, tn), lambda t: (0, prev(t) % nblk)),
                  table, table, table, table],
        out_specs=pl.BlockSpec((d, tm), lambda t: (0, prev(t) // nblk)),
        out_shape=jax.ShapeDtypeStruct((d, s), F32),
        scratch_shapes=[pltpu.VMEM((tn, tm), F32), pltpu.VMEM((2, tn, tm), BF16)],
        compiler_params=_params(("arbitrary",)),
        name="peer",
    )(h2t, u, vt, kc, e1, l2, e2)


def _fin_kernel(x_ref, pt_ref, g2_ref, fg_ref, o_ref):
    x2 = x_ref[...] + g2_ref[...] * pt_ref[...].T
    ms = jnp.mean(x2 * x2, axis=-1, keepdims=True)
    o_ref[...] = x2 * lax.rsqrt(ms + EPS) * fg_ref[...]


def final_norm(x1, peer_t, gate2, final_g, *, tm=256):
    s, d = x1.shape
    vec = pl.BlockSpec((1, d), lambda i: (0, 0))
    return pl.pallas_call(
        _fin_kernel,
        grid=(s // tm,),
        in_specs=[pl.BlockSpec((tm, d), lambda i: (i, 0)),
                  pl.BlockSpec((d, tm), lambda i: (0, i)), vec, vec],
        out_specs=pl.BlockSpec((tm, d), lambda i: (i, 0)),
        out_shape=jax.ShapeDtypeStruct((s, d), F32),
        compiler_params=_params(("arbitrary",)),
        name="fin",
    )(x1, peer_t, gate2, final_g)


def _layer(x, mod, positions, norm1_g, w_in, b_forget, ret_gn_g, fox_norm_g, w_out, norm2_g,
           w_peer_q, peer_sub_keys, peer_u, peer_v, *, bq=2048, bk=512):
    s, d = x.shape
    shift1, scale1, gate1, shift2, scale2, gate2 = [mod[:, k * d:(k + 1) * d] for k in range(6)]
    row = lambda v: v.reshape(1, -1)

    w_t = w_in.T
    w_main = cast_bf16(w_t, rows=IN_MAIN, transpose=True, br=2048, bc=512, name="cast_in")
    n_ff = w_in.shape[1] - IN_MAIN
    w_ff = tail_weight(w_t, IN_MAIN, n_ff)
    pf32, kb, vt3, ff = in_proj(x, row(norm1_g), scale1, shift1, w_main, w_ff, tm=bk)

    b128 = jnp.pad(b_forget, (0, LANES - n_ff)).reshape(1, LANES)
    caug = cum_gate(ff, b128)
    y_ret = retention(pf32, positions.reshape(s, 1), row(ret_gn_g))
    y_fox = fox_attention(pf32, kb, caug, vt3, bq=bq)
    x1 = out_proj(y_ret, y_fox, row(fox_norm_g), cast_bf16(w_out, name="cast_out"), x, gate1)

    h2t, kc, e1, l2, e2, ub, vtb = peer_query(x1, row(norm2_g), scale2, shift2, cast_bf16(w_peer_q, name="cast_q"),
                                              peer_sub_keys.astype(BF16), peer_u, peer_v)
    peer_t = peer_experts(h2t, ub, vtb, kc, e1, l2, e2)
    return x1, peer_t, gate2


def kernel(x, c, positions, w_ada, b_ada, norm1_g, w_in, b_forget, ret_gn_g, fox_norm_g, w_out,
           norm2_g, w_peer_q, peer_sub_keys, peer_u, peer_v, final_g):
    b, s, d = x.shape
    assert b == 1 and w_ada.shape[0] == 1, "one sequence, one layer"
    c8 = jnp.broadcast_to(c, (8, d))
    mod = ada_mod(c8, w_ada[0], b_ada[0].reshape(1, -1))[0:1]
    x1, peer_t, gate2 = _layer(x[0], mod, positions[0], norm1_g[0], w_in[0], b_forget[0], ret_gn_g[0],
                               fox_norm_g[0], w_out[0], norm2_g[0], w_peer_q[0], peer_sub_keys[0],
                               peer_u[0], peer_v[0])
    return final_norm(x1, peer_t, gate2, final_g.reshape(1, -1))[None]
```

```python
import functools

import numpy as np
import jax
import jax.numpy as jnp
from jax import lax
from jax.experimental import pallas as pl
from jax.experimental.pallas import tpu as pltpu

F32 = jnp.float32
BF16 = jnp.bfloat16

LANES = 128
RET_HEADS = 8
RET_DK = 128
RET_DV = 256
RET_QK = RET_HEADS * RET_DK
RET_WIDTH = RET_HEADS * RET_DV
RET_CHUNK = 128
FOX_HEADS = 16
FOX_DH = 128
FOX_WIDTH = FOX_HEADS * FOX_DH
IN_MAIN = 2 * RET_QK + 2 * RET_WIDTH + 3 * FOX_WIDTH
N_F32_COLS = 2 * RET_QK + 2 * RET_WIDTH + FOX_WIDTH
ROPE_BASE = 10000.0
PEER_HEADS = 8
PEER_NKEYS = 128
PEER_DQ = 256
PEER_TOPK = 16
PEER_NCAND = 80
ACT_PIECE = 128
EPS = 1e-6
NEG_BIG = -1e30
LOG2E = 1.4426950408889634
VMEM_LIMIT = 56 * 1024 * 1024

_NT = (((1,), (1,)), ((), ()))
_TN = (((0,), (0,)), ((), ()))


def _params(sem):
    return pltpu.CompilerParams(dimension_semantics=sem, vmem_limit_bytes=VMEM_LIMIT)


def _dot(a, b):
    return jnp.dot(a, b, preferred_element_type=F32)


def _dot_nt(a, b):
    return lax.dot_general(a, b, _NT, preferred_element_type=F32)


def _ada_kernel(c_ref, w_ref, b_ref, o_ref):
    c = c_ref[...]
    ca = (c / (1.0 + jnp.exp(-c))).astype(BF16)
    o_ref[...] = _dot(ca, w_ref[...].astype(BF16)) + b_ref[...]


def ada_mod(c8, w, b, *, bn=512):
    d, n = w.shape
    return pl.pallas_call(
        _ada_kernel,
        grid=(n // bn,),
        in_specs=[pl.BlockSpec((8, d), lambda j: (0, 0)),
                  pl.BlockSpec((d, bn), lambda j: (0, j)),
                  pl.BlockSpec((1, bn), lambda j: (0, j))],
        out_specs=pl.BlockSpec((8, bn), lambda j: (0, j)),
        out_shape=jax.ShapeDtypeStruct((8, n), F32),
        compiler_params=_params(("arbitrary",)),
        name="ada",
    )(c8, w, b)


def _cast_kernel(x_ref, o_ref):
    o_ref[...] = x_ref[...].astype(BF16)


def _cast_t_kernel(x_ref, o_ref):
    o_ref[...] = x_ref[...].T.astype(BF16)


def cast_bf16(x, *, rows=None, cols=None, br=512, bc=2048, transpose=False, name="cast"):
    r, c = x.shape
    r = r if rows is None else rows
    c = c if cols is None else cols
    br, bc = min(br, r), min(bc, c)
    if transpose:
        return pl.pallas_call(
            _cast_t_kernel, grid=(r // br, c // bc),
            in_specs=[pl.BlockSpec((br, bc), lambda i, j: (i, j))],
            out_specs=pl.BlockSpec((bc, br), lambda i, j: (j, i)),
            out_shape=jax.ShapeDtypeStruct((c, r), BF16),
            compiler_params=_params(("arbitrary", "arbitrary")), name=name)(x)
    return pl.pallas_call(
        _cast_kernel, grid=(r // br, c // bc),
        in_specs=[pl.BlockSpec((br, bc), lambda i, j: (i, j))],
        out_specs=pl.BlockSpec((br, bc), lambda i, j: (i, j)),
        out_shape=jax.ShapeDtypeStruct((r, c), BF16),
        compiler_params=_params(("arbitrary", "arbitrary")), name=name)(x)


def _tail_kernel(x_ref, o_ref):
    o_ref[...] = jnp.zeros_like(o_ref)
    o_ref[:, 0:x_ref.shape[0]] = x_ref[...].T.astype(BF16)


def tail_weight(w_t, row0, nrows):
    d = w_t.shape[1]
    assert row0 % nrows == 0 and nrows % 8 == 0
    return pl.pallas_call(
        _tail_kernel, grid=(1,),
        in_specs=[pl.BlockSpec((nrows, d), lambda i: (row0 // nrows, 0))],
        out_specs=pl.BlockSpec((d, LANES), lambda i: (0, 0)),
        out_shape=jax.ShapeDtypeStruct((d, LANES), BF16),
        compiler_params=_params(("arbitrary",)), name="cast_tail")(w_t)


def _norm_mod(x, g, scale, shift):
    ms = jnp.mean(x * x, axis=-1, keepdims=True)
    y = x * lax.rsqrt(ms + EPS) * g
    return y * (1.0 + scale) + shift


def _inpj_kernel(nf, nk, x_ref, g_ref, sc_ref, sh_ref, w_ref, wf_ref, o_ref, ok_ref, ovt_ref, of_ref, h_scr):
    j = pl.program_id(1)

    @pl.when(j == 0)
    def _():
        hb = _norm_mod(x_ref[...], g_ref[...], sc_ref[...], sh_ref[...]).astype(BF16)
        h_scr[...] = hb
        of_ref[...] = _dot(hb, wf_ref[...])

    tn = w_ref.shape[1]
    piece = tn // 2

    def pieces(store):
        for c0 in range(0, tn, piece):
            store(c0, _dot(h_scr[...], w_ref[:, c0:c0 + piece]))

    @pl.when(j < nf)
    def _():
        def store(c0, r):
            o_ref[:, c0:c0 + piece] = r
        pieces(store)

    @pl.when(jnp.logical_and(j >= nf, j < nf + nk))
    def _():
        def store(c0, r):
            ok_ref[:, c0:c0 + piece] = r.astype(BF16)
        pieces(store)

    @pl.when(j >= nf + nk)
    def _():
        def store(c0, r):
            ovt_ref[0, c0:c0 + piece, :] = r.T.astype(BF16)
        pieces(store)


def in_proj(x, g, scale, shift, w_main, w_ff, *, tm=512, tn=1024):
    s, d = x.shape
    n = w_main.shape[1]
    nf, nk = N_F32_COLS // tn, FOX_WIDTH // tn
    vec = pl.BlockSpec((1, d), lambda i, j: (0, 0))
    return pl.pallas_call(
        functools.partial(_inpj_kernel, nf, nk),
        grid=(s // tm, n // tn),
        in_specs=[pl.BlockSpec((tm, d), lambda i, j: (i, 0)), vec, vec, vec,
                  pl.BlockSpec((d, tn), lambda i, j: (0, j)),
                  pl.BlockSpec((d, LANES), lambda i, j: (0, 0))],
        out_specs=[pl.BlockSpec((tm, tn), lambda i, j: (i, jnp.minimum(j, nf - 1))),
                   pl.BlockSpec((tm, tn), lambda i, j: (i, jnp.clip(j - nf, 0, nk - 1))),
                   pl.BlockSpec((1, tn, tm), lambda i, j: (i, jnp.clip(j - nf - nk, 0, nk - 1), 0)),
                   pl.BlockSpec((tm, LANES), lambda i, j: (i, 0))],
        out_shape=[jax.ShapeDtypeStruct((s, N_F32_COLS), F32),
                   jax.ShapeDtypeStruct((s, FOX_WIDTH), BF16),
                   jax.ShapeDtypeStruct((s // tm, FOX_WIDTH, tm), BF16),
                   jax.ShapeDtypeStruct((s, LANES), F32)],
        scratch_shapes=[pltpu.VMEM((tm, d), BF16)],
        compiler_params=_params(("arbitrary", "arbitrary")),
        name="inpj",
    )(x, g, scale, shift, w_main, w_ff)


def _split3(v):
    hi = v.astype(BF16)
    r1 = v - hi.astype(F32)
    mid = r1.astype(BF16)
    lo = (r1 - mid.astype(F32)).astype(BF16)
    return hi, mid, lo


def _cum_place():
    m = np.zeros((3 * LANES, FOX_HEADS * LANES), np.float32)
    for p in range(3):
        for h in range(FOX_HEADS):
            m[p * LANES + h, h * LANES + p] = 1.0
    return m


def _cum_kernel(ff_ref, b_ref, place_ref, o_ref, carry):
    @pl.when(pl.program_id(0) == 0)
    def _():
        carry[...] = jnp.zeros_like(carry)

    z = ff_ref[...] + b_ref[...]
    logf = jnp.minimum(z, 0.0) - jnp.log(1.0 + jnp.exp(-jnp.abs(z)))
    tc = z.shape[0]
    row = lax.broadcasted_iota(jnp.int32, (tc, tc), 0)
    col = lax.broadcasted_iota(jnp.int32, (tc, tc), 1)
    tri = jnp.where(row >= col, 1.0, 0.0).astype(BF16)
    hi, mid, lo = _split3(logf)
    cum = _dot(tri, hi) + _dot(tri, mid) + _dot(tri, lo) + carry[...]
    carry[...] = cum[tc - 1:tc, :]
    pieces = jnp.concatenate(_split3(cum * LOG2E), axis=1)
    o_ref[...] = _dot(pieces, place_ref[...]).astype(BF16)


def cum_gate(ff, b128, *, tc=256):
    s = ff.shape[0]
    place = jnp.asarray(_cum_place(), BF16)
    return pl.pallas_call(
        _cum_kernel,
        grid=(s // tc,),
        in_specs=[pl.BlockSpec((tc, LANES), lambda i: (i, 0)),
                  pl.BlockSpec((1, LANES), lambda i: (0, 0)),
                  pl.BlockSpec(place.shape, lambda i: (0, 0))],
        out_specs=pl.BlockSpec((tc, FOX_HEADS * LANES), lambda i: (i, 0)),
        out_shape=jax.ShapeDtypeStruct((s, FOX_HEADS * LANES), BF16),
        scratch_shapes=[pltpu.VMEM((1, LANES), F32)],
        compiler_params=_params(("arbitrary",)),
        name="cum",
    )(ff, b128, place)


def _ret_consts():
    h = np.arange(RET_HEADS, dtype=np.float32)
    log_g = np.log1p(-np.exp2(-5.0 - h)).astype(np.float32)
    i = np.arange(RET_CHUNK, dtype=np.float32)
    diff = i[:, None] - i[None, :]
    dmat = np.where(diff >= 0, np.exp(np.maximum(diff, 0.0)[None] * log_g[:, None, None]), 0.0)
    kdec = np.exp((RET_CHUNK - 1.0 - i)[None, :] * log_g[:, None])
    qdec = np.exp((i + 1.0)[None, :] * log_g[:, None])
    cdec = np.exp(RET_CHUNK * log_g)
    kdec = np.broadcast_to(kdec[:, :, None], (RET_HEADS, RET_CHUNK, RET_DK))
    qdec = np.broadcast_to(qdec[:, :, None], (RET_HEADS, RET_CHUNK, RET_DK))
    half = RET_DK // 2
    inv = (ROPE_BASE ** (-np.arange(half, dtype=np.float32) / half)).astype(np.float32)
    inv_full = np.concatenate([inv, inv])[None, :]
    sign = np.concatenate([-np.ones(half, np.float32), np.ones(half, np.float32)])[None, :]
    return (dmat.astype(np.float32), np.ascontiguousarray(kdec, np.float32),
            np.ascontiguousarray(qdec, np.float32), [float(v) for v in cdec],
            inv_full.astype(np.float32), sign)


def _ret_kernel(cdec, rq_ref, rk_ref, rv_ref, rg_ref, pos_ref, inv_ref, sign_ref,
                dmat_ref, kdec_ref, qdec_ref, gn_ref, o_ref, state):
    @pl.when(pl.program_id(0) == 0)
    def _():
        state[...] = jnp.zeros_like(state)

    ang = pos_ref[...].astype(F32) * inv_ref[...]
    cosf = jnp.cos(ang)
    sinf = jnp.sin(ang) * sign_ref[...]
    kscale = RET_DK ** -0.5
    for h in range(RET_HEADS):
        q = rq_ref[:, h * RET_DK:(h + 1) * RET_DK]
        k = rk_ref[:, h * RET_DK:(h + 1) * RET_DK]
        qr = q * cosf + pltpu.roll(q, RET_DK // 2, 1) * sinf
        kr = (k * cosf + pltpu.roll(k, RET_DK // 2, 1) * sinf) * kscale
        vb = rv_ref[:, h * RET_DV:(h + 1) * RET_DV].astype(BF16)
        scores = _dot_nt(qr.astype(BF16), kr.astype(BF16)) * dmat_ref[h]
        intra = _dot(scores.astype(BF16), vb)
        st = state[h]
        cross = _dot((qr * qdec_ref[h]).astype(BF16), st.astype(BF16))
        kd = (kr * kdec_ref[h]).astype(BF16)
        kv = lax.dot_general(kd, vb, _TN, preferred_element_type=F32)
        state[h] = st * cdec[h] + kv
        y = intra + cross
        mu = jnp.mean(y, axis=-1, keepdims=True)
        dlt = y - mu
        var = jnp.mean(dlt * dlt, axis=-1, keepdims=True)
        g = rg_ref[:, h * RET_DV:(h + 1) * RET_DV]
        yn = dlt * lax.rsqrt(var + EPS) * gn_ref[:, h * RET_DV:(h + 1) * RET_DV] * (g / (1.0 + jnp.exp(-g)))
        o_ref[:, h * RET_DV:(h + 1) * RET_DV] = yn.astype(BF16)


def retention(proj, pos_col, gn_g):
    s = proj.shape[0]
    c = RET_CHUNK
    dmat, kdec, qdec, cdec, inv_full, sign = _ret_consts()
    full3 = lambda n: (0, 0, 0)
    return pl.pallas_call(
        functools.partial(_ret_kernel, cdec),
        grid=(s // c,),
        in_specs=[pl.BlockSpec((c, RET_QK), lambda n: (n, 0)),
                  pl.BlockSpec((c, RET_QK), lambda n: (n, 1)),
                  pl.BlockSpec((c, RET_WIDTH), lambda n: (n, 1)),
                  pl.BlockSpec((c, RET_WIDTH), lambda n: (n, 2)),
                  pl.BlockSpec((c, 1), lambda n: (n, 0)),
                  pl.BlockSpec((1, RET_DK), lambda n: (0, 0)),
                  pl.BlockSpec((1, RET_DK), lambda n: (0, 0)),
                  pl.BlockSpec((RET_HEADS, c, c), full3),
                  pl.BlockSpec((RET_HEADS, c, RET_DK), full3),
                  pl.BlockSpec((RET_HEADS, c, RET_DK), full3),
                  pl.BlockSpec((1, RET_WIDTH), lambda n: (0, 0))],
        out_specs=pl.BlockSpec((c, RET_WIDTH), lambda n: (n, 0)),
        out_shape=jax.ShapeDtypeStruct((s, RET_WIDTH), BF16),
        scratch_shapes=[pltpu.VMEM((RET_HEADS, RET_DK, RET_DV), F32)],
        compiler_params=_params(("arbitrary",)),
        name="ret",
    )(proj, proj, proj, proj, pos_col, jnp.asarray(inv_full), jnp.asarray(sign),
      jnp.asarray(dmat), jnp.asarray(kdec), jnp.asarray(qdec), gn_g)


def _fox_kernel(q_ref, k_ref, c_ref, vt_ref, o_ref, m_scr, l_scr, acc_scr, *bufs):
    i = pl.program_id(1)
    bq = q_ref.shape[0]
    bk = bufs[0].shape[0]
    nd = len(bufs)
    assert nd * bk == bq
    sub = 8
    lane = lax.broadcasted_iota(jnp.int32, (bq, FOX_DH), 1)
    qa = jnp.where(lane < 3, -1.0, 0.0).astype(BF16)
    q2 = jnp.concatenate([(q_ref[...] * (FOX_DH ** -0.5 * LOG2E)).astype(BF16), qa], axis=1)
    m_scr[...] = jnp.full_like(m_scr, NEG_BIG)
    l_scr[...] = jnp.zeros_like(l_scr)
    acc_scr[...] = jnp.zeros_like(acc_scr)
    ones_rows = jnp.where(lax.broadcasted_iota(jnp.int32, (2 * sub, bk), 0) == 0, 1.0, 0.0).astype(BF16)

    def logits(j, s_scr, c0=0):
        off = pl.multiple_of(j * bk, bk)
        k2 = jnp.concatenate([k_ref[pl.ds(off, bk), :], c_ref[pl.ds(off, bk), :]], axis=1)
        s_scr[:, c0:bq] = _dot_nt(k2, q2[c0:bq, :])

    def update_cols(j, s_scr, c0, c1, diagonal):
        w = c1 - c0
        st = s_scr[:, c0:c1].reshape(bk // sub, sub, w)
        if diagonal:
            kid = lax.broadcasted_iota(jnp.int32, (bk, bk), 0).reshape(bk // sub, sub, bk)
            qid = lax.broadcasted_iota(jnp.int32, (bk, bk), 1).reshape(bk // sub, sub, bk)
            st = jnp.where(kid <= qid, st, NEG_BIG)
        m_old = m_scr[:, c0:c1]
        m_new = jnp.maximum(m_old, jnp.max(jnp.max(st, axis=0), axis=0, keepdims=True))
        alpha = jnp.exp2(m_old - m_new)
        p = jnp.exp2(st - jnp.broadcast_to(m_new, (sub, w))[None])
        pv = _dot(jnp.concatenate([vt_ref[j], ones_rows], axis=0), p.reshape(bk, w).astype(BF16))
        l_scr[:, c0:c1] = alpha * l_scr[:, c0:c1] + pv[FOX_DH:FOX_DH + 1, :]
        a8 = jnp.broadcast_to(alpha, (sub, w))[None]
        acc = acc_scr[:, c0:c1].reshape(FOX_DH // sub, sub, w)
        acc_scr[:, c0:c1] = (a8 * acc).reshape(FOX_DH, w) + pv[0:FOX_DH, :]
        m_scr[:, c0:c1] = m_new

    def update(j, s_scr, c0=0, diagonal=False):
        if diagonal:
            update_cols(j, s_scr, c0, c0 + bk, True)
            c0 += bk
        if c0 < bq:
            update_cols(j, s_scr, c0, bq, False)

    ahead = 2
    for k in range(ahead):
        logits(k, bufs[k])

    def body(jj, carry):
        for k in range(nd):
            logits(nd * jj + k + ahead, bufs[(k + ahead) % nd])
            update(nd * jj + k, bufs[k])
        return carry

    lax.fori_loop(0, i, body, 0)
    for k in range(nd):
        if k + ahead < nd:
            logits(nd * i + k + ahead, bufs[k + ahead], (k + ahead) * bk)
        update(nd * i + k, bufs[k], k * bk, diagonal=True)
    o_ref[...] = (acc_scr[...] / l_scr[...]).T


def fox_attention(pf32, kb, caug, vt3, *, bq=2048):
    s = pf32.shape[0]
    bk = vt3.shape[2]
    assert vt3.shape == (s // bk, FOX_WIDTH, bk) and bq % (2 * bk) == 0
    qcol0 = (N_F32_COLS - FOX_WIDTH) // FOX_DH
    return pl.pallas_call(
        _fox_kernel,
        grid=(FOX_HEADS, s // bq),
        in_specs=[pl.BlockSpec((bq, FOX_DH), lambda h, i: (i, qcol0 + h)),
                  pl.BlockSpec((s, FOX_DH), lambda h, i: (0, h)),
                  pl.BlockSpec((s, LANES), lambda h, i: (0, h)),
                  pl.BlockSpec((s // bk, FOX_DH, bk), lambda h, i: (0, h, 0))],
        out_specs=pl.BlockSpec((bq, FOX_DH), lambda h, i: (i, h)),
        out_shape=jax.ShapeDtypeStruct((s, FOX_WIDTH), F32),
        scratch_shapes=[pltpu.VMEM((1, bq), F32), pltpu.VMEM((1, bq), F32), pltpu.VMEM((FOX_DH, bq), F32)]
        + [pltpu.VMEM((bk, bq), F32)] * (bq // bk),
        compiler_params=_params(("arbitrary", "arbitrary")),
        name="fox",
    )(pf32, kb, caug, vt3)


def _outp_kernel(yr_ref, yf_ref, fg_ref, w_ref, x_ref, g1_ref, o_ref, y_scr):
    @pl.when(pl.program_id(1) == 0)
    def _():
        yf = yf_ref[...]
        ms = jnp.mean(yf * yf, axis=-1, keepdims=True)
        yn = yf * lax.rsqrt(ms + EPS) * fg_ref[...]
        y_scr[:, 0:RET_WIDTH] = yr_ref[...]
        y_scr[:, RET_WIDTH:RET_WIDTH + FOX_WIDTH] = yn.astype(BF16)

    tn = w_ref.shape[1]
    piece = tn // 2
    for c0 in range(0, tn, piece):
        cs = slice(c0, c0 + piece)
        o_ref[:, cs] = x_ref[:, cs] + g1_ref[:, cs] * _dot(y_scr[...], w_ref[:, cs])


def out_proj(y_ret, y_fox, fox_g, w_out, x, gate1, *, tm=512, tn=1024):
    s, d = x.shape
    kdim = RET_WIDTH + FOX_WIDTH
    return pl.pallas_call(
        _outp_kernel,
        grid=(s // tm, d // tn),
        in_specs=[pl.BlockSpec((tm, RET_WIDTH), lambda i, j: (i, 0)),
                  pl.BlockSpec((tm, FOX_WIDTH), lambda i, j: (i, 0)),
                  pl.BlockSpec((1, FOX_WIDTH), lambda i, j: (0, 0)),
                  pl.BlockSpec((kdim, tn), lambda i, j: (0, j)),
                  pl.BlockSpec((tm, tn), lambda i, j: (i, j)),
                  pl.BlockSpec((1, tn), lambda i, j: (0, j))],
        out_specs=pl.BlockSpec((tm, tn), lambda i, j: (i, j)),
        out_shape=jax.ShapeDtypeStruct((s, d), F32),
        scratch_shapes=[pltpu.VMEM((tm, kdim), BF16)],
        compiler_params=_params(("arbitrary", "arbitrary")),
        name="outp",
    )(y_ret, y_fox, fox_g, w_out, x, gate1)


def _drop_max(cur):
    mx = jnp.max(jnp.max(cur, axis=0), axis=0, keepdims=True)
    hit = cur == jnp.broadcast_to(mx, cur.shape[1:])[None]
    return mx, hit, jnp.where(hit, -jnp.inf, cur)


def _top_rows(s, t_scr, ls):
    cur = s.reshape(s.shape[0] // 8, 8, s.shape[1])
    level = jnp.full(cur.shape, float(PEER_TOPK), F32)
    for r in range(PEER_TOPK):
        mx, hit, cur = _drop_max(cur)
        t_scr[r:r + 1, ls] = mx
        level = jnp.where(hit, float(r), level)
    return level.reshape(s.shape)


def _peerq_kernel(group, x_ref, g_ref, sc_ref, sh_ref, wq0_ref, wqn_ref, keys_ref, u_ref, v_ref,
                  h2t_ref, kc_ref, e1_ref, l2_ref, e2_ref, ub_ref, vtb_ref,
                  h_scr, qb_scr, s1_scr, s2_scr, t1_scr, t2_scr, cand_scr):
    head = pl.program_id(1)
    step = pl.program_id(0) * pl.num_programs(1) + head
    cur = head % 2

    @pl.when(step % group == 0)
    def _():
        ub_ref[...] = u_ref[...].astype(BF16)
        vtb_ref[...] = v_ref[...].T.astype(BF16)

    @pl.when(pl.program_id(1) == 0)
    def _():
        h = _norm_mod(x_ref[...], g_ref[...], sc_ref[...], sh_ref[...])
        h_scr[...] = h.astype(BF16)
        h2t_ref[...] = h.T.astype(BF16)
        qb_scr[0] = _dot(h_scr[...], wq0_ref[...]).astype(BF16)

    half = PEER_DQ // 2
    qb = qb_scr[cur]
    s1 = _dot_nt(keys_ref[0, 0], qb[:, 0:half])
    s2 = _dot_nt(keys_ref[0, 1], qb[:, half:PEER_DQ])
    s1_scr[...] = s1
    s2_scr[...] = s2
    qb_scr[1 - cur] = _dot(h_scr[...], wqn_ref[...]).astype(BF16)
    for g in range(s1.shape[1] // LANES):
        ls = slice(g * LANES, (g + 1) * LANES)
        s1l = s1_scr[:, ls]
        s2l = s2_scr[:, ls]
        _top_rows(s1l, t1_scr, ls)
        l2_ref[0, :, ls] = _top_rows(s2l, t2_scr, ls).astype(BF16)
        t1_lo = t1_scr[0:8, ls]
        for b in range(8):
            cand_scr[b * 8:(b + 1) * 8, ls] = t1_lo + t2_scr[b:b + 1, ls]
        cand_scr[64:72, ls] = t1_scr[0:1, ls] + t2_scr[8:16, ls]
        cand_scr[72:80, ls] = t1_scr[8:16, ls] + t2_scr[0:1, ls]
        cand = cand_scr[:, ls].reshape(PEER_NCAND // 8, 8, LANES)
        cur = cand
        for r in range(PEER_TOPK):
            tau, _, cur = _drop_max(cur)
        m1 = t1_scr[0:1, ls]
        m2 = t2_scr[0:1, ls]
        top8 = jnp.broadcast_to(m1 + m2, (8, LANES))[None]
        tau8 = jnp.broadcast_to(tau, (8, LANES))[None]
        z = jnp.sum(jnp.sum(jnp.where(cand >= tau8, jnp.exp(cand - top8), 0.0), axis=0), axis=0, keepdims=True)
        s1g = s1l.reshape(PEER_NKEYS // 8, 8, LANES)
        kc = jnp.zeros(s1g.shape, F32)
        for b in range(PEER_TOPK):
            t2b = jnp.broadcast_to(t2_scr[b:b + 1, ls], (8, LANES))[None]
            kc = jnp.where(s1g + t2b >= tau8, float(b + 1), kc)
        kc_ref[0, :, ls] = kc.reshape(PEER_NKEYS, LANES)
        e1_ref[0, :, ls] = jnp.exp(s1l - (m1 + jnp.log(z)))
        e2_ref[0, :, ls] = jnp.exp(s2l - m2).astype(BF16)


def peer_query(x1, g, scale, shift, wq, keys, u, v, *, tm=512):
    s, d = x1.shape
    ne = u.shape[0]
    nsteps = (s // tm) * PEER_HEADS
    per_step = ne // nsteps
    group = max(1, LANES // per_step)
    rows = per_step * group
    assert rows * (nsteps // group) == ne and rows % LANES == 0
    vec = pl.BlockSpec((1, d), lambda i, h: (0, 0))
    stat = pl.BlockSpec((1, PEER_NKEYS, tm), lambda i, h: (h, 0, i))
    table = lambda dt: jax.ShapeDtypeStruct((PEER_HEADS, PEER_NKEYS, s), dt)
    eblk = lambda i, h: (i * PEER_HEADS + h) // group
    return pl.pallas_call(
        functools.partial(_peerq_kernel, group),
        grid=(s // tm, PEER_HEADS),
        in_specs=[pl.BlockSpec((tm, d), lambda i, h: (i, 0)), vec, vec, vec,
                  pl.BlockSpec((d, PEER_DQ), lambda i, h: (0, 0)),
                  pl.BlockSpec((d, PEER_DQ), lambda i, h: (0, jnp.minimum(h + 1, PEER_HEADS - 1))),
                  pl.BlockSpec((1, 2, PEER_NKEYS, PEER_DQ // 2), lambda i, h: (h, 0, 0, 0)),
                  pl.BlockSpec((rows, d), lambda i, h: (eblk(i, h), 0)),
                  pl.BlockSpec((rows, d), lambda i, h: (eblk(i, h), 0))],
        out_specs=[pl.BlockSpec((d, tm), lambda i, h: (0, i)), stat, stat, stat, stat,
                   pl.BlockSpec((rows, d), lambda i, h: (eblk(i, h), 0)),
                   pl.BlockSpec((d, rows), lambda i, h: (0, eblk(i, h)))],
        out_shape=[jax.ShapeDtypeStruct((d, s), BF16), table(F32), table(F32), table(BF16), table(BF16),
                   jax.ShapeDtypeStruct((ne, d), BF16), jax.ShapeDtypeStruct((d, ne), BF16)],
        scratch_shapes=[pltpu.VMEM((tm, d), BF16), pltpu.VMEM((2, tm, PEER_DQ), BF16),
                        pltpu.VMEM((PEER_NKEYS, tm), F32), pltpu.VMEM((PEER_NKEYS, tm), F32),
                        pltpu.VMEM((PEER_TOPK, tm), F32), pltpu.VMEM((PEER_TOPK, tm), F32),
                        pltpu.VMEM((PEER_NCAND, tm), F32)],
        compiler_params=_params(("arbitrary", "arbitrary")),
        name="peerq",
    )(x1, g, scale, shift, wq, wq, keys, u, v)


_GELU_C = 0.7978845608028654


def _gelu_tanh(x):
    k1 = -2.0 * _GELU_C * LOG2E
    z2 = x * (k1 + (k1 * 0.044715) * (x * x))
    return x / (1.0 + jnp.exp2(z2))


def _peer_kernel(nblk, h2t_ref, u_ref, vt_ref, kc_ref, e1_ref, l2_ref, e2_ref, o_ref, a_scr, w_scr):
    t = pl.program_id(0)
    tm = h2t_ref.shape[1]
    tn = u_ref.shape[0]
    rows = 16
    nrow = tn // PEER_NKEYS
    cur = t % 2

    @pl.when(jnp.maximum(t - 1, 0) % nblk == 0)
    def _():
        o_ref[...] = jnp.zeros_like(o_ref)

    @pl.when(t == 0)
    def _():
        w_scr[1] = jnp.zeros(w_scr.shape[1:], BF16)

    for e0 in range(0, tn, ACT_PIECE):
        a_scr[e0:e0 + ACT_PIECE, :] = _dot(u_ref[e0:e0 + ACT_PIECE, :], h2t_ref[...])
    o_ref[...] += _dot(vt_ref[...], w_scr[1 - cur])
    base = lax.shift_right_logical(t, 30) * rows

    for c in range(nrow):
        i1 = (t % nblk) * nrow + c
        krow = [jnp.broadcast_to(kc_ref[h, pl.ds(i1, 1), :], (rows, tm)).astype(BF16) for h in range(PEER_HEADS)]
        e1row = [jnp.broadcast_to(e1_ref[h, pl.ds(i1, 1), :], (rows, tm)).astype(BF16) for h in range(PEER_HEADS)]
        for r0 in range(0, PEER_NKEYS, rows):
            gate = None
            for h in range(PEER_HEADS):
                gv = jnp.where(l2_ref[h, r0:r0 + rows, :] < krow[h], e2_ref[h, r0:r0 + rows, :] * e1row[h],
                               jnp.zeros((), BF16))
                gate = gv if gate is None else gate + gv
            rr = c * PEER_NKEYS + r0
            a_rows = a_scr[pl.ds(pl.multiple_of(base + rr, rows), rows), :]
            w_scr[cur, rr:rr + rows, :] = _gelu_tanh(a_rows.astype(BF16)) * gate


def peer_experts(h2t, u, vt, kc, e1, l2, e2, *, tm=512, tn=512):
    d, s = h2t.shape
    ne = u.shape[0]
    nblk, ntile = ne // tn, s // tm
    once = pl.Buffered(1)
    tile_a = lambda t: jnp.minimum(t // nblk, ntile - 1)
    prev = lambda t: jnp.maximum(t - 1, 0)
    table = pl.BlockSpec((PEER_HEADS, PEER_NKEYS, tm), lambda t: (0, 0, tile_a(t)), pipeline_mode=once)
    return pl.pallas_call(
        functools.partial(_peer_kernel, nblk),
        grid=(ntile * nblk + 1,),
        in_specs=[pl.BlockSpec((d, tm), lambda t: (0, tile_a(t)), pipeline_mode=once),
                  pl.BlockSpec((tn, d), lambda t: (t % nblk, 0)),
                  pl.BlockSpec((d, tn), lambda t: (0, prev(t) % nblk)),
                  table, table, table, table],
        out_specs=pl.BlockSpec((d, tm), lambda t: (0, prev(t) // nblk)),
        out_shape=jax.ShapeDtypeStruct((d, s), F32),
        scratch_shapes=[pltpu.VMEM((tn, tm), F32), pltpu.VMEM((2, tn, tm), BF16)],
        compiler_params=_params(("arbitrary",)),
        name="peer",
    )(h2t, u, vt, kc, e1, l2, e2)


def _fin_kernel(x_ref, pt_ref, g2_ref, fg_ref, o_ref):
    x2 = x_ref[...] + g2_ref[...] * pt_ref[...].T
    ms = jnp.mean(x2 * x2, axis=-1, keepdims=True)
    o_ref[...] = x2 * lax.rsqrt(ms + EPS) * fg_ref[...]


def final_norm(x1, peer_t, gate2, final_g, *, tm=256):
    s, d = x1.shape
    vec = pl.BlockSpec((1, d), lambda i: (0, 0))
    return pl.pallas_call(
        _fin_kernel,
        grid=(s // tm,),
        in_specs=[pl.BlockSpec((tm, d), lambda i: (i, 0)),
                  pl.BlockSpec((d, tm), lambda i: (0, i)), vec, vec],
        out_specs=pl.BlockSpec((tm, d), lambda i: (i, 0)),
        out_shape=jax.ShapeDtypeStruct((s, d), F32),
        compiler_params=_params(("arbitrary",)),
        name="fin",
    )(x1, peer_t, gate2, final_g)


def _layer(x, mod, positions, norm1_g, w_in, b_forget, ret_gn_g, fox_norm_g, w_out, norm2_g,
           w_peer_q, peer_sub_keys, peer_u, peer_v, *, bq=2048, bk=512):
    s, d = x.shape
    shift1, scale1, gate1, shift2, scale2, gate2 = [mod[:, k * d:(k + 1) * d] for k in range(6)]
    row = lambda v: v.reshape(1, -1)

    w_t = w_in.T
    w_main = cast_bf16(w_t, rows=IN_MAIN, transpose=True, br=2048, bc=512, name="cast_in")
    n_ff = w_in.shape[1] - IN_MAIN
    w_ff = tail_weight(w_t, IN_MAIN, n_ff)
    pf32, kb, vt3, ff = in_proj(x, row(norm1_g), scale1, shift1, w_main, w_ff, tm=bk)

    b128 = jnp.pad(b_forget, (0, LANES - n_ff)).reshape(1, LANES)
    caug = cum_gate(ff, b128)
    y_ret = retention(pf32, positions.reshape(s, 1), row(ret_gn_g))
    y_fox = fox_attention(pf32, kb, caug, vt3, bq=bq)
    x1 = out_proj(y_ret, y_fox, row(fox_norm_g), cast_bf16(w_out, name="cast_out"), x, gate1)

    h2t, kc, e1, l2, e2, ub, vtb = peer_query(x1, row(norm2_g), scale2, shift2, cast_bf16(w_peer_q, name="cast_q"),
                                              peer_sub_keys.astype(BF16), peer_u, peer_v)
    peer_t = peer_experts(h2t, ub, vtb, kc, e1, l2, e2)
    return x1, peer_t, gate2


def kernel(x, c, positions, w_ada, b_ada, norm1_g, w_in, b_forget, ret_gn_g, fox_norm_g, w_out,
           norm2_g, w_peer_q, peer_sub_keys, peer_u, peer_v, final_g):
    b, s, d = x.shape
    assert b == 1 and w_ada.shape[0] == 1, "one sequence, one layer"
    c8 = jnp.broadcast_to(c, (8, d))
    mod = ada_mod(c8, w_ada[0], b_ada[0].reshape(1, -1))[0:1]
    x1, peer_t, gate2 = _layer(x[0], mod, positions[0], norm1_g[0], w_in[0], b_forget[0], ret_gn_g[0],
                               fox_norm_g[0], w_out[0], norm2_g[0], w_peer_q[0], peer_sub_keys[0],
                               peer_u[0], peer_v[0])
    return final_norm(x1, peer_t, gate2, final_g.reshape(1, -1))[None]
```

```python
import functools

import numpy as np
import jax
import jax.numpy as jnp
from jax import lax
from jax.experimental import pallas as pl
from jax.experimental.pallas import tpu as pltpu

F32 = jnp.float32
BF16 = jnp.bfloat16

LANES = 128
RET_HEADS = 8
RET_DK = 128
RET_DV = 256
RET_QK = RET_HEADS * RET_DK
RET_WIDTH = RET_HEADS * RET_DV
RET_CHUNK = 128
FOX_HEADS = 16
FOX_DH = 128
FOX_WIDTH = FOX_HEADS * FOX_DH
IN_MAIN = 2 * RET_QK + 2 * RET_WIDTH + 3 * FOX_WIDTH
N_F32_COLS = 2 * RET_QK + 2 * RET_WIDTH + FOX_WIDTH
ROPE_BASE = 10000.0
PEER_HEADS = 8
PEER_NKEYS = 128
PEER_DQ = 256
PEER_TOPK = 16
PEER_NCAND = 80
ACT_PIECE = 128
EPS = 1e-6
NEG_BIG = -1e30
LOG2E = 1.4426950408889634
VMEM_LIMIT = 56 * 1024 * 1024

_NT = (((1,), (1,)), ((), ()))
_TN = (((0,), (0,)), ((), ()))


def _params(sem):
    return pltpu.CompilerParams(dimension_semantics=sem, vmem_limit_bytes=VMEM_LIMIT)


def _dot(a, b):
    return jnp.dot(a, b, preferred_element_type=F32)


def _dot_nt(a, b):
    return lax.dot_general(a, b, _NT, preferred_element_type=F32)


def _ada_kernel(c_ref, w_ref, b_ref, o_ref):
    c = c_ref[...]
    ca = (c / (1.0 + jnp.exp(-c))).astype(BF16)
    o_ref[...] = _dot(ca, w_ref[...].astype(BF16)) + b_ref[...]


def ada_mod(c8, w, b, *, bn=512):
    d, n = w.shape
    return pl.pallas_call(
        _ada_kernel,
        grid=(n // bn,),
        in_specs=[pl.BlockSpec((8, d), lambda j: (0, 0)),
                  pl.BlockSpec((d, bn), lambda j: (0, j)),
                  pl.BlockSpec((1, bn), lambda j: (0, j))],
        out_specs=pl.BlockSpec((8, bn), lambda j: (0, j)),
        out_shape=jax.ShapeDtypeStruct((8, n), F32),
        compiler_params=_params(("arbitrary",)),
        name="ada",
    )(c8, w, b)


def _cast_kernel(x_ref, o_ref):
    o_ref[...] = x_ref[...].astype(BF16)


def _cast_t_kernel(x_ref, o_ref):
    o_ref[...] = x_ref[...].T.astype(BF16)


def cast_bf16(x, *, rows=None, cols=None, br=512, bc=2048, transpose=False, name="cast"):
    r, c = x.shape
    r = r if rows is None else rows
    c = c if cols is None else cols
    br, bc = min(br, r), min(bc, c)
    if transpose:
        return pl.pallas_call(
            _cast_t_kernel, grid=(r // br, c // bc),
            in_specs=[pl.BlockSpec((br, bc), lambda i, j: (i, j))],
            out_specs=pl.BlockSpec((bc, br), lambda i, j: (j, i)),
            out_shape=jax.ShapeDtypeStruct((c, r), BF16),
            compiler_params=_params(("arbitrary", "arbitrary")), name=name)(x)
    return pl.pallas_call(
        _cast_kernel, grid=(r // br, c // bc),
        in_specs=[pl.BlockSpec((br, bc), lambda i, j: (i, j))],
        out_specs=pl.BlockSpec((br, bc), lambda i, j: (i, j)),
        out_shape=jax.ShapeDtypeStruct((r, c), BF16),
        compiler_params=_params(("arbitrary", "arbitrary")), name=name)(x)


def _tail_kernel(x_ref, o_ref):
    o_ref[...] = jnp.zeros_like(o_ref)
    o_ref[:, 0:x_ref.shape[0]] = x_ref[...].T.astype(BF16)


def tail_weight(w_t, row0, nrows):
    d = w_t.shape[1]
    assert row0 % nrows == 0 and nrows % 8 == 0
    return pl.pallas_call(
        _tail_kernel, grid=(1,),
        in_specs=[pl.BlockSpec((nrows, d), lambda i: (row0 // nrows, 0))],
        out_specs=pl.BlockSpec((d, LANES), lambda i: (0, 0)),
        out_shape=jax.ShapeDtypeStruct((d, LANES), BF16),
        compiler_params=_params(("arbitrary",)), name="cast_tail")(w_t)


def _norm_mod(x, g, scale, shift):
    ms = jnp.mean(x * x, axis=-1, keepdims=True)
    y = x * lax.rsqrt(ms + EPS) * g
    return y * (1.0 + scale) + shift


def _inpj_kernel(nf, nk, x_ref, g_ref, sc_ref, sh_ref, w_ref, wf_ref, o_ref, ok_ref, ovt_ref, of_ref, h_scr):
    j = pl.program_id(1)

    @pl.when(j == 0)
    def _():
        hb = _norm_mod(x_ref[...], g_ref[...], sc_ref[...], sh_ref[...]).astype(BF16)
        h_scr[...] = hb
        of_ref[...] = _dot(hb, wf_ref[...])

    tn = w_ref.shape[1]
    piece = tn // 2

    def pieces(store):
        for c0 in range(0, tn, piece):
            store(c0, _dot(h_scr[...], w_ref[:, c0:c0 + piece]))

    @pl.when(j < nf)
    def _():
        def store(c0, r):
            o_ref[:, c0:c0 + piece] = r
        pieces(store)

    @pl.when(jnp.logical_and(j >= nf, j < nf + nk))
    def _():
        def store(c0, r):
            ok_ref[:, c0:c0 + piece] = r.astype(BF16)
        pieces(store)

    @pl.when(j >= nf + nk)
    def _():
        def store(c0, r):
            ovt_ref[0, c0:c0 + piece, :] = r.T.astype(BF16)
        pieces(store)


def in_proj(x, g, scale, shift, w_main, w_ff, *, tm=512, tn=1024):
    s, d = x.shape
    n = w_main.shape[1]
    nf, nk = N_F32_COLS // tn, FOX_WIDTH // tn
    vec = pl.BlockSpec((1, d), lambda i, j: (0, 0))
    return pl.pallas_call(
        functools.partial(_inpj_kernel, nf, nk),
        grid=(s // tm, n // tn),
        in_specs=[pl.BlockSpec((tm, d), lambda i, j: (i, 0)), vec, vec, vec,
                  pl.BlockSpec((d, tn), lambda i, j: (0, j)),
                  pl.BlockSpec((d, LANES), lambda i, j: (0, 0))],
        out_specs=[pl.BlockSpec((tm, tn), lambda i, j: (i, jnp.minimum(j, nf - 1))),
                   pl.BlockSpec((tm, tn), lambda i, j: (i, jnp.clip(j - nf, 0, nk - 1))),
                   pl.BlockSpec((1, tn, tm), lambda i, j: (i, jnp.clip(j - nf - nk, 0, nk - 1), 0)),
                   pl.BlockSpec((tm, LANES), lambda i, j: (i, 0))],
        out_shape=[jax.ShapeDtypeStruct((s, N_F32_COLS), F32),
                   jax.ShapeDtypeStruct((s, FOX_WIDTH), BF16),
                   jax.ShapeDtypeStruct((s // tm, FOX_WIDTH, tm), BF16),
                   jax.ShapeDtypeStruct((s, LANES), F32)],
        scratch_shapes=[pltpu.VMEM((tm, d), BF16)],
        compiler_params=_params(("arbitrary", "arbitrary")),
        name="inpj",
    )(x, g, scale, shift, w_main, w_ff)


def _split3(v):
    hi = v.astype(BF16)
    r1 = v - hi.astype(F32)
    mid = r1.astype(BF16)
    lo = (r1 - mid.astype(F32)).astype(BF16)
    return hi, mid, lo


def _cum_place():
    m = np.zeros((3 * LANES, FOX_HEADS * LANES), np.float32)
    for p in range(3):
        for h in range(FOX_HEADS):
            m[p * LANES + h, h * LANES + p] = 1.0
    return m


def _cum_kernel(ff_ref, b_ref, place_ref, o_ref, carry):
    @pl.when(pl.program_id(0) == 0)
    def _():
        carry[...] = jnp.zeros_like(carry)

    z = ff_ref[...] + b_ref[...]
    logf = jnp.minimum(z, 0.0) - jnp.log(1.0 + jnp.exp(-jnp.abs(z)))
    tc = z.shape[0]
    row = lax.broadcasted_iota(jnp.int32, (tc, tc), 0)
    col = lax.broadcasted_iota(jnp.int32, (tc, tc), 1)
    tri = jnp.where(row >= col, 1.0, 0.0).astype(BF16)
    hi, mid, lo = _split3(logf)
    cum = _dot(tri, hi) + _dot(tri, mid) + _dot(tri, lo) + carry[...]
    carry[...] = cum[tc - 1:tc, :]
    pieces = jnp.concatenate(_split3(cum * LOG2E), axis=1)
    o_ref[...] = _dot(pieces, place_ref[...]).astype(BF16)


def cum_gate(ff, b128, *, tc=256):
    s = ff.shape[0]
    place = jnp.asarray(_cum_place(), BF16)
    return pl.pallas_call(
        _cum_kernel,
        grid=(s // tc,),
        in_specs=[pl.BlockSpec((tc, LANES), lambda i: (i, 0)),
                  pl.BlockSpec((1, LANES), lambda i: (0, 0)),
                  pl.BlockSpec(place.shape, lambda i: (0, 0))],
        out_specs=pl.BlockSpec((tc, FOX_HEADS * LANES), lambda i: (i, 0)),
        out_shape=jax.ShapeDtypeStruct((s, FOX_HEADS * LANES), BF16),
        scratch_shapes=[pltpu.VMEM((1, LANES), F32)],
        compiler_params=_params(("arbitrary",)),
        name="cum",
    )(ff, b128, place)


def _ret_consts():
    h = np.arange(RET_HEADS, dtype=np.float32)
    log_g = np.log1p(-np.exp2(-5.0 - h)).astype(np.float32)
    i = np.arange(RET_CHUNK, dtype=np.float32)
    diff = i[:, None] - i[None, :]
    dmat = np.where(diff >= 0, np.exp(np.maximum(diff, 0.0)[None] * log_g[:, None, None]), 0.0)
    kdec = np.exp((RET_CHUNK - 1.0 - i)[None, :] * log_g[:, None])
    qdec = np.exp((i + 1.0)[None, :] * log_g[:, None])
    cdec = np.exp(RET_CHUNK * log_g)
    kdec = np.broadcast_to(kdec[:, :, None], (RET_HEADS, RET_CHUNK, RET_DK))
    qdec = np.broadcast_to(qdec[:, :, None], (RET_HEADS, RET_CHUNK, RET_DK))
    half = RET_DK // 2
    inv = (ROPE_BASE ** (-np.arange(half, dtype=np.float32) / half)).astype(np.float32)
    inv_full = np.concatenate([inv, inv])[None, :]
    sign = np.concatenate([-np.ones(half, np.float32), np.ones(half, np.float32)])[None, :]
    return (dmat.astype(np.float32), np.ascontiguousarray(kdec, np.float32),
            np.ascontiguousarray(qdec, np.float32), [float(v) for v in cdec],
            inv_full.astype(np.float32), sign)


def _ret_kernel(cdec, rq_ref, rk_ref, rv_ref, rg_ref, pos_ref, inv_ref, sign_ref,
                dmat_ref, kdec_ref, qdec_ref, gn_ref, o_ref, state):
    @pl.when(pl.program_id(0) == 0)
    def _():
        state[...] = jnp.zeros_like(state)

    ang = pos_ref[...].astype(F32) * inv_ref[...]
    cosf = jnp.cos(ang)
    sinf = jnp.sin(ang) * sign_ref[...]
    kscale = RET_DK ** -0.5
    for h in range(RET_HEADS):
        q = rq_ref[:, h * RET_DK:(h + 1) * RET_DK]
        k = rk_ref[:, h * RET_DK:(h + 1) * RET_DK]
        qr = q * cosf + pltpu.roll(q, RET_DK // 2, 1) * sinf
        kr = (k * cosf + pltpu.roll(k, RET_DK // 2, 1) * sinf) * kscale
        vb = rv_ref[:, h * RET_DV:(h + 1) * RET_DV].astype(BF16)
        scores = _dot_nt(qr.astype(BF16), kr.astype(BF16)) * dmat_ref[h]
        intra = _dot(scores.astype(BF16), vb)
        st = state[h]
        cross = _dot((qr * qdec_ref[h]).astype(BF16), st.astype(BF16))
        kd = (kr * kdec_ref[h]).astype(BF16)
        kv = lax.dot_general(kd, vb, _TN, preferred_element_type=F32)
        state[h] = st * cdec[h] + kv
        y = intra + cross
        mu = jnp.mean(y, axis=-1, keepdims=True)
        dlt = y - mu
        var = jnp.mean(dlt * dlt, axis=-1, keepdims=True)
        g = rg_ref[:, h * RET_DV:(h + 1) * RET_DV]
        yn = dlt * lax.rsqrt(var + EPS) * gn_ref[:, h * RET_DV:(h + 1) * RET_DV] * (g / (1.0 + jnp.exp(-g)))
        o_ref[:, h * RET_DV:(h + 1) * RET_DV] = yn.astype(BF16)


def retention(proj, pos_col, gn_g):
    s = proj.shape[0]
    c = RET_CHUNK
    dmat, kdec, qdec, cdec, inv_full, sign = _ret_consts()
    full3 = lambda n: (0, 0, 0)
    return pl.pallas_call(
        functools.partial(_ret_kernel, cdec),
        grid=(s // c,),
        in_specs=[pl.BlockSpec((c, RET_QK), lambda n: (n, 0)),
                  pl.BlockSpec((c, RET_QK), lambda n: (n, 1)),
                  pl.BlockSpec((c, RET_WIDTH), lambda n: (n, 1)),
                  pl.BlockSpec((c, RET_WIDTH), lambda n: (n, 2)),
                  pl.BlockSpec((c, 1), lambda n: (n, 0)),
                  pl.BlockSpec((1, RET_DK), lambda n: (0, 0)),
                  pl.BlockSpec((1, RET_DK), lambda n: (0, 0)),
                  pl.BlockSpec((RET_HEADS, c, c), full3),
                  pl.BlockSpec((RET_HEADS, c, RET_DK), full3),
                  pl.BlockSpec((RET_HEADS, c, RET_DK), full3),
                  pl.BlockSpec((1, RET_WIDTH), lambda n: (0, 0))],
        out_specs=pl.BlockSpec((c, RET_WIDTH), lambda n: (n, 0)),
        out_shape=jax.ShapeDtypeStruct((s, RET_WIDTH), BF16),
        scratch_shapes=[pltpu.VMEM((RET_HEADS, RET_DK, RET_DV), F32)],
        compiler_params=_params(("arbitrary",)),
        name="ret",
    )(proj, proj, proj, proj, pos_col, jnp.asarray(inv_full), jnp.asarray(sign),
      jnp.asarray(dmat), jnp.asarray(kdec), jnp.asarray(qdec), gn_g)


def _fox_kernel(q_ref, k_ref, c_ref, vt_ref, o_ref, m_scr, l_scr, acc_scr, mx_scr, *bufs):
    i = pl.program_id(1)
    bq = q_ref.shape[0]
    bk = bufs[0].shape[0]
    nd = len(bufs)
    assert nd * bk == bq
    sub = 8
    lane = lax.broadcasted_iota(jnp.int32, (bq, FOX_DH), 1)
    qa = jnp.where(lane < 3, -1.0, 0.0).astype(BF16)
    q2 = jnp.concatenate([(q_ref[...] * (FOX_DH ** -0.5 * LOG2E)).astype(BF16), qa], axis=1)
    m_scr[...] = jnp.full_like(m_scr, NEG_BIG)
    l_scr[...] = jnp.zeros_like(l_scr)
    acc_scr[...] = jnp.zeros_like(acc_scr)
    ones_rows = jnp.where(lax.broadcasted_iota(jnp.int32, (2 * sub, bk), 0) == 0, 1.0, 0.0).astype(BF16)

    def logits(j, slot, c0=0):
        off = pl.multiple_of(j * bk, bk)
        k2 = jnp.concatenate([k_ref[pl.ds(off, bk), :], c_ref[pl.ds(off, bk), :]], axis=1)
        st = _dot_nt(k2, q2[c0:bq, :])
        bufs[slot][:, c0:bq] = st
        mx_scr[slot, :, c0:bq] = jnp.max(jnp.max(st.reshape(bk // sub, sub, bq - c0), axis=0), axis=0, keepdims=True)

    def update_cols(j, slot, c0, c1, diagonal):
        w = c1 - c0
        st = bufs[slot][:, c0:c1].reshape(bk // sub, sub, w)
        if diagonal:
            kid = lax.broadcasted_iota(jnp.int32, (bk, bk), 0).reshape(bk // sub, sub, bk)
            qid = lax.broadcasted_iota(jnp.int32, (bk, bk), 1).reshape(bk // sub, sub, bk)
            st = jnp.where(kid <= qid, st, NEG_BIG)
            blk_max = jnp.max(jnp.max(st, axis=0), axis=0, keepdims=True)
        else:
            blk_max = mx_scr[slot, :, c0:c1]
        m_old = m_scr[:, c0:c1]
        m_new = jnp.maximum(m_old, blk_max)
        alpha = jnp.exp2(m_old - m_new)
        p = jnp.exp2(st - jnp.broadcast_to(m_new, (sub, w))[None])
        pv = _dot(jnp.concatenate([vt_ref[j], ones_rows], axis=0), p.reshape(bk, w).astype(BF16))
        l_scr[:, c0:c1] = alpha * l_scr[:, c0:c1] + pv[FOX_DH:FOX_DH + 1, :]
        a8 = jnp.broadcast_to(alpha, (sub, w))[None]
        acc = acc_scr[:, c0:c1].reshape(FOX_DH // sub, sub, w)
        acc_scr[:, c0:c1] = (a8 * acc).reshape(FOX_DH, w) + pv[0:FOX_DH, :]
        m_scr[:, c0:c1] = m_new

    def update(j, slot, c0=0, diagonal=False):
        if diagonal:
            update_cols(j, slot, c0, c0 + bk, True)
            c0 += bk
        if c0 < bq:
            update_cols(j, slot, c0, bq, False)

    ahead = 2
    for k in range(ahead):
        logits(k, k)

    def body(jj, carry):
        for k in range(nd):
            logits(nd * jj + k + ahead, (k + ahead) % nd)
            update(nd * jj + k, k)
        return carry

    lax.fori_loop(0, i, body, 0)
    for k in range(nd):
        if k + ahead < nd:
            logits(nd * i + k + ahead, k + ahead, (k + ahead) * bk)
        update(nd * i + k, k, k * bk, diagonal=True)
    o_ref[...] = (acc_scr[...] / l_scr[...]).T


def fox_attention(pf32, kb, caug, vt3, *, bq=2048):
    s = pf32.shape[0]
    bk = vt3.shape[2]
    assert vt3.shape == (s // bk, FOX_WIDTH, bk) and bq % (2 * bk) == 0
    qcol0 = (N_F32_COLS - FOX_WIDTH) // FOX_DH
    return pl.pallas_call(
        _fox_kernel,
        grid=(FOX_HEADS, s // bq),
        in_specs=[pl.BlockSpec((bq, FOX_DH), lambda h, i: (i, qcol0 + h)),
                  pl.BlockSpec((s, FOX_DH), lambda h, i: (0, h)),
                  pl.BlockSpec((s, LANES), lambda h, i: (0, h)),
                  pl.BlockSpec((s // bk, FOX_DH, bk), lambda h, i: (0, h, 0))],
        out_specs=pl.BlockSpec((bq, FOX_DH), lambda h, i: (i, h)),
        out_shape=jax.ShapeDtypeStruct((s, FOX_WIDTH), F32),
        scratch_shapes=[pltpu.VMEM((1, bq), F32), pltpu.VMEM((1, bq), F32), pltpu.VMEM((FOX_DH, bq), F32),
                        pltpu.VMEM((bq // bk, 1, bq), F32)]
        + [pltpu.VMEM((bk, bq), F32)] * (bq // bk),
        compiler_params=_params(("arbitrary", "arbitrary")),
        name="fox",
    )(pf32, kb, caug, vt3)


def _outp_kernel(yr_ref, yf_ref, fg_ref, w_ref, x_ref, g1_ref, o_ref, y_scr):
    @pl.when(pl.program_id(1) == 0)
    def _():
        yf = yf_ref[...]
        ms = jnp.mean(yf * yf, axis=-1, keepdims=True)
        yn = yf * lax.rsqrt(ms + EPS) * fg_ref[...]
        y_scr[:, 0:RET_WIDTH] = yr_ref[...]
        y_scr[:, RET_WIDTH:RET_WIDTH + FOX_WIDTH] = yn.astype(BF16)

    tn = w_ref.shape[1]
    piece = tn // 2
    for c0 in range(0, tn, piece):
        cs = slice(c0, c0 + piece)
        o_ref[:, cs] = x_ref[:, cs] + g1_ref[:, cs] * _dot(y_scr[...], w_ref[:, cs])


def out_proj(y_ret, y_fox, fox_g, w_out, x, gate1, *, tm=512, tn=1024):
    s, d = x.shape
    kdim = RET_WIDTH + FOX_WIDTH
    return pl.pallas_call(
        _outp_kernel,
        grid=(s // tm, d // tn),
        in_specs=[pl.BlockSpec((tm, RET_WIDTH), lambda i, j: (i, 0)),
                  pl.BlockSpec((tm, FOX_WIDTH), lambda i, j: (i, 0)),
                  pl.BlockSpec((1, FOX_WIDTH), lambda i, j: (0, 0)),
                  pl.BlockSpec((kdim, tn), lambda i, j: (0, j)),
                  pl.BlockSpec((tm, tn), lambda i, j: (i, j)),
                  pl.BlockSpec((1, tn), lambda i, j: (0, j))],
        out_specs=pl.BlockSpec((tm, tn), lambda i, j: (i, j)),
        out_shape=jax.ShapeDtypeStruct((s, d), F32),
        scratch_shapes=[pltpu.VMEM((tm, kdim), BF16)],
        compiler_params=_params(("arbitrary", "arbitrary")),
        name="outp",
    )(y_ret, y_fox, fox_g, w_out, x, gate1)


def _drop_max(cur):
    mx = jnp.max(jnp.max(cur, axis=0), axis=0, keepdims=True)
    hit = cur == jnp.broadcast_to(mx, cur.shape[1:])[None]
    return mx, hit, jnp.where(hit, -jnp.inf, cur)


def _top_rows(s, t_scr, ls):
    cur = s.reshape(s.shape[0] // 8, 8, s.shape[1])
    level = jnp.full(cur.shape, float(PEER_TOPK), F32)
    for r in range(PEER_TOPK):
        mx, hit, cur = _drop_max(cur)
        t_scr[r:r + 1, ls] = mx
        level = jnp.where(hit, float(r), level)
    return level.reshape(s.shape)


def _peerq_kernel(group, x_ref, g_ref, sc_ref, sh_ref, wq0_ref, wqn_ref, keys_ref, u_ref, v_ref,
                  h2t_ref, kc_ref, e1_ref, l2_ref, e2_ref, ub_ref, vtb_ref,
                  h_scr, qb_scr, s1_scr, s2_scr, t1_scr, t2_scr, cand_scr):
    head = pl.program_id(1)
    step = pl.program_id(0) * pl.num_programs(1) + head
    cur = head % 2

    @pl.when(step % group == 0)
    def _():
        ub_ref[...] = u_ref[...].astype(BF16)
        vtb_ref[...] = v_ref[...].T.astype(BF16)

    @pl.when(pl.program_id(1) == 0)
    def _():
        h = _norm_mod(x_ref[...], g_ref[...], sc_ref[...], sh_ref[...])
        h_scr[...] = h.astype(BF16)
        h2t_ref[...] = h.T.astype(BF16)
        qb_scr[0] = _dot(h_scr[...], wq0_ref[...]).astype(BF16)

    half = PEER_DQ // 2
    qb = qb_scr[cur]
    s1 = _dot_nt(keys_ref[0, 0], qb[:, 0:half])
    s2 = _dot_nt(keys_ref[0, 1], qb[:, half:PEER_DQ])
    s1_scr[...] = s1
    s2_scr[...] = s2
    qb_scr[1 - cur] = _dot(h_scr[...], wqn_ref[...]).astype(BF16)
    for g in range(s1.shape[1] // LANES):
        ls = slice(g * LANES, (g + 1) * LANES)
        s1l = s1_scr[:, ls]
        s2l = s2_scr[:, ls]
        _top_rows(s1l, t1_scr, ls)
        l2_ref[0, :, ls] = _top_rows(s2l, t2_scr, ls).astype(BF16)
        t1_lo = t1_scr[0:8, ls]
        for b in range(8):
            cand_scr[b * 8:(b + 1) * 8, ls] = t1_lo + t2_scr[b:b + 1, ls]
        cand_scr[64:72, ls] = t1_scr[0:1, ls] + t2_scr[8:16, ls]
        cand_scr[72:80, ls] = t1_scr[8:16, ls] + t2_scr[0:1, ls]
        cand = cand_scr[:, ls].reshape(PEER_NCAND // 8, 8, LANES)
        cur = cand
        for r in range(PEER_TOPK):
            tau, _, cur = _drop_max(cur)
        m1 = t1_scr[0:1, ls]
        m2 = t2_scr[0:1, ls]
        top8 = jnp.broadcast_to(m1 + m2, (8, LANES))[None]
        tau8 = jnp.broadcast_to(tau, (8, LANES))[None]
        z = jnp.sum(jnp.sum(jnp.where(cand >= tau8, jnp.exp(cand - top8), 0.0), axis=0), axis=0, keepdims=True)
        s1g = s1l.reshape(PEER_NKEYS // 8, 8, LANES)
        kc = jnp.zeros(s1g.shape, F32)
        for b in range(PEER_TOPK):
            t2b = jnp.broadcast_to(t2_scr[b:b + 1, ls], (8, LANES))[None]
            kc = jnp.where(s1g + t2b >= tau8, float(b + 1), kc)
        kc_ref[0, :, ls] = kc.reshape(PEER_NKEYS, LANES)
        e1_ref[0, :, ls] = jnp.exp(s1l - (m1 + jnp.log(z)))
        e2_ref[0, :, ls] = jnp.exp(s2l - m2).astype(BF16)


def peer_query(x1, g, scale, shift, wq, keys, u, v, *, tm=512):
    s, d = x1.shape
    ne = u.shape[0]
    nsteps = (s // tm) * PEER_HEADS
    per_step = ne // nsteps
    group = max(1, LANES // per_step)
    rows = per_step * group
    assert rows * (nsteps // group) == ne and rows % LANES == 0
    vec = pl.BlockSpec((1, d), lambda i, h: (0, 0))
    stat = pl.BlockSpec((1, PEER_NKEYS, tm), lambda i, h: (h, 0, i))
    table = lambda dt: jax.ShapeDtypeStruct((PEER_HEADS, PEER_NKEYS, s), dt)
    eblk = lambda i, h: (i * PEER_HEADS + h) // group
    return pl.pallas_call(
        functools.partial(_peerq_kernel, group),
        grid=(s // tm, PEER_HEADS),
        in_specs=[pl.BlockSpec((tm, d), lambda i, h: (i, 0)), vec, vec, vec,
                  pl.BlockSpec((d, PEER_DQ), lambda i, h: (0, 0)),
                  pl.BlockSpec((d, PEER_DQ), lambda i, h: (0, jnp.minimum(h + 1, PEER_HEADS - 1))),
                  pl.BlockSpec((1, 2, PEER_NKEYS, PEER_DQ // 2), lambda i, h: (h, 0, 0, 0)),
                  pl.BlockSpec((rows, d), lambda i, h: (eblk(i, h), 0)),
                  pl.BlockSpec((rows, d), lambda i, h: (eblk(i, h), 0))],
        out_specs=[pl.BlockSpec((d, tm), lambda i, h: (0, i)), stat, stat, stat, stat,
                   pl.BlockSpec((rows, d), lambda i, h: (eblk(i, h), 0)),
                   pl.BlockSpec((d, rows), lambda i, h: (0, eblk(i, h)))],
        out_shape=[jax.ShapeDtypeStruct((d, s), BF16), table(F32), table(F32), table(BF16), table(BF16),
                   jax.ShapeDtypeStruct((ne, d), BF16), jax.ShapeDtypeStruct((d, ne), BF16)],
        scratch_shapes=[pltpu.VMEM((tm, d), BF16), pltpu.VMEM((2, tm, PEER_DQ), BF16),
                        pltpu.VMEM((PEER_NKEYS, tm), F32), pltpu.VMEM((PEER_NKEYS, tm), F32),
                        pltpu.VMEM((PEER_TOPK, tm), F32), pltpu.VMEM((PEER_TOPK, tm), F32),
                        pltpu.VMEM((PEER_NCAND, tm), F32)],
        compiler_params=_params(("arbitrary", "arbitrary")),
        name="peerq",
    )(x1, g, scale, shift, wq, wq, keys, u, v)


_GELU_C = 0.7978845608028654


def _gelu_tanh(x):
    k1 = -2.0 * _GELU_C * LOG2E
    z2 = x * (k1 + (k1 * 0.044715) * (x * x))
    return x / (1.0 + jnp.exp2(z2))


def _peer_kernel(nblk, h2t_ref, u_ref, vt_ref, kc_ref, e1_ref, l2_ref, e2_ref, o_ref, a_scr, w_scr):
    t = pl.program_id(0)
    tm = h2t_ref.shape[1]
    tn = u_ref.shape[0]
    rows = 16
    nrow = tn // PEER_NKEYS
    cur = t % 2

    @pl.when(jnp.maximum(t - 1, 0) % nblk == 0)
    def _():
        o_ref[...] = jnp.zeros_like(o_ref)

    @pl.when(t == 0)
    def _():
        w_scr[1] = jnp.zeros(w_scr.shape[1:], BF16)

    for e0 in range(0, tn, ACT_PIECE):
        a_scr[e0:e0 + ACT_PIECE, :] = _dot(u_ref[e0:e0 + ACT_PIECE, :], h2t_ref[...])
    o_ref[...] += _dot(vt_ref[...], w_scr[1 - cur])
    base = lax.shift_right_logical(t, 30) * rows

    for c in range(nrow):
        i1 = (t % nblk) * nrow + c
        krow = [jnp.broadcast_to(kc_ref[h, pl.ds(i1, 1), :], (rows, tm)).astype(BF16) for h in range(PEER_HEADS)]
        e1row = [jnp.broadcast_to(e1_ref[h, pl.ds(i1, 1), :], (rows, tm)).astype(BF16) for h in range(PEER_HEADS)]
        for r0 in range(0, PEER_NKEYS, rows):
            gate = None
            for h in range(PEER_HEADS):
                gv = jnp.where(l2_ref[h, r0:r0 + rows, :] < krow[h], e2_ref[h, r0:r0 + rows, :] * e1row[h],
                               jnp.zeros((), BF16))
                gate = gv if gate is None else gate + gv
            rr = c * PEER_NKEYS + r0
            a_rows = a_scr[pl.ds(pl.multiple_of(base + rr, rows), rows), :]
            w_scr[cur, rr:rr + rows, :] = _gelu_tanh(a_rows.astype(BF16)) * gate


def peer_experts(h2t, u, vt, kc, e1, l2, e2, *, tm=512, tn=512):
    d, s = h2t.shape
    ne = u.shape[0]
    nblk, ntile = ne // tn, s // tm
    once = pl.Buffered(1)
    tile_a = lambda t: jnp.minimum(t // nblk, ntile - 1)
    prev = lambda t: jnp.maximum(t - 1, 0)
    table = pl.BlockSpec((PEER_HEADS, PEER_NKEYS, tm), lambda t: (0, 0, tile_a(t)), pipeline_mode=once)
    return pl.pallas_call(
        functools.partial(_peer_kernel, nblk),
        grid=(ntile * nblk + 1,),
        in_specs=[pl.BlockSpec((d, tm), lambda t: (0, tile_a(t)), pipeline_mode=once),
                  pl.BlockSpec((tn, d), lambda t: (t % nblk, 0)),
                  pl.BlockSpec((d, tn), lambda t: (0, prev(t) % nblk)),
                  table, table, table, table],
        out_specs=pl.BlockSpec((d, tm), lambda t: (0, prev(t) // nblk)),
        out_shape=jax.ShapeDtypeStruct((d, s), F32),
        scratch_shapes=[pltpu.VMEM((tn, tm), F32), pltpu.VMEM((2, tn, tm), BF16)],
        compiler_params=_params(("arbitrary",)),
        name="peer",
    )(h2t, u, vt, kc, e1, l2, e2)


def _fin_kernel(x_ref, pt_ref, g2_ref, fg_ref, o_ref):
    x2 = x_ref[...] + g2_ref[...] * pt_ref[...].T
    ms = jnp.mean(x2 * x2, axis=-1, keepdims=True)
    o_ref[...] = x2 * lax.rsqrt(ms + EPS) * fg_ref[...]


def final_norm(x1, peer_t, gate2, final_g, *, tm=256):
    s, d = x1.shape
    vec = pl.BlockSpec((1, d), lambda i: (0, 0))
    return pl.pallas_call(
        _fin_kernel,
        grid=(s // tm,),
        in_specs=[pl.BlockSpec((tm, d), lambda i: (i, 0)),
                  pl.BlockSpec((d, tm), lambda i: (0, i)), vec, vec],
        out_specs=pl.BlockSpec((tm, d), lambda i: (i, 0)),
        out_shape=jax.ShapeDtypeStruct((s, d), F32),
        compiler_params=_params(("arbitrary",)),
        name="fin",
    )(x1, peer_t, gate2, final_g)


def _layer(x, mod, positions, norm1_g, w_in, b_forget, ret_gn_g, fox_norm_g, w_out, norm2_g,
           w_peer_q, peer_sub_keys, peer_u, peer_v, *, bq=2048, bk=512):
    s, d = x.shape
    shift1, scale1, gate1, shift2, scale2, gate2 = [mod[:, k * d:(k + 1) * d] for k in range(6)]
    row = lambda v: v.reshape(1, -1)

    w_t = w_in.T
    w_main = cast_bf16(w_t, rows=IN_MAIN, transpose=True, br=2048, bc=512, name="cast_in")
    n_ff = w_in.shape[1] - IN_MAIN
    w_ff = tail_weight(w_t, IN_MAIN, n_ff)
    pf32, kb, vt3, ff = in_proj(x, row(norm1_g), scale1, shift1, w_main, w_ff, tm=bk)

    b128 = jnp.pad(b_forget, (0, LANES - n_ff)).reshape(1, LANES)
    caug = cum_gate(ff, b128)
    y_ret = retention(pf32, positions.reshape(s, 1), row(ret_gn_g))
    y_fox = fox_attention(pf32, kb, caug, vt3, bq=bq)
    x1 = out_proj(y_ret, y_fox, row(fox_norm_g), cast_bf16(w_out, name="cast_out"), x, gate1)

    h2t, kc, e1, l2, e2, ub, vtb = peer_query(x1, row(norm2_g), scale2, shift2, cast_bf16(w_peer_q, name="cast_q"),
                                              peer_sub_keys.astype(BF16), peer_u, peer_v)
    peer_t = peer_experts(h2t, ub, vtb, kc, e1, l2, e2)
    return x1, peer_t, gate2


def kernel(x, c, positions, w_ada, b_ada, norm1_g, w_in, b_forget, ret_gn_g, fox_norm_g, w_out,
           norm2_g, w_peer_q, peer_sub_keys, peer_u, peer_v, final_g):
    b, s, d = x.shape
    assert b == 1 and w_ada.shape[0] == 1, "one sequence, one layer"
    c8 = jnp.broadcast_to(c, (8, d))
    mod = ada_mod(c8, w_ada[0], b_ada[0].reshape(1, -1))[0:1]
    x1, peer_t, gate2 = _layer(x[0], mod, positions[0], norm1_g[0], w_in[0], b_forget[0], ret_gn_g[0],
                               fox_norm_g[0], w_out[0], norm2_g[0], w_peer_q[0], peer_sub_keys[0],
                               peer_u[0], peer_v[0])
    return final_norm(x1, peer_t, gate2, final_g.reshape(1, -1))[None]
```

```python
import functools

import numpy as np
import jax
import jax.numpy as jnp
from jax import lax
from jax.experimental import pallas as pl
from jax.experimental.pallas import tpu as pltpu

F32 = jnp.float32
BF16 = jnp.bfloat16

LANES = 128
RET_HEADS = 8
RET_DK = 128
RET_DV = 256
RET_QK = RET_HEADS * RET_DK
RET_WIDTH = RET_HEADS * RET_DV
RET_CHUNK = 128
FOX_HEADS = 16
FOX_DH = 128
FOX_WIDTH = FOX_HEADS * FOX_DH
IN_MAIN = 2 * RET_QK + 2 * RET_WIDTH + 3 * FOX_WIDTH
N_F32_COLS = 2 * RET_QK + 2 * RET_WIDTH + FOX_WIDTH
ROPE_BASE = 10000.0
PEER_HEADS = 8
PEER_NKEYS = 128
PEER_DQ = 256
PEER_TOPK = 16
PEER_NCAND = 80
ACT_PIECE = 128
EPS = 1e-6
NEG_BIG = -1e30
LOG2E = 1.4426950408889634
VMEM_LIMIT = 60 * 1024 * 1024

_NT = (((1,), (1,)), ((), ()))
_TN = (((0,), (0,)), ((), ()))


def _params(sem):
    return pltpu.CompilerParams(dimension_semantics=sem, vmem_limit_bytes=VMEM_LIMIT)


def _dot(a, b):
    return jnp.dot(a, b, preferred_element_type=F32)


def _dot_nt(a, b):
    return lax.dot_general(a, b, _NT, preferred_element_type=F32)


def _ada_kernel(c_ref, w_ref, b_ref, o_ref):
    c = c_ref[...]
    ca = (c / (1.0 + jnp.exp(-c))).astype(BF16)
    o_ref[...] = _dot(ca, w_ref[...].astype(BF16)) + b_ref[...]


def ada_mod(c8, w, b, *, bn=512):
    d, n = w.shape
    return pl.pallas_call(
        _ada_kernel,
        grid=(n // bn,),
        in_specs=[pl.BlockSpec((8, d), lambda j: (0, 0)),
                  pl.BlockSpec((d, bn), lambda j: (0, j)),
                  pl.BlockSpec((1, bn), lambda j: (0, j))],
        out_specs=pl.BlockSpec((8, bn), lambda j: (0, j)),
        out_shape=jax.ShapeDtypeStruct((8, n), F32),
        compiler_params=_params(("arbitrary",)),
        name="ada",
    )(c8, w, b)


def _cast_kernel(x_ref, o_ref):
    o_ref[...] = x_ref[...].astype(BF16)


def _cast_t_kernel(x_ref, o_ref):
    o_ref[...] = x_ref[...].T.astype(BF16)


def cast_bf16(x, *, rows=None, cols=None, br=512, bc=2048, transpose=False, name="cast"):
    r, c = x.shape
    r = r if rows is None else rows
    c = c if cols is None else cols
    br, bc = min(br, r), min(bc, c)
    if transpose:
        return pl.pallas_call(
            _cast_t_kernel, grid=(r // br, c // bc),
            in_specs=[pl.BlockSpec((br, bc), lambda i, j: (i, j))],
            out_specs=pl.BlockSpec((bc, br), lambda i, j: (j, i)),
            out_shape=jax.ShapeDtypeStruct((c, r), BF16),
            compiler_params=_params(("arbitrary", "arbitrary")), name=name)(x)
    return pl.pallas_call(
        _cast_kernel, grid=(r // br, c // bc),
        in_specs=[pl.BlockSpec((br, bc), lambda i, j: (i, j))],
        out_specs=pl.BlockSpec((br, bc), lambda i, j: (i, j)),
        out_shape=jax.ShapeDtypeStruct((r, c), BF16),
        compiler_params=_params(("arbitrary", "arbitrary")), name=name)(x)


def _tail_kernel(x_ref, o_ref):
    o_ref[...] = jnp.zeros_like(o_ref)
    o_ref[:, 0:x_ref.shape[0]] = x_ref[...].T.astype(BF16)


def tail_weight(w_t, row0, nrows):
    d = w_t.shape[1]
    assert row0 % nrows == 0 and nrows % 8 == 0
    return pl.pallas_call(
        _tail_kernel, grid=(1,),
        in_specs=[pl.BlockSpec((nrows, d), lambda i: (row0 // nrows, 0))],
        out_specs=pl.BlockSpec((d, LANES), lambda i: (0, 0)),
        out_shape=jax.ShapeDtypeStruct((d, LANES), BF16),
        compiler_params=_params(("arbitrary",)), name="cast_tail")(w_t)


def _norm_mod(x, g, scale, shift):
    ms = jnp.mean(x * x, axis=-1, keepdims=True)
    y = x * lax.rsqrt(ms + EPS) * g
    return y * (1.0 + scale) + shift


def _inpj_kernel(nf, nk, x_ref, g_ref, sc_ref, sh_ref, w_ref, wf_ref, o_ref, ok_ref, ovt_ref, of_ref, h_scr):
    j = pl.program_id(1)

    @pl.when(j == 0)
    def _():
        hb = _norm_mod(x_ref[...], g_ref[...], sc_ref[...], sh_ref[...]).astype(BF16)
        h_scr[...] = hb
        of_ref[...] = _dot(hb, wf_ref[...])

    tn = w_ref.shape[1]
    piece = tn // 2

    def pieces(store):
        for c0 in range(0, tn, piece):
            store(c0, _dot(h_scr[...], w_ref[:, c0:c0 + piece]))

    @pl.when(j < nf)
    def _():
        def store(c0, r):
            o_ref[:, c0:c0 + piece] = r
        pieces(store)

    @pl.when(jnp.logical_and(j >= nf, j < nf + nk))
    def _():
        def store(c0, r):
            ok_ref[:, c0:c0 + piece] = r.astype(BF16)
        pieces(store)

    @pl.when(j >= nf + nk)
    def _():
        def store(c0, r):
            ovt_ref[0, c0:c0 + piece, :] = r.T.astype(BF16)
        pieces(store)


def in_proj(x, g, scale, shift, w_main, w_ff, *, tm=512, tn=1024):
    s, d = x.shape
    n = w_main.shape[1]
    nf, nk = N_F32_COLS // tn, FOX_WIDTH // tn
    vec = pl.BlockSpec((1, d), lambda i, j: (0, 0))
    return pl.pallas_call(
        functools.partial(_inpj_kernel, nf, nk),
        grid=(s // tm, n // tn),
        in_specs=[pl.BlockSpec((tm, d), lambda i, j: (i, 0)), vec, vec, vec,
                  pl.BlockSpec((d, tn), lambda i, j: (0, j)),
                  pl.BlockSpec((d, LANES), lambda i, j: (0, 0))],
        out_specs=[pl.BlockSpec((tm, tn), lambda i, j: (i, jnp.minimum(j, nf - 1))),
                   pl.BlockSpec((tm, tn), lambda i, j: (i, jnp.clip(j - nf, 0, nk - 1))),
                   pl.BlockSpec((1, tn, tm), lambda i, j: (i, jnp.clip(j - nf - nk, 0, nk - 1), 0)),
                   pl.BlockSpec((tm, LANES), lambda i, j: (i, 0))],
        out_shape=[jax.ShapeDtypeStruct((s, N_F32_COLS), F32),
                   jax.ShapeDtypeStruct((s, FOX_WIDTH), BF16),
                   jax.ShapeDtypeStruct((s // tm, FOX_WIDTH, tm), BF16),
                   jax.ShapeDtypeStruct((s, LANES), F32)],
        scratch_shapes=[pltpu.VMEM((tm, d), BF16)],
        compiler_params=_params(("arbitrary", "arbitrary")),
        name="inpj",
    )(x, g, scale, shift, w_main, w_ff)


def _split3(v):
    hi = v.astype(BF16)
    r1 = v - hi.astype(F32)
    mid = r1.astype(BF16)
    lo = (r1 - mid.astype(F32)).astype(BF16)
    return hi, mid, lo


def _cum_place():
    m = np.zeros((3 * LANES, FOX_HEADS * LANES), np.float32)
    for p in range(3):
        for h in range(FOX_HEADS):
            m[p * LANES + h, h * LANES + p] = 1.0
    return m


def _cum_kernel(ff_ref, b_ref, place_ref, o_ref, carry):
    @pl.when(pl.program_id(0) == 0)
    def _():
        carry[...] = jnp.zeros_like(carry)

    z = ff_ref[...] + b_ref[...]
    logf = jnp.minimum(z, 0.0) - jnp.log(1.0 + jnp.exp(-jnp.abs(z)))
    tc = z.shape[0]
    row = lax.broadcasted_iota(jnp.int32, (tc, tc), 0)
    col = lax.broadcasted_iota(jnp.int32, (tc, tc), 1)
    tri = jnp.where(row >= col, 1.0, 0.0).astype(BF16)
    hi, mid, lo = _split3(logf)
    cum = _dot(tri, hi) + _dot(tri, mid) + _dot(tri, lo) + carry[...]
    carry[...] = cum[tc - 1:tc, :]
    pieces = jnp.concatenate(_split3(cum * LOG2E), axis=1)
    o_ref[...] = _dot(pieces, place_ref[...]).astype(BF16)


def cum_gate(ff, b128, *, tc=256):
    s = ff.shape[0]
    place = jnp.asarray(_cum_place(), BF16)
    return pl.pallas_call(
        _cum_kernel,
        grid=(s // tc,),
        in_specs=[pl.BlockSpec((tc, LANES), lambda i: (i, 0)),
                  pl.BlockSpec((1, LANES), lambda i: (0, 0)),
                  pl.BlockSpec(place.shape, lambda i: (0, 0))],
        out_specs=pl.BlockSpec((tc, FOX_HEADS * LANES), lambda i: (i, 0)),
        out_shape=jax.ShapeDtypeStruct((s, FOX_HEADS * LANES), BF16),
        scratch_shapes=[pltpu.VMEM((1, LANES), F32)],
        compiler_params=_params(("arbitrary",)),
        name="cum",
    )(ff, b128, place)


def _ret_consts():
    h = np.arange(RET_HEADS, dtype=np.float32)
    log_g = np.log1p(-np.exp2(-5.0 - h)).astype(np.float32)
    i = np.arange(RET_CHUNK, dtype=np.float32)
    diff = i[:, None] - i[None, :]
    dmat = np.where(diff >= 0, np.exp(np.maximum(diff, 0.0)[None] * log_g[:, None, None]), 0.0)
    kdec = np.exp((RET_CHUNK - 1.0 - i)[None, :] * log_g[:, None])
    qdec = np.exp((i + 1.0)[None, :] * log_g[:, None])
    cdec = np.exp(RET_CHUNK * log_g)
    kdec = np.broadcast_to(kdec[:, :, None], (RET_HEADS, RET_CHUNK, RET_DK))
    qdec = np.broadcast_to(qdec[:, :, None], (RET_HEADS, RET_CHUNK, RET_DK))
    half = RET_DK // 2
    inv = (ROPE_BASE ** (-np.arange(half, dtype=np.float32) / half)).astype(np.float32)
    inv_full = np.concatenate([inv, inv])[None, :]
    sign = np.concatenate([-np.ones(half, np.float32), np.ones(half, np.float32)])[None, :]
    return (dmat.astype(np.float32), np.ascontiguousarray(kdec, np.float32),
            np.ascontiguousarray(qdec, np.float32), [float(v) for v in cdec],
            inv_full.astype(np.float32), sign)


def _ret_kernel(cdec, rq_ref, rk_ref, rv_ref, rg_ref, pos_ref, inv_ref, sign_ref,
                dmat_ref, kdec_ref, qdec_ref, gn_ref, o_ref, state):
    @pl.when(pl.program_id(0) == 0)
    def _():
        state[...] = jnp.zeros_like(state)

    ang = pos_ref[...].astype(F32) * inv_ref[...]
    cosf = jnp.cos(ang)
    sinf = jnp.sin(ang) * sign_ref[...]
    kscale = RET_DK ** -0.5
    for h in range(RET_HEADS):
        q = rq_ref[:, h * RET_DK:(h + 1) * RET_DK]
        k = rk_ref[:, h * RET_DK:(h + 1) * RET_DK]
        qr = q * cosf + pltpu.roll(q, RET_DK // 2, 1) * sinf
        kr = (k * cosf + pltpu.roll(k, RET_DK // 2, 1) * sinf) * kscale
        vb = rv_ref[:, h * RET_DV:(h + 1) * RET_DV].astype(BF16)
        scores = _dot_nt(qr.astype(BF16), kr.astype(BF16)) * dmat_ref[h]
        intra = _dot(scores.astype(BF16), vb)
        st = state[h]
        cross = _dot((qr * qdec_ref[h]).astype(BF16), st.astype(BF16))
        kd = (kr * kdec_ref[h]).astype(BF16)
        kv = lax.dot_general(kd, vb, _TN, preferred_element_type=F32)
        state[h] = st * cdec[h] + kv
        y = intra + cross
        mu = jnp.mean(y, axis=-1, keepdims=True)
        dlt = y - mu
        var = jnp.mean(dlt * dlt, axis=-1, keepdims=True)
        g = rg_ref[:, h * RET_DV:(h + 1) * RET_DV]
        yn = dlt * lax.rsqrt(var + EPS) * gn_ref[:, h * RET_DV:(h + 1) * RET_DV] * (g / (1.0 + jnp.exp(-g)))
        o_ref[:, h * RET_DV:(h + 1) * RET_DV] = yn.astype(BF16)


def retention(proj, pos_col, gn_g):
    s = proj.shape[0]
    c = RET_CHUNK
    dmat, kdec, qdec, cdec, inv_full, sign = _ret_consts()
    full3 = lambda n: (0, 0, 0)
    return pl.pallas_call(
        functools.partial(_ret_kernel, cdec),
        grid=(s // c,),
        in_specs=[pl.BlockSpec((c, RET_QK), lambda n: (n, 0)),
                  pl.BlockSpec((c, RET_QK), lambda n: (n, 1)),
                  pl.BlockSpec((c, RET_WIDTH), lambda n: (n, 1)),
                  pl.BlockSpec((c, RET_WIDTH), lambda n: (n, 2)),
                  pl.BlockSpec((c, 1), lambda n: (n, 0)),
                  pl.BlockSpec((1, RET_DK), lambda n: (0, 0)),
                  pl.BlockSpec((1, RET_DK), lambda n: (0, 0)),
                  pl.BlockSpec((RET_HEADS, c, c), full3),
                  pl.BlockSpec((RET_HEADS, c, RET_DK), full3),
                  pl.BlockSpec((RET_HEADS, c, RET_DK), full3),
                  pl.BlockSpec((1, RET_WIDTH), lambda n: (0, 0))],
        out_specs=pl.BlockSpec((c, RET_WIDTH), lambda n: (n, 0)),
        out_shape=jax.ShapeDtypeStruct((s, RET_WIDTH), BF16),
        scratch_shapes=[pltpu.VMEM((RET_HEADS, RET_DK, RET_DV), F32)],
        compiler_params=_params(("arbitrary",)),
        name="ret",
    )(proj, proj, proj, proj, pos_col, jnp.asarray(inv_full), jnp.asarray(sign),
      jnp.asarray(dmat), jnp.asarray(kdec), jnp.asarray(qdec), gn_g)


def _fox_kernel(q_ref, k_ref, c_ref, vt_ref, o_ref, m_scr, l_scr, acc_scr, mx_scr, *bufs):
    i = pl.program_id(1)
    bq = q_ref.shape[0]
    bk = bufs[0].shape[0]
    nd = len(bufs)
    assert nd * bk == bq
    sub = 8
    lane = lax.broadcasted_iota(jnp.int32, (bq, FOX_DH), 1)
    qa = jnp.where(lane < 3, -1.0, 0.0).astype(BF16)
    q2 = jnp.concatenate([(q_ref[...] * (FOX_DH ** -0.5 * LOG2E)).astype(BF16), qa], axis=1)
    m_scr[...] = jnp.full_like(m_scr, NEG_BIG)
    l_scr[...] = jnp.zeros_like(l_scr)
    acc_scr[...] = jnp.zeros_like(acc_scr)
    ones_rows = jnp.where(lax.broadcasted_iota(jnp.int32, (2 * sub, bk), 0) == 0, 1.0, 0.0).astype(BF16)

    def logits(j, slot, c0=0):
        off = pl.multiple_of(j * bk, bk)
        k2 = jnp.concatenate([k_ref[pl.ds(off, bk), :], c_ref[pl.ds(off, bk), :]], axis=1)
        st = _dot_nt(k2, q2[c0:bq, :])
        bufs[slot][:, c0:bq] = st
        mx_scr[slot, :, c0:bq] = jnp.max(jnp.max(st.reshape(bk // sub, sub, bq - c0), axis=0), axis=0, keepdims=True)

    def update_cols(j, slot, c0, c1, diagonal):
        w = c1 - c0
        st = bufs[slot][:, c0:c1].reshape(bk // sub, sub, w)
        if diagonal:
            kid = lax.broadcasted_iota(jnp.int32, (bk, bk), 0).reshape(bk // sub, sub, bk)
            qid = lax.broadcasted_iota(jnp.int32, (bk, bk), 1).reshape(bk // sub, sub, bk)
            st = jnp.where(kid <= qid, st, NEG_BIG)
            blk_max = jnp.max(jnp.max(st, axis=0), axis=0, keepdims=True)
        else:
            blk_max = mx_scr[slot, :, c0:c1]
        m_old = m_scr[:, c0:c1]
        m_new = jnp.maximum(m_old, blk_max)
        alpha = jnp.exp2(m_old - m_new)
        p = jnp.exp2(st - jnp.broadcast_to(m_new, (sub, w))[None])
        pv = _dot(jnp.concatenate([vt_ref[j], ones_rows], axis=0), p.reshape(bk, w).astype(BF16))
        l_scr[:, c0:c1] = alpha * l_scr[:, c0:c1] + pv[FOX_DH:FOX_DH + 1, :]
        a8 = jnp.broadcast_to(alpha, (sub, w))[None]
        acc = acc_scr[:, c0:c1].reshape(FOX_DH // sub, sub, w)
        acc_scr[:, c0:c1] = (a8 * acc).reshape(FOX_DH, w) + pv[0:FOX_DH, :]
        m_scr[:, c0:c1] = m_new

    def update(j, slot, c0=0, diagonal=False):
        if diagonal:
            update_cols(j, slot, c0, c0 + bk, True)
            c0 += bk
        if c0 < bq:
            update_cols(j, slot, c0, bq, False)

    ahead = 2
    for k in range(ahead):
        logits(k, k)

    def body(jj, carry):
        for k in range(nd):
            logits(nd * jj + k + ahead, (k + ahead) % nd)
            update(nd * jj + k, k)
        return carry

    lax.fori_loop(0, i, body, 0)
    for k in range(nd):
        if k + ahead < nd:
            logits(nd * i + k + ahead, k + ahead, (k + ahead) * bk)
        update(nd * i + k, k, k * bk, diagonal=True)
    o_ref[...] = (acc_scr[...] / l_scr[...]).T


def fox_attention(pf32, kb, caug, vt3, *, bq=2048):
    s = pf32.shape[0]
    bk = vt3.shape[2]
    assert vt3.shape == (s // bk, FOX_WIDTH, bk) and bq % (2 * bk) == 0
    qcol0 = (N_F32_COLS - FOX_WIDTH) // FOX_DH
    return pl.pallas_call(
        _fox_kernel,
        grid=(FOX_HEADS, s // bq),
        in_specs=[pl.BlockSpec((bq, FOX_DH), lambda h, i: (i, qcol0 + h)),
                  pl.BlockSpec((s, FOX_DH), lambda h, i: (0, h)),
                  pl.BlockSpec((s, LANES), lambda h, i: (0, h)),
                  pl.BlockSpec((s // bk, FOX_DH, bk), lambda h, i: (0, h, 0))],
        out_specs=pl.BlockSpec((bq, FOX_DH), lambda h, i: (i, h)),
        out_shape=jax.ShapeDtypeStruct((s, FOX_WIDTH), F32),
        scratch_shapes=[pltpu.VMEM((1, bq), F32), pltpu.VMEM((1, bq), F32), pltpu.VMEM((FOX_DH, bq), F32),
                        pltpu.VMEM((bq // bk, 1, bq), F32)]
        + [pltpu.VMEM((bk, bq), F32)] * (bq // bk),
        compiler_params=_params(("arbitrary", "arbitrary")),
        name="fox",
    )(pf32, kb, caug, vt3)


def _outp_kernel(yr_ref, yf_ref, fg_ref, w_ref, x_ref, g1_ref, o_ref, y_scr):
    @pl.when(pl.program_id(1) == 0)
    def _():
        yf = yf_ref[...]
        ms = jnp.mean(yf * yf, axis=-1, keepdims=True)
        yn = yf * lax.rsqrt(ms + EPS) * fg_ref[...]
        y_scr[:, 0:RET_WIDTH] = yr_ref[...]
        y_scr[:, RET_WIDTH:RET_WIDTH + FOX_WIDTH] = yn.astype(BF16)

    tn = w_ref.shape[1]
    piece = tn // 2
    for c0 in range(0, tn, piece):
        cs = slice(c0, c0 + piece)
        o_ref[:, cs] = x_ref[:, cs] + g1_ref[:, cs] * _dot(y_scr[...], w_ref[:, cs])


def out_proj(y_ret, y_fox, fox_g, w_out, x, gate1, *, tm=512, tn=1024):
    s, d = x.shape
    kdim = RET_WIDTH + FOX_WIDTH
    return pl.pallas_call(
        _outp_kernel,
        grid=(s // tm, d // tn),
        in_specs=[pl.BlockSpec((tm, RET_WIDTH), lambda i, j: (i, 0)),
                  pl.BlockSpec((tm, FOX_WIDTH), lambda i, j: (i, 0)),
                  pl.BlockSpec((1, FOX_WIDTH), lambda i, j: (0, 0)),
                  pl.BlockSpec((kdim, tn), lambda i, j: (0, j)),
                  pl.BlockSpec((tm, tn), lambda i, j: (i, j)),
                  pl.BlockSpec((1, tn), lambda i, j: (0, j))],
        out_specs=pl.BlockSpec((tm, tn), lambda i, j: (i, j)),
        out_shape=jax.ShapeDtypeStruct((s, d), F32),
        scratch_shapes=[pltpu.VMEM((tm, kdim), BF16)],
        compiler_params=_params(("arbitrary", "arbitrary")),
        name="outp",
    )(y_ret, y_fox, fox_g, w_out, x, gate1)


def _drop_max(cur):
    mx = jnp.max(jnp.max(cur, axis=0), axis=0, keepdims=True)
    hit = cur == jnp.broadcast_to(mx, cur.shape[1:])[None]
    return mx, hit, jnp.where(hit, -jnp.inf, cur)


def _top_rows(s, t_scr, ls):
    cur = s.reshape(s.shape[0] // 8, 8, s.shape[1])
    level = jnp.full(cur.shape, float(PEER_TOPK), F32)
    for r in range(PEER_TOPK):
        mx, hit, cur = _drop_max(cur)
        t_scr[r:r + 1, ls] = mx
        level = jnp.where(hit, float(r), level)
    return level.reshape(s.shape)


def _peerq_kernel(group, x_ref, g_ref, sc_ref, sh_ref, wq0_ref, wqn_ref, keys_ref, u_ref, v_ref,
                  h2t_ref, kc_ref, e1_ref, l2_ref, e2_ref, ub_ref, vtb_ref,
                  h_scr, qb_scr, s1_scr, s2_scr, t1_scr, t2_scr, cand_scr):
    head = pl.program_id(1)
    step = pl.program_id(0) * pl.num_programs(1) + head
    cur = head % 2

    @pl.when(step % group == 0)
    def _():
        ub_ref[...] = u_ref[...].astype(BF16)
        vtb_ref[...] = v_ref[...].T.astype(BF16)

    @pl.when(pl.program_id(1) == 0)
    def _():
        h = _norm_mod(x_ref[...], g_ref[...], sc_ref[...], sh_ref[...])
        h_scr[...] = h.astype(BF16)
        h2t_ref[...] = h.T.astype(BF16)
        qb_scr[0] = _dot(h_scr[...], wq0_ref[...]).astype(BF16)

    half = PEER_DQ // 2
    qb = qb_scr[cur]
    s1 = _dot_nt(keys_ref[0, 0], qb[:, 0:half])
    s2 = _dot_nt(keys_ref[0, 1], qb[:, half:PEER_DQ])
    s1_scr[...] = s1
    s2_scr[...] = s2
    qb_scr[1 - cur] = _dot(h_scr[...], wqn_ref[...]).astype(BF16)
    for g in range(s1.shape[1] // LANES):
        ls = slice(g * LANES, (g + 1) * LANES)
        s1l = s1_scr[:, ls]
        s2l = s2_scr[:, ls]
        _top_rows(s1l, t1_scr, ls)
        l2_ref[0, :, ls] = _top_rows(s2l, t2_scr, ls).astype(BF16)
        t1_lo = t1_scr[0:8, ls]
        for b in range(8):
            cand_scr[b * 8:(b + 1) * 8, ls] = t1_lo + t2_scr[b:b + 1, ls]
        cand_scr[64:72, ls] = t1_scr[0:1, ls] + t2_scr[8:16, ls]
        cand_scr[72:80, ls] = t1_scr[8:16, ls] + t2_scr[0:1, ls]
        cand = cand_scr[:, ls].reshape(PEER_NCAND // 8, 8, LANES)
        cur = cand
        for r in range(PEER_TOPK):
            tau, _, cur = _drop_max(cur)
        m1 = t1_scr[0:1, ls]
        m2 = t2_scr[0:1, ls]
        top8 = jnp.broadcast_to(m1 + m2, (8, LANES))[None]
        tau8 = jnp.broadcast_to(tau, (8, LANES))[None]
        z = jnp.sum(jnp.sum(jnp.where(cand >= tau8, jnp.exp(cand - top8), 0.0), axis=0), axis=0, keepdims=True)
        s1g = s1l.reshape(PEER_NKEYS // 8, 8, LANES)
        kc = jnp.zeros(s1g.shape, F32)
        for b in range(PEER_TOPK):
            t2b = jnp.broadcast_to(t2_scr[b:b + 1, ls], (8, LANES))[None]
            kc = jnp.where(s1g + t2b >= tau8, float(b + 1), kc)
        kc_ref[0, :, ls] = kc.reshape(PEER_NKEYS, LANES)
        e1_ref[0, :, ls] = jnp.exp(s1l - (m1 + jnp.log(z)))
        e2_ref[0, :, ls] = jnp.exp(s2l - m2).astype(BF16)


def peer_query(x1, g, scale, shift, wq, keys, u, v, *, tm=512):
    s, d = x1.shape
    ne = u.shape[0]
    nsteps = (s // tm) * PEER_HEADS
    per_step = ne // nsteps
    group = max(1, LANES // per_step)
    rows = per_step * group
    assert rows * (nsteps // group) == ne and rows % LANES == 0
    vec = pl.BlockSpec((1, d), lambda i, h: (0, 0))
    stat = pl.BlockSpec((1, PEER_NKEYS, tm), lambda i, h: (h, 0, i))
    table = lambda dt: jax.ShapeDtypeStruct((PEER_HEADS, PEER_NKEYS, s), dt)
    eblk = lambda i, h: (i * PEER_HEADS + h) // group
    return pl.pallas_call(
        functools.partial(_peerq_kernel, group),
        grid=(s // tm, PEER_HEADS),
        in_specs=[pl.BlockSpec((tm, d), lambda i, h: (i, 0)), vec, vec, vec,
                  pl.BlockSpec((d, PEER_DQ), lambda i, h: (0, 0)),
                  pl.BlockSpec((d, PEER_DQ), lambda i, h: (0, jnp.minimum(h + 1, PEER_HEADS - 1))),
                  pl.BlockSpec((1, 2, PEER_NKEYS, PEER_DQ // 2), lambda i, h: (h, 0, 0, 0)),
                  pl.BlockSpec((rows, d), lambda i, h: (eblk(i, h), 0)),
                  pl.BlockSpec((rows, d), lambda i, h: (eblk(i, h), 0))],
        out_specs=[pl.BlockSpec((d, tm), lambda i, h: (0, i)), stat, stat, stat, stat,
                   pl.BlockSpec((rows, d), lambda i, h: (eblk(i, h), 0)),
                   pl.BlockSpec((d, rows), lambda i, h: (0, eblk(i, h)))],
        out_shape=[jax.ShapeDtypeStruct((d, s), BF16), table(F32), table(F32), table(BF16), table(BF16),
                   jax.ShapeDtypeStruct((ne, d), BF16), jax.ShapeDtypeStruct((d, ne), BF16)],
        scratch_shapes=[pltpu.VMEM((tm, d), BF16), pltpu.VMEM((2, tm, PEER_DQ), BF16),
                        pltpu.VMEM((PEER_NKEYS, tm), F32), pltpu.VMEM((PEER_NKEYS, tm), F32),
                        pltpu.VMEM((PEER_TOPK, tm), F32), pltpu.VMEM((PEER_TOPK, tm), F32),
                        pltpu.VMEM((PEER_NCAND, tm), F32)],
        compiler_params=_params(("arbitrary", "arbitrary")),
        name="peerq",
    )(x1, g, scale, shift, wq, wq, keys, u, v)


_GELU_C = 0.7978845608028654


def _gelu_tanh(x):
    k1 = -2.0 * _GELU_C * LOG2E
    z2 = x * (k1 + (k1 * 0.044715) * (x * x))
    return x / (1.0 + jnp.exp2(z2))


def _peer_kernel(nblk, h2t_ref, u_ref, vt_ref, kc_ref, e1_ref, l2_ref, e2_ref, x_ref, g2_ref, fg_ref, o_ref,
                 a_scr, w_scr, acc_scr):
    t = pl.program_id(0)
    tm = h2t_ref.shape[1]
    tn = u_ref.shape[0]
    rows = 16
    nrow = tn // PEER_NKEYS
    cur = t % 2
    prev = jnp.maximum(t - 1, 0)
    slot = (prev // nblk) % 2

    @pl.when(prev % nblk == 0)
    def _():
        acc_scr[slot] = jnp.zeros(acc_scr.shape[1:], F32)

    @pl.when(t == 0)
    def _():
        w_scr[1] = jnp.zeros(w_scr.shape[1:], BF16)

    done = t - nblk - 1
    for piece in range(tm // LANES):
        @pl.when(jnp.logical_and(done >= 0, done % nblk == piece))
        def _(piece=piece):
            o_t = acc_scr[(done // nblk) % 2, :, piece * LANES:(piece + 1) * LANES]
            x2 = x_ref[...] + g2_ref[...] * o_t.T
            ms = jnp.mean(x2 * x2, axis=-1, keepdims=True)
            o_ref[...] = x2 * lax.rsqrt(ms + EPS) * fg_ref[...]

    for e0 in range(0, tn, ACT_PIECE):
        a_scr[e0:e0 + ACT_PIECE, :] = _dot(u_ref[e0:e0 + ACT_PIECE, :], h2t_ref[...])
    acc_scr[slot] += _dot(vt_ref[...], w_scr[1 - cur])
    base = lax.shift_right_logical(t, 30) * rows

    for c in range(nrow):
        i1 = (t % nblk) * nrow + c
        krow = [jnp.broadcast_to(kc_ref[h, pl.ds(i1, 1), :], (rows, tm)).astype(BF16) for h in range(PEER_HEADS)]
        e1row = [jnp.broadcast_to(e1_ref[h, pl.ds(i1, 1), :], (rows, tm)).astype(BF16) for h in range(PEER_HEADS)]
        for r0 in range(0, PEER_NKEYS, rows):
            gate = None
            for h in range(PEER_HEADS):
                gv = jnp.where(l2_ref[h, r0:r0 + rows, :] < krow[h], e2_ref[h, r0:r0 + rows, :] * e1row[h],
                               jnp.zeros((), BF16))
                gate = gv if gate is None else gate + gv
            rr = c * PEER_NKEYS + r0
            a_rows = a_scr[pl.ds(pl.multiple_of(base + rr, rows), rows), :]
            w_scr[cur, rr:rr + rows, :] = _gelu_tanh(a_rows.astype(BF16)) * gate


def peer_experts(h2t, u, vt, kc, e1, l2, e2, x1, gate2, final_g, *, tm=512, tn=512):
    d, s = h2t.shape
    ne = u.shape[0]
    nblk, ntile, npiece = ne // tn, s // tm, tm // LANES
    assert nblk > npiece, "a tile's output pieces must finish before its accumulator is reused"
    once = pl.Buffered(1)
    tile_a = lambda t: jnp.minimum(t // nblk, ntile - 1)
    prev = lambda t: jnp.maximum(t - 1, 0)

    def out_rows(t):
        done = jnp.maximum(t - nblk - 1, 0)
        return (done // nblk) * npiece + jnp.minimum(done % nblk, npiece - 1), 0

    table = pl.BlockSpec((PEER_HEADS, PEER_NKEYS, tm), lambda t: (0, 0, tile_a(t)), pipeline_mode=once)
    vec = pl.BlockSpec((1, d), lambda t: (0, 0))
    return pl.pallas_call(
        functools.partial(_peer_kernel, nblk),
        grid=(ntile * nblk + 1 + npiece,),
        in_specs=[pl.BlockSpec((d, tm), lambda t: (0, tile_a(t)), pipeline_mode=once),
                  pl.BlockSpec((tn, d), lambda t: (t % nblk, 0)),
                  pl.BlockSpec((d, tn), lambda t: (0, prev(t) % nblk)),
                  table, table, table, table,
                  pl.BlockSpec((LANES, d), out_rows), vec, vec],
        out_specs=pl.BlockSpec((LANES, d), out_rows),
        out_shape=jax.ShapeDtypeStruct((s, d), F32),
        scratch_shapes=[pltpu.VMEM((tn, tm), F32), pltpu.VMEM((2, tn, tm), BF16), pltpu.VMEM((2, d, tm), F32)],
        compiler_params=_params(("arbitrary",)),
        name="peer",
    )(h2t, u, vt, kc, e1, l2, e2, x1, gate2, final_g)


def _fin_kernel(x_ref, pt_ref, g2_ref, fg_ref, o_ref):
    x2 = x_ref[...] + g2_ref[...] * pt_ref[...].T
    ms = jnp.mean(x2 * x2, axis=-1, keepdims=True)
    o_ref[...] = x2 * lax.rsqrt(ms + EPS) * fg_ref[...]


def final_norm(x1, peer_t, gate2, final_g, *, tm=256):
    s, d = x1.shape
    vec = pl.BlockSpec((1, d), lambda i: (0, 0))
    return pl.pallas_call(
        _fin_kernel,
        grid=(s // tm,),
        in_specs=[pl.BlockSpec((tm, d), lambda i: (i, 0)),
                  pl.BlockSpec((d, tm), lambda i: (0, i)), vec, vec],
        out_specs=pl.BlockSpec((tm, d), lambda i: (i, 0)),
        out_shape=jax.ShapeDtypeStruct((s, d), F32),
        compiler_params=_params(("arbitrary",)),
        name="fin",
    )(x1, peer_t, gate2, final_g)


def _layer(x, mod, positions, norm1_g, w_in, b_forget, ret_gn_g, fox_norm_g, w_out, norm2_g,
           w_peer_q, peer_sub_keys, peer_u, peer_v, final_g, *, bq=2048, bk=512):
    s, d = x.shape
    shift1, scale1, gate1, shift2, scale2, gate2 = [mod[:, k * d:(k + 1) * d] for k in range(6)]
    row = lambda v: v.reshape(1, -1)

    w_t = w_in.T
    w_main = cast_bf16(w_t, rows=IN_MAIN, transpose=True, br=2048, bc=512, name="cast_in")
    n_ff = w_in.shape[1] - IN_MAIN
    w_ff = tail_weight(w_t, IN_MAIN, n_ff)
    pf32, kb, vt3, ff = in_proj(x, row(norm1_g), scale1, shift1, w_main, w_ff, tm=bk)

    b128 = jnp.pad(b_forget, (0, LANES - n_ff)).reshape(1, LANES)
    caug = cum_gate(ff, b128)
    y_ret = retention(pf32, positions.reshape(s, 1), row(ret_gn_g))
    y_fox = fox_attention(pf32, kb, caug, vt3, bq=bq)
    x1 = out_proj(y_ret, y_fox, row(fox_norm_g), cast_bf16(w_out, name="cast_out"), x, gate1)

    h2t, kc, e1, l2, e2, ub, vtb = peer_query(x1, row(norm2_g), scale2, shift2, cast_bf16(w_peer_q, name="cast_q"),
                                              peer_sub_keys.astype(BF16), peer_u, peer_v)
    return peer_experts(h2t, ub, vtb, kc, e1, l2, e2, x1, gate2, row(final_g))


def kernel(x, c, positions, w_ada, b_ada, norm1_g, w_in, b_forget, ret_gn_g, fox_norm_g, w_out,
           norm2_g, w_peer_q, peer_sub_keys, peer_u, peer_v, final_g):
    b, s, d = x.shape
    assert b == 1 and w_ada.shape[0] == 1, "one sequence, one layer"
    c8 = jnp.broadcast_to(c, (8, d))
    mod = ada_mod(c8, w_ada[0], b_ada[0].reshape(1, -1))[0:1]
    return _layer(x[0], mod, positions[0], norm1_g[0], w_in[0], b_forget[0], ret_gn_g[0],
                  fox_norm_g[0], w_out[0], norm2_g[0], w_peer_q[0], peer_sub_keys[0],
                  peer_u[0], peer_v[0], final_g)[None]
```

```python
import functools

import numpy as np
import jax
import jax.numpy as jnp
from jax import lax
from jax.experimental import pallas as pl
from jax.experimental.pallas import tpu as pltpu

F32 = jnp.float32
BF16 = jnp.bfloat16

LANES = 128
RET_HEADS = 8
RET_DK = 128
RET_DV = 256
RET_QK = RET_HEADS * RET_DK
RET_WIDTH = RET_HEADS * RET_DV
RET_CHUNK = 128
FOX_HEADS = 16
FOX_DH = 128
FOX_WIDTH = FOX_HEADS * FOX_DH
IN_MAIN = 2 * RET_QK + 2 * RET_WIDTH + 3 * FOX_WIDTH
N_F32_COLS = 2 * RET_QK + 2 * RET_WIDTH + FOX_WIDTH
ROPE_BASE = 10000.0
PEER_HEADS = 8
PEER_NKEYS = 128
PEER_DQ = 256
PEER_TOPK = 16
PEER_NCAND = 80
ACT_PIECE = 128
EPS = 1e-6
NEG_BIG = -1e30
LOG2E = 1.4426950408889634
VMEM_LIMIT = 60 * 1024 * 1024

_NT = (((1,), (1,)), ((), ()))
_TN = (((0,), (0,)), ((), ()))


def _params(sem):
    return pltpu.CompilerParams(dimension_semantics=sem, vmem_limit_bytes=VMEM_LIMIT)


def _dot(a, b):
    return jnp.dot(a, b, preferred_element_type=F32)


def _dot_nt(a, b):
    return lax.dot_general(a, b, _NT, preferred_element_type=F32)


def _ada_kernel(c_ref, w_ref, b_ref, o_ref):
    c = c_ref[...]
    ca = (c / (1.0 + jnp.exp(-c))).astype(BF16)
    o_ref[...] = _dot(ca, w_ref[...].astype(BF16)) + b_ref[...]


def ada_mod(c8, w, b, *, bn=512):
    d, n = w.shape
    return pl.pallas_call(
        _ada_kernel,
        grid=(n // bn,),
        in_specs=[pl.BlockSpec((8, d), lambda j: (0, 0)),
                  pl.BlockSpec((d, bn), lambda j: (0, j)),
                  pl.BlockSpec((1, bn), lambda j: (0, j))],
        out_specs=pl.BlockSpec((8, bn), lambda j: (0, j)),
        out_shape=jax.ShapeDtypeStruct((8, n), F32),
        compiler_params=_params(("arbitrary",)),
        name="ada",
    )(c8, w, b)


def _cast_kernel(x_ref, o_ref):
    o_ref[...] = x_ref[...].astype(BF16)


def _cast_t_kernel(x_ref, o_ref):
    o_ref[...] = x_ref[...].T.astype(BF16)


def cast_bf16(x, *, rows=None, cols=None, br=512, bc=2048, transpose=False, name="cast"):
    r, c = x.shape
    r = r if rows is None else rows
    c = c if cols is None else cols
    br, bc = min(br, r), min(bc, c)
    if transpose:
        return pl.pallas_call(
            _cast_t_kernel, grid=(r // br, c // bc),
            in_specs=[pl.BlockSpec((br, bc), lambda i, j: (i, j))],
            out_specs=pl.BlockSpec((bc, br), lambda i, j: (j, i)),
            out_shape=jax.ShapeDtypeStruct((c, r), BF16),
            compiler_params=_params(("arbitrary", "arbitrary")), name=name)(x)
    return pl.pallas_call(
        _cast_kernel, grid=(r // br, c // bc),
        in_specs=[pl.BlockSpec((br, bc), lambda i, j: (i, j))],
        out_specs=pl.BlockSpec((br, bc), lambda i, j: (i, j)),
        out_shape=jax.ShapeDtypeStruct((r, c), BF16),
        compiler_params=_params(("arbitrary", "arbitrary")), name=name)(x)


def _tail_kernel(x_ref, o_ref):
    o_ref[...] = jnp.zeros_like(o_ref)
    o_ref[:, 0:x_ref.shape[0]] = x_ref[...].T.astype(BF16)


def tail_weight(w_t, row0, nrows):
    d = w_t.shape[1]
    assert row0 % nrows == 0 and nrows % 8 == 0
    return pl.pallas_call(
        _tail_kernel, grid=(1,),
        in_specs=[pl.BlockSpec((nrows, d), lambda i: (row0 // nrows, 0))],
        out_specs=pl.BlockSpec((d, LANES), lambda i: (0, 0)),
        out_shape=jax.ShapeDtypeStruct((d, LANES), BF16),
        compiler_params=_params(("arbitrary",)), name="cast_tail")(w_t)


def _norm_mod(x, g, scale, shift):
    ms = jnp.mean(x * x, axis=-1, keepdims=True)
    y = x * lax.rsqrt(ms + EPS) * g
    return y * (1.0 + scale) + shift


def _inpj_kernel(nf, nk, x_ref, g_ref, sc_ref, sh_ref, w_ref, wf_ref, o_ref, ok_ref, ovt_ref, of_ref, h_scr):
    j = pl.program_id(1)

    @pl.when(j == 0)
    def _():
        hb = _norm_mod(x_ref[...], g_ref[...], sc_ref[...], sh_ref[...]).astype(BF16)
        h_scr[...] = hb
        of_ref[...] = _dot(hb, wf_ref[...])

    tn = w_ref.shape[1]
    piece = tn // 2

    def pieces(store):
        for c0 in range(0, tn, piece):
            store(c0, _dot(h_scr[...], w_ref[:, c0:c0 + piece]))

    @pl.when(j < nf)
    def _():
        def store(c0, r):
            o_ref[:, c0:c0 + piece] = r
        pieces(store)

    @pl.when(jnp.logical_and(j >= nf, j < nf + nk))
    def _():
        def store(c0, r):
            ok_ref[:, c0:c0 + piece] = r.astype(BF16)
        pieces(store)

    @pl.when(j >= nf + nk)
    def _():
        def store(c0, r):
            ovt_ref[0, c0:c0 + piece, :] = r.T.astype(BF16)
        pieces(store)


def in_proj(x, g, scale, shift, w_main, w_ff, *, tm=512, tn=1024):
    s, d = x.shape
    n = w_main.shape[1]
    nf, nk = N_F32_COLS // tn, FOX_WIDTH // tn
    vec = pl.BlockSpec((1, d), lambda i, j: (0, 0))
    return pl.pallas_call(
        functools.partial(_inpj_kernel, nf, nk),
        grid=(s // tm, n // tn),
        in_specs=[pl.BlockSpec((tm, d), lambda i, j: (i, 0)), vec, vec, vec,
                  pl.BlockSpec((d, tn), lambda i, j: (0, j)),
                  pl.BlockSpec((d, LANES), lambda i, j: (0, 0))],
        out_specs=[pl.BlockSpec((tm, tn), lambda i, j: (i, jnp.minimum(j, nf - 1))),
                   pl.BlockSpec((tm, tn), lambda i, j: (i, jnp.clip(j - nf, 0, nk - 1))),
                   pl.BlockSpec((1, tn, tm), lambda i, j: (i, jnp.clip(j - nf - nk, 0, nk - 1), 0)),
                   pl.BlockSpec((tm, LANES), lambda i, j: (i, 0))],
        out_shape=[jax.ShapeDtypeStruct((s, N_F32_COLS), F32),
                   jax.ShapeDtypeStruct((s, FOX_WIDTH), BF16),
                   jax.ShapeDtypeStruct((s // tm, FOX_WIDTH, tm), BF16),
                   jax.ShapeDtypeStruct((s, LANES), F32)],
        scratch_shapes=[pltpu.VMEM((tm, d), BF16)],
        compiler_params=_params(("arbitrary", "arbitrary")),
        name="inpj",
    )(x, g, scale, shift, w_main, w_ff)


def _split3(v):
    hi = v.astype(BF16)
    r1 = v - hi.astype(F32)
    mid = r1.astype(BF16)
    lo = (r1 - mid.astype(F32)).astype(BF16)
    return hi, mid, lo


def _cum_place():
    m = np.zeros((3 * LANES, FOX_HEADS * LANES), np.float32)
    for p in range(3):
        for h in range(FOX_HEADS):
            m[p * LANES + h, h * LANES + p] = 1.0
    return m


def _cum_kernel(ff_ref, b_ref, place_ref, o_ref, carry):
    @pl.when(pl.program_id(0) == 0)
    def _():
        carry[...] = jnp.zeros_like(carry)

    z = ff_ref[...] + b_ref[...]
    logf = jnp.minimum(z, 0.0) - jnp.log(1.0 + jnp.exp(-jnp.abs(z)))
    tc = z.shape[0]
    row = lax.broadcasted_iota(jnp.int32, (tc, tc), 0)
    col = lax.broadcasted_iota(jnp.int32, (tc, tc), 1)
    tri = jnp.where(row >= col, 1.0, 0.0).astype(BF16)
    hi, mid, lo = _split3(logf)
    cum = _dot(tri, hi) + _dot(tri, mid) + _dot(tri, lo) + carry[...]
    carry[...] = cum[tc - 1:tc, :]
    pieces = jnp.concatenate(_split3(cum * LOG2E), axis=1)
    o_ref[...] = _dot(pieces, place_ref[...]).astype(BF16)


def cum_gate(ff, b128, *, tc=256):
    s = ff.shape[0]
    place = jnp.asarray(_cum_place(), BF16)
    return pl.pallas_call(
        _cum_kernel,
        grid=(s // tc,),
        in_specs=[pl.BlockSpec((tc, LANES), lambda i: (i, 0)),
                  pl.BlockSpec((1, LANES), lambda i: (0, 0)),
                  pl.BlockSpec(place.shape, lambda i: (0, 0))],
        out_specs=pl.BlockSpec((tc, FOX_HEADS * LANES), lambda i: (i, 0)),
        out_shape=jax.ShapeDtypeStruct((s, FOX_HEADS * LANES), BF16),
        scratch_shapes=[pltpu.VMEM((1, LANES), F32)],
        compiler_params=_params(("arbitrary",)),
        name="cum",
    )(ff, b128, place)


def _ret_consts():
    h = np.arange(RET_HEADS, dtype=np.float32)
    log_g = np.log1p(-np.exp2(-5.0 - h)).astype(np.float32)
    i = np.arange(RET_CHUNK, dtype=np.float32)
    diff = i[:, None] - i[None, :]
    dmat = np.where(diff >= 0, np.exp(np.maximum(diff, 0.0)[None] * log_g[:, None, None]), 0.0)
    kdec = np.exp((RET_CHUNK - 1.0 - i)[None, :] * log_g[:, None])
    qdec = np.exp((i + 1.0)[None, :] * log_g[:, None])
    cdec = np.exp(RET_CHUNK * log_g)
    kdec = np.broadcast_to(kdec[:, :, None], (RET_HEADS, RET_CHUNK, RET_DK))
    qdec = np.broadcast_to(qdec[:, :, None], (RET_HEADS, RET_CHUNK, RET_DK))
    half = RET_DK // 2
    inv = (ROPE_BASE ** (-np.arange(half, dtype=np.float32) / half)).astype(np.float32)
    inv_full = np.concatenate([inv, inv])[None, :]
    sign = np.concatenate([-np.ones(half, np.float32), np.ones(half, np.float32)])[None, :]
    return (dmat.astype(np.float32), np.ascontiguousarray(kdec, np.float32),
            np.ascontiguousarray(qdec, np.float32), [float(v) for v in cdec],
            inv_full.astype(np.float32), sign)


def _ret_kernel(cdec, rq_ref, rk_ref, rv_ref, rg_ref, pos_ref, inv_ref, sign_ref,
                dmat_ref, kdec_ref, qdec_ref, gn_ref, o_ref, state):
    @pl.when(pl.program_id(0) == 0)
    def _():
        state[...] = jnp.zeros_like(state)

    ang = pos_ref[...].astype(F32) * inv_ref[...]
    cosf = jnp.cos(ang)
    sinf = jnp.sin(ang) * sign_ref[...]
    kscale = RET_DK ** -0.5
    for h in range(RET_HEADS):
        q = rq_ref[:, h * RET_DK:(h + 1) * RET_DK]
        k = rk_ref[:, h * RET_DK:(h + 1) * RET_DK]
        qr = q * cosf + pltpu.roll(q, RET_DK // 2, 1) * sinf
        kr = (k * cosf + pltpu.roll(k, RET_DK // 2, 1) * sinf) * kscale
        vb = rv_ref[:, h * RET_DV:(h + 1) * RET_DV].astype(BF16)
        scores = _dot_nt(qr.astype(BF16), kr.astype(BF16)) * dmat_ref[h]
        intra = _dot(scores.astype(BF16), vb)
        st = state[h]
        cross = _dot((qr * qdec_ref[h]).astype(BF16), st.astype(BF16))
        kd = (kr * kdec_ref[h]).astype(BF16)
        kv = lax.dot_general(kd, vb, _TN, preferred_element_type=F32)
        state[h] = st * cdec[h] + kv
        y = intra + cross
        mu = jnp.mean(y, axis=-1, keepdims=True)
        dlt = y - mu
        var = jnp.mean(dlt * dlt, axis=-1, keepdims=True)
        g = rg_ref[:, h * RET_DV:(h + 1) * RET_DV]
        yn = dlt * lax.rsqrt(var + EPS) * gn_ref[:, h * RET_DV:(h + 1) * RET_DV] * (g / (1.0 + jnp.exp(-g)))
        o_ref[:, h * RET_DV:(h + 1) * RET_DV] = yn.astype(BF16)


def retention(proj, pos_col, gn_g):
    s = proj.shape[0]
    c = RET_CHUNK
    dmat, kdec, qdec, cdec, inv_full, sign = _ret_consts()
    full3 = lambda n: (0, 0, 0)
    return pl.pallas_call(
        functools.partial(_ret_kernel, cdec),
        grid=(s // c,),
        in_specs=[pl.BlockSpec((c, RET_QK), lambda n: (n, 0)),
                  pl.BlockSpec((c, RET_QK), lambda n: (n, 1)),
                  pl.BlockSpec((c, RET_WIDTH), lambda n: (n, 1)),
                  pl.BlockSpec((c, RET_WIDTH), lambda n: (n, 2)),
                  pl.BlockSpec((c, 1), lambda n: (n, 0)),
                  pl.BlockSpec((1, RET_DK), lambda n: (0, 0)),
                  pl.BlockSpec((1, RET_DK), lambda n: (0, 0)),
                  pl.BlockSpec((RET_HEADS, c, c), full3),
                  pl.BlockSpec((RET_HEADS, c, RET_DK), full3),
                  pl.BlockSpec((RET_HEADS, c, RET_DK), full3),
                  pl.BlockSpec((1, RET_WIDTH), lambda n: (0, 0))],
        out_specs=pl.BlockSpec((c, RET_WIDTH), lambda n: (n, 0)),
        out_shape=jax.ShapeDtypeStruct((s, RET_WIDTH), BF16),
        scratch_shapes=[pltpu.VMEM((RET_HEADS, RET_DK, RET_DV), F32)],
        compiler_params=_params(("arbitrary",)),
        name="ret",
    )(proj, proj, proj, proj, pos_col, jnp.asarray(inv_full), jnp.asarray(sign),
      jnp.asarray(dmat), jnp.asarray(kdec), jnp.asarray(qdec), gn_g)


def _fox_kernel(q_ref, k_ref, c_ref, vt_ref, o_ref, m_scr, l_scr, acc_scr, mx_scr, *bufs):
    i = pl.program_id(1)
    bq = q_ref.shape[0]
    bk = bufs[0].shape[0]
    nd = len(bufs)
    assert nd * bk == bq
    sub = 8
    lane = lax.broadcasted_iota(jnp.int32, (bq, FOX_DH), 1)
    qa = jnp.where(lane < 3, -1.0, 0.0).astype(BF16)
    q2 = jnp.concatenate([(q_ref[...] * (FOX_DH ** -0.5 * LOG2E)).astype(BF16), qa], axis=1)
    m_scr[...] = jnp.full_like(m_scr, NEG_BIG)
    l_scr[...] = jnp.zeros_like(l_scr)
    acc_scr[...] = jnp.zeros_like(acc_scr)
    ones_rows = jnp.where(lax.broadcasted_iota(jnp.int32, (2 * sub, bk), 0) == 0, 1.0, 0.0).astype(BF16)

    def logits(j, slot, c0=0):
        off = pl.multiple_of(j * bk, bk)
        k2 = jnp.concatenate([k_ref[pl.ds(off, bk), :], c_ref[pl.ds(off, bk), :]], axis=1)
        st = _dot_nt(k2, q2[c0:bq, :])
        bufs[slot][:, c0:bq] = st
        mx_scr[slot, :, c0:bq] = jnp.max(jnp.max(st.reshape(bk // sub, sub, bq - c0), axis=0), axis=0, keepdims=True)

    def update_cols(j, slot, c0, c1, diagonal):
        w = c1 - c0
        st = bufs[slot][:, c0:c1].reshape(bk // sub, sub, w)
        if diagonal:
            kid = lax.broadcasted_iota(jnp.int32, (bk, bk), 0).reshape(bk // sub, sub, bk)
            qid = lax.broadcasted_iota(jnp.int32, (bk, bk), 1).reshape(bk // sub, sub, bk)
            st = jnp.where(kid <= qid, st, NEG_BIG)
            blk_max = jnp.max(jnp.max(st, axis=0), axis=0, keepdims=True)
        else:
            blk_max = mx_scr[slot, :, c0:c1]
        m_old = m_scr[:, c0:c1]
        m_new = jnp.maximum(m_old, blk_max)
        alpha = jnp.exp2(m_old - m_new)
        p = jnp.exp2(st - jnp.broadcast_to(m_new, (sub, w))[None])
        pv = _dot(jnp.concatenate([vt_ref[j], ones_rows], axis=0), p.reshape(bk, w).astype(BF16))
        l_scr[:, c0:c1] = alpha * l_scr[:, c0:c1] + pv[FOX_DH:FOX_DH + 1, :]
        a8 = jnp.broadcast_to(alpha, (sub, w))[None]
        acc = acc_scr[:, c0:c1].reshape(FOX_DH // sub, sub, w)
        acc_scr[:, c0:c1] = (a8 * acc).reshape(FOX_DH, w) + pv[0:FOX_DH, :]
        m_scr[:, c0:c1] = m_new

    def update(j, slot, c0=0, diagonal=False):
        if diagonal:
            update_cols(j, slot, c0, c0 + bk, True)
            c0 += bk
        if c0 < bq:
            update_cols(j, slot, c0, bq, False)

    ahead = 2
    for k in range(ahead):
        logits(k, k)

    def body(jj, carry):
        for k in range(nd):
            logits(nd * jj + k + ahead, (k + ahead) % nd)
            update(nd * jj + k, k)
        return carry

    lax.fori_loop(0, i, body, 0)
    for k in range(nd):
        if k + ahead < nd:
            logits(nd * i + k + ahead, k + ahead, (k + ahead) * bk)
        update(nd * i + k, k, k * bk, diagonal=True)
    o_ref[...] = (acc_scr[...] / l_scr[...]).T


def fox_attention(pf32, kb, caug, vt3, *, bq=2048):
    s = pf32.shape[0]
    bk = vt3.shape[2]
    assert vt3.shape == (s // bk, FOX_WIDTH, bk) and bq % (2 * bk) == 0
    qcol0 = (N_F32_COLS - FOX_WIDTH) // FOX_DH
    return pl.pallas_call(
        _fox_kernel,
        grid=(FOX_HEADS, s // bq),
        in_specs=[pl.BlockSpec((bq, FOX_DH), lambda h, i: (i, qcol0 + h)),
                  pl.BlockSpec((s, FOX_DH), lambda h, i: (0, h)),
                  pl.BlockSpec((s, LANES), lambda h, i: (0, h)),
                  pl.BlockSpec((s // bk, FOX_DH, bk), lambda h, i: (0, h, 0))],
        out_specs=pl.BlockSpec((bq, FOX_DH), lambda h, i: (i, h)),
        out_shape=jax.ShapeDtypeStruct((s, FOX_WIDTH), F32),
        scratch_shapes=[pltpu.VMEM((1, bq), F32), pltpu.VMEM((1, bq), F32), pltpu.VMEM((FOX_DH, bq), F32),
                        pltpu.VMEM((bq // bk, 1, bq), F32)]
        + [pltpu.VMEM((bk, bq), F32)] * (bq // bk),
        compiler_params=_params(("arbitrary", "arbitrary")),
        name="fox",
    )(pf32, kb, caug, vt3)


def _outp_kernel(yr_ref, yf_ref, fg_ref, w_ref, x_ref, g1_ref, o_ref, y_scr):
    @pl.when(pl.program_id(1) == 0)
    def _():
        yf = yf_ref[...]
        ms = jnp.mean(yf * yf, axis=-1, keepdims=True)
        yn = yf * lax.rsqrt(ms + EPS) * fg_ref[...]
        y_scr[:, 0:RET_WIDTH] = yr_ref[...]
        y_scr[:, RET_WIDTH:RET_WIDTH + FOX_WIDTH] = yn.astype(BF16)

    tn = w_ref.shape[1]
    piece = tn // 2
    for c0 in range(0, tn, piece):
        cs = slice(c0, c0 + piece)
        o_ref[:, cs] = x_ref[:, cs] + g1_ref[:, cs] * _dot(y_scr[...], w_ref[:, cs])


def out_proj(y_ret, y_fox, fox_g, w_out, x, gate1, *, tm=512, tn=1024):
    s, d = x.shape
    kdim = RET_WIDTH + FOX_WIDTH
    return pl.pallas_call(
        _outp_kernel,
        grid=(s // tm, d // tn),
        in_specs=[pl.BlockSpec((tm, RET_WIDTH), lambda i, j: (i, 0)),
                  pl.BlockSpec((tm, FOX_WIDTH), lambda i, j: (i, 0)),
                  pl.BlockSpec((1, FOX_WIDTH), lambda i, j: (0, 0)),
                  pl.BlockSpec((kdim, tn), lambda i, j: (0, j)),
                  pl.BlockSpec((tm, tn), lambda i, j: (i, j)),
                  pl.BlockSpec((1, tn), lambda i, j: (0, j))],
        out_specs=pl.BlockSpec((tm, tn), lambda i, j: (i, j)),
        out_shape=jax.ShapeDtypeStruct((s, d), F32),
        scratch_shapes=[pltpu.VMEM((tm, kdim), BF16)],
        compiler_params=_params(("arbitrary", "arbitrary")),
        name="outp",
    )(y_ret, y_fox, fox_g, w_out, x, gate1)


def _drop_max(cur):
    mx = jnp.max(jnp.max(cur, axis=0), axis=0, keepdims=True)
    hit = cur == jnp.broadcast_to(mx, cur.shape[1:])[None]
    return mx, hit, jnp.where(hit, -jnp.inf, cur)


def _top_rows(s, t_scr, ls):
    cur = s.reshape(s.shape[0] // 8, 8, s.shape[1])
    level = jnp.full(cur.shape, float(PEER_TOPK), F32)
    for r in range(PEER_TOPK):
        mx, hit, cur = _drop_max(cur)
        t_scr[r:r + 1, ls] = mx
        level = jnp.where(hit, float(r), level)
    return level.reshape(s.shape)


def _peerq_kernel(group, x_ref, g_ref, sc_ref, sh_ref, wq0_ref, wqn_ref, keys_ref, u_ref, v_ref,
                  h2t_ref, kc_ref, e1_ref, l2_ref, e2_ref, ub_ref, vtb_ref,
                  h_scr, qb_scr, s1_scr, s2_scr, t1_scr, t2_scr, cand_scr):
    head = pl.program_id(1)
    step = pl.program_id(0) * pl.num_programs(1) + head
    cur = head % 2

    @pl.when(step % group == 0)
    def _():
        ub_ref[...] = u_ref[...].astype(BF16)
        vtb_ref[...] = v_ref[...].T.astype(BF16)

    @pl.when(pl.program_id(1) == 0)
    def _():
        h = _norm_mod(x_ref[...], g_ref[...], sc_ref[...], sh_ref[...])
        h_scr[...] = h.astype(BF16)
        h2t_ref[...] = h.T.astype(BF16)
        qb_scr[0] = _dot(h_scr[...], wq0_ref[...]).astype(BF16)

    half = PEER_DQ // 2
    qb = qb_scr[cur]
    s1 = _dot_nt(keys_ref[0, 0], qb[:, 0:half])
    s2 = _dot_nt(keys_ref[0, 1], qb[:, half:PEER_DQ])
    s1_scr[...] = s1
    s2_scr[...] = s2
    qb_scr[1 - cur] = _dot(h_scr[...], wqn_ref[...]).astype(BF16)
    for g in range(s1.shape[1] // LANES):
        ls = slice(g * LANES, (g + 1) * LANES)
        s1l = s1_scr[:, ls]
        s2l = s2_scr[:, ls]
        _top_rows(s1l, t1_scr, ls)
        l2_ref[0, :, ls] = _top_rows(s2l, t2_scr, ls).astype(BF16)
        t1_lo = t1_scr[0:8, ls]
        for b in range(8):
            cand_scr[b * 8:(b + 1) * 8, ls] = t1_lo + t2_scr[b:b + 1, ls]
        cand_scr[64:72, ls] = t1_scr[0:1, ls] + t2_scr[8:16, ls]
        cand_scr[72:80, ls] = t1_scr[8:16, ls] + t2_scr[0:1, ls]
        cand = cand_scr[:, ls].reshape(PEER_NCAND // 8, 8, LANES)
        cur = cand
        for r in range(PEER_TOPK):
            tau, _, cur = _drop_max(cur)
        m1 = t1_scr[0:1, ls]
        m2 = t2_scr[0:1, ls]
        top8 = jnp.broadcast_to(m1 + m2, (8, LANES))[None]
        tau8 = jnp.broadcast_to(tau, (8, LANES))[None]
        z = jnp.sum(jnp.sum(jnp.where(cand >= tau8, jnp.exp(cand - top8), 0.0), axis=0), axis=0, keepdims=True)
        s1g = s1l.reshape(PEER_NKEYS // 8, 8, LANES)
        kc = jnp.zeros(s1g.shape, F32)
        for b in range(PEER_TOPK):
            t2b = jnp.broadcast_to(t2_scr[b:b + 1, ls], (8, LANES))[None]
            kc = jnp.where(s1g + t2b >= tau8, float(b + 1), kc)
        kc_ref[0, :, ls] = kc.reshape(PEER_NKEYS, LANES)
        e1_ref[0, :, ls] = jnp.exp(s1l - (m1 + jnp.log(z)))
        e2_ref[0, :, ls] = jnp.exp(s2l - m2).astype(BF16)


def peer_query(x1, g, scale, shift, wq, keys, u, v, *, tm=512):
    s, d = x1.shape
    ne = u.shape[0]
    nsteps = (s // tm) * PEER_HEADS
    per_step = ne // nsteps
    group = max(1, LANES // per_step)
    rows = per_step * group
    assert rows * (nsteps // group) == ne and rows % LANES == 0
    vec = pl.BlockSpec((1, d), lambda i, h: (0, 0))
    stat = pl.BlockSpec((1, PEER_NKEYS, tm), lambda i, h: (h, 0, i))
    table = lambda dt: jax.ShapeDtypeStruct((PEER_HEADS, PEER_NKEYS, s), dt)
    eblk = lambda i, h: (i * PEER_HEADS + h) // group
    return pl.pallas_call(
        functools.partial(_peerq_kernel, group),
        grid=(s // tm, PEER_HEADS),
        in_specs=[pl.BlockSpec((tm, d), lambda i, h: (i, 0)), vec, vec, vec,
                  pl.BlockSpec((d, PEER_DQ), lambda i, h: (0, 0)),
                  pl.BlockSpec((d, PEER_DQ), lambda i, h: (0, jnp.minimum(h + 1, PEER_HEADS - 1))),
                  pl.BlockSpec((1, 2, PEER_NKEYS, PEER_DQ // 2), lambda i, h: (h, 0, 0, 0)),
                  pl.BlockSpec((rows, d), lambda i, h: (eblk(i, h), 0)),
                  pl.BlockSpec((rows, d), lambda i, h: (eblk(i, h), 0))],
        out_specs=[pl.BlockSpec((d, tm), lambda i, h: (0, i)), stat, stat, stat, stat,
                   pl.BlockSpec((rows, d), lambda i, h: (eblk(i, h), 0)),
                   pl.BlockSpec((d, rows), lambda i, h: (0, eblk(i, h)))],
        out_shape=[jax.ShapeDtypeStruct((d, s), BF16), table(F32), table(F32), table(BF16), table(BF16),
                   jax.ShapeDtypeStruct((ne, d), BF16), jax.ShapeDtypeStruct((d, ne), BF16)],
        scratch_shapes=[pltpu.VMEM((tm, d), BF16), pltpu.VMEM((2, tm, PEER_DQ), BF16),
                        pltpu.VMEM((PEER_NKEYS, tm), F32), pltpu.VMEM((PEER_NKEYS, tm), F32),
                        pltpu.VMEM((PEER_TOPK, tm), F32), pltpu.VMEM((PEER_TOPK, tm), F32),
                        pltpu.VMEM((PEER_NCAND, tm), F32)],
        compiler_params=_params(("arbitrary", "arbitrary")),
        name="peerq",
    )(x1, g, scale, shift, wq, wq, keys, u, v)


_GELU_C = 0.7978845608028654


def _gelu_tanh(x):
    k1 = -2.0 * _GELU_C * LOG2E
    z2 = x * (k1 + (k1 * 0.044715) * (x * x))
    return x / (1.0 + jnp.exp2(z2))


def _peer_kernel(nblk, h2t_ref, u_ref, vt_ref, kc_ref, e1_ref, l2_ref, e2_ref, x_ref, g2_ref, fg_ref, o_ref,
                 a_scr, w_scr, acc_scr):
    t = pl.program_id(0)
    tm = h2t_ref.shape[1]
    tn = u_ref.shape[0]
    rows = 16
    nrow = tn // PEER_NKEYS
    cur = t % 2
    prev = jnp.maximum(t - 1, 0)
    slot = (prev // nblk) % 2

    @pl.when(prev % nblk == 0)
    def _():
        acc_scr[slot] = jnp.zeros(acc_scr.shape[1:], F32)

    @pl.when(t == 0)
    def _():
        w_scr[1] = jnp.zeros(w_scr.shape[1:], BF16)

    done = t - nblk - 1
    for piece in range(tm // LANES):
        @pl.when(jnp.logical_and(done >= 0, done % nblk == piece))
        def _(piece=piece):
            o_t = acc_scr[(done // nblk) % 2, :, piece * LANES:(piece + 1) * LANES]
            x2 = x_ref[...] + g2_ref[...] * o_t.T
            ms = jnp.mean(x2 * x2, axis=-1, keepdims=True)
            o_ref[...] = x2 * lax.rsqrt(ms + EPS) * fg_ref[...]

    for e0 in range(0, tn, ACT_PIECE):
        a_scr[e0:e0 + ACT_PIECE, :] = _dot(u_ref[e0:e0 + ACT_PIECE, :], h2t_ref[...])
    acc_scr[slot] += _dot(vt_ref[...], w_scr[1 - cur])
    base = lax.shift_right_logical(t, 30) * rows

    for c in range(nrow):
        i1 = (t % nblk) * nrow + c
        krow = [jnp.broadcast_to(kc_ref[h, pl.ds(i1, 1), :], (rows, tm)).astype(BF16) for h in range(PEER_HEADS)]
        e1row = [jnp.broadcast_to(e1_ref[h, pl.ds(i1, 1), :], (rows, tm)).astype(BF16) for h in range(PEER_HEADS)]
        for r0 in range(0, PEER_NKEYS, rows):
            gate = None
            for h in range(PEER_HEADS):
                gv = jnp.where(l2_ref[h, r0:r0 + rows, :] < krow[h], e2_ref[h, r0:r0 + rows, :] * e1row[h],
                               jnp.zeros((), BF16))
                gate = gv if gate is None else gate + gv
            rr = c * PEER_NKEYS + r0
            a_rows = a_scr[pl.ds(pl.multiple_of(base + rr, rows), rows), :]
            w_scr[cur, rr:rr + rows, :] = _gelu_tanh(a_rows.astype(BF16)) * gate


def peer_experts(h2t, u, vt, kc, e1, l2, e2, x1, gate2, final_g, *, tm=512, tn=512):
    d, s = h2t.shape
    ne = u.shape[0]
    nblk, ntile, npiece = ne // tn, s // tm, tm // LANES
    assert nblk > npiece, "a tile's output pieces must finish before its accumulator is reused"
    once = pl.Buffered(1)
    tile_a = lambda t: jnp.minimum(t // nblk, ntile - 1)
    prev = lambda t: jnp.maximum(t - 1, 0)

    def out_rows(t):
        done = jnp.maximum(t - nblk - 1, 0)
        return (done // nblk) * npiece + jnp.minimum(done % nblk, npiece - 1), 0

    table = pl.BlockSpec((PEER_HEADS, PEER_NKEYS, tm), lambda t: (0, 0, tile_a(t)), pipeline_mode=once)
    vec = pl.BlockSpec((1, d), lambda t: (0, 0))
    return pl.pallas_call(
        functools.partial(_peer_kernel, nblk),
        grid=(ntile * nblk + 1 + npiece,),
        in_specs=[pl.BlockSpec((d, tm), lambda t: (0, tile_a(t)), pipeline_mode=once),
                  pl.BlockSpec((tn, d), lambda t: (t % nblk, 0)),
                  pl.BlockSpec((d, tn), lambda t: (0, prev(t) % nblk)),
                  table, table, table, table,
                  pl.BlockSpec((LANES, d), out_rows), vec, vec],
        out_specs=pl.BlockSpec((LANES, d), out_rows),
        out_shape=jax.ShapeDtypeStruct((s, d), F32),
        scratch_shapes=[pltpu.VMEM((tn, tm), F32), pltpu.VMEM((2, tn, tm), BF16), pltpu.VMEM((2, d, tm), F32)],
        compiler_params=_params(("arbitrary",)),
        name="peer",
    )(h2t, u, vt, kc, e1, l2, e2, x1, gate2, final_g)


def _layer(x, mod, positions, norm1_g, w_in, b_forget, ret_gn_g, fox_norm_g, w_out, norm2_g,
           w_peer_q, peer_sub_keys, peer_u, peer_v, final_g, *, bq=2048, bk=512):
    s, d = x.shape
    shift1, scale1, gate1, shift2, scale2, gate2 = [mod[:, k * d:(k + 1) * d] for k in range(6)]
    row = lambda v: v.reshape(1, -1)

    w_t = w_in.T
    w_main = cast_bf16(w_t, rows=IN_MAIN, transpose=True, br=2048, bc=512, name="cast_in")
    n_ff = w_in.shape[1] - IN_MAIN
    w_ff = tail_weight(w_t, IN_MAIN, n_ff)
    pf32, kb, vt3, ff = in_proj(x, row(norm1_g), scale1, shift1, w_main, w_ff, tm=bk)

    b128 = jnp.pad(b_forget, (0, LANES - n_ff)).reshape(1, LANES)
    caug = cum_gate(ff, b128)
    y_ret = retention(pf32, positions.reshape(s, 1), row(ret_gn_g))
    y_fox = fox_attention(pf32, kb, caug, vt3, bq=bq)
    x1 = out_proj(y_ret, y_fox, row(fox_norm_g), cast_bf16(w_out, name="cast_out"), x, gate1)

    h2t, kc, e1, l2, e2, ub, vtb = peer_query(x1, row(norm2_g), scale2, shift2, cast_bf16(w_peer_q, name="cast_q"),
                                              peer_sub_keys.astype(BF16), peer_u, peer_v)
    return peer_experts(h2t, ub, vtb, kc, e1, l2, e2, x1, gate2, row(final_g))


def kernel(x, c, positions, w_ada, b_ada, norm1_g, w_in, b_forget, ret_gn_g, fox_norm_g, w_out,
           norm2_g, w_peer_q, peer_sub_keys, peer_u, peer_v, final_g):
    b, s, d = x.shape
    assert b == 1 and w_ada.shape[0] == 1, "one sequence, one layer"
    c8 = jnp.broadcast_to(c, (8, d))
    mod = ada_mod(c8, w_ada[0], b_ada[0].reshape(1, -1))[0:1]
    return _layer(x[0], mod, positions[0], norm1_g[0], w_in[0], b_forget[0], ret_gn_g[0],
                  fox_norm_g[0], w_out[0], norm2_g[0], w_peer_q[0], peer_sub_keys[0],
                  peer_u[0], peer_v[0], final_g)[None]
```
